```python
import jax, jax.numpy as jnp
from jax import lax
import numpy as np

D_MODEL = 1024
BATCH = 8
SEQ = 4096
DEPTH = 1

CHUNK = 64
LEFT_CHUNKS = 8
BAND = (LEFT_CHUNKS + 1) * CHUNK
N_HEADS = 8
HEAD_DIM = 64
ATTN_W = N_HEADS * HEAD_DIM
CONV_W = D_MODEL // 2
CONV_K = 3
MAX_REL_PAST = 256
NUM_REL = (CHUNK - 1) + MAX_REL_PAST + 1
PLE_DIM = 256
N_GROUPS = 4
EXPERTS_PER_GROUP = 8
N_EXPERTS = N_GROUPS * EXPERTS_PER_GROUP
TOP_K = 2
D_EXPERT = 512
BLK = 128
EPS = 1e-6

IN_WIDTHS = [ATTN_W, ATTN_W, ATTN_W, CONV_W, CONV_W, CONV_W, D_MODEL, D_MODEL]
IN_TOTAL = sum(IN_WIDTHS)
SPLITS = [int(v) for v in np.cumsum(IN_WIDTHS)[:-1]]

kernel_name = "hybrid_chunked_attn_shortconv_hmoe_block"


def rms_norm(x, g):
    xf = x.astype(jnp.float32)
    y = xf * lax.rsqrt(jnp.mean(xf * xf, axis=-1, keepdims=True) + EPS)
    return (y * g.astype(jnp.float32)).astype(x.dtype)


def chunked_rel_attention(q, k, v, g_q, g_k, rel_bias):
    b, s, h, dh = q.shape
    nc = s // CHUNK
    pad = LEFT_CHUNKS * CHUNK
    q = rms_norm(q, g_q)
    k = rms_norm(k, g_k)
    kp = jnp.pad(k, ((0, 0), (pad, 0), (0, 0), (0, 0)))
    vp = jnp.pad(v, ((0, 0), (pad, 0), (0, 0), (0, 0)))
    dist = jnp.arange(CHUNK)[:, None] - jnp.arange(BAND)[None, :] + pad
    idx = jnp.clip(dist, -(CHUNK - 1), MAX_REL_PAST) + (CHUNK - 1)
    bias = rel_bias[:, idx].astype(jnp.float32)
    qc = q.reshape(b, nc, CHUNK, h, dh).transpose(1, 0, 2, 3, 4)
    scale = dh ** -0.5

    def one_chunk(args):
        q_blk, c = args
        start = c * CHUNK
        kb = lax.dynamic_slice_in_dim(kp, start, BAND, axis=1)
        vb = lax.dynamic_slice_in_dim(vp, start, BAND, axis=1)
        sc = jnp.einsum('bqhd,bkhd->bhqk', q_blk, kb).astype(jnp.float32) * scale + bias
        valid = (start - pad + jnp.arange(BAND)) >= 0
        sc = jnp.where(valid[None, None, None, :], sc, -1e30)
        pr = jax.nn.softmax(sc, axis=-1).astype(vb.dtype)
        return jnp.einsum('bhqk,bkhd->bqhd', pr, vb)

    out = lax.map(one_chunk, (qc, jnp.arange(nc)))
    return out.transpose(1, 0, 2, 3, 4).reshape(b, s, h * dh)


def short_conv(u, bg, cg, w, bias):
    s = u.shape[1]
    zp = jnp.pad(cg * u, ((0, 0), (CONV_K - 1, 0), (0, 0)))
    y = bias
    for j in range(CONV_K):
        y = y + w[j] * zp[:, j:j + s]
    return bg * y


def hier_moe(xn, w_group, b_group, w_router, b_router, w1, w3, w2):
    t, d = xn.shape
    gl = (xn @ w_group).astype(jnp.float32) + b_group
    grp = jnp.argmax(gl, axis=-1)
    p_grp = jnp.take_along_axis(jax.nn.softmax(gl, axis=-1), grp[:, None], axis=1)
    el = ((xn @ w_router).astype(jnp.float32) + b_router).reshape(t, N_GROUPS, EXPERTS_PER_GROUP)
    el = jnp.take_along_axis(el, grp[:, None, None], axis=1)[:, 0]
    top_l, top_i = lax.top_k(el, TOP_K)
    wgt = (p_grp * jax.nn.softmax(top_l, axis=-1)).reshape(-1)
    eid = (grp[:, None] * EXPERTS_PER_GROUP + top_i).reshape(-1).astype(jnp.int32)
    tok = jnp.repeat(jnp.arange(t, dtype=jnp.int32), TOP_K)
    a = t * TOP_K
    order = jnp.argsort(eid)
    se = eid[order]
    counts = jnp.bincount(eid, length=N_EXPERTS)
    pcounts = (counts + BLK - 1) // BLK * BLK
    pends = jnp.cumsum(pcounts)
    pstarts = pends - pcounts
    starts = jnp.cumsum(counts) - counts
    dest = pstarts[se] + jnp.arange(a) - starts[se]
    n_rows = (a + BLK - 1) // BLK * BLK + N_EXPERTS * BLK
    n_blk = n_rows // BLK
    row_tok = jnp.zeros((n_rows,), jnp.int32).at[dest].set(tok[order])
    row_w = jnp.zeros((n_rows,), wgt.dtype).at[dest].set(wgt[order])
    blk_e = jnp.minimum(jnp.searchsorted(pends, jnp.arange(n_blk) * BLK, side='right'), N_EXPERTS - 1)
    xs = xn[row_tok].reshape(n_blk, BLK, d)

    def expert_block(args):
        xb, e = args
        hdn = jax.nn.silu(xb @ w1[e]) * (xb @ w3[e])
        return hdn @ w2[e]

    ys = lax.map(expert_block, (xs, blk_e)).reshape(n_rows, d)
    return jnp.zeros((t, d), xn.dtype).at[row_tok].add(ys * row_w[:, None].astype(ys.dtype))


def setup_inputs(seed: int = 0) -> dict:
    key = jax.random.key(seed)
    ks = jax.random.split(key, 32)
    nrm = lambda k, shape, sc: jax.random.normal(k, shape, jnp.float32) * sc
    L = DEPTH
    return {
        "x": nrm(ks[0], (BATCH, SEQ, D_MODEL), 1.0),
        "p": nrm(ks[1], (DEPTH, BATCH, SEQ, PLE_DIM), 1.0),
        "g_mix": 1.0 + nrm(ks[2], (L, D_MODEL), 0.02),
        "w_in": nrm(ks[3], (L, D_MODEL, IN_TOTAL), D_MODEL ** -0.5),
        "b_in": nrm(ks[4], (L, IN_TOTAL), 0.02),
        "g_q": 1.0 + nrm(ks[5], (L, HEAD_DIM), 0.02),
        "g_k": 1.0 + nrm(ks[6], (L, HEAD_DIM), 0.02),
        "rel_bias": nrm(ks[7], (L, N_HEADS, NUM_REL), 0.1),
        "conv_w": nrm(ks[8], (L, CONV_K, CONV_W), CONV_K ** -0.5),
        "conv_b": nrm(ks[9], (L, CONV_W), 0.02),
        "w_pa": nrm(ks[10], (L, ATTN_W, D_MODEL), ATTN_W ** -0.5),
        "w_pc": nrm(ks[11], (L, CONV_W, D_MODEL), CONV_W ** -0.5),
        "w_o": nrm(ks[12], (L, D_MODEL, D_MODEL), D_MODEL ** -0.5),
        "g_ffn": 1.0 + nrm(ks[13], (L, D_MODEL), 0.02),
        "w_group": nrm(ks[14], (L, D_MODEL, N_GROUPS), D_MODEL ** -0.5),
        "b_group": nrm(ks[15], (L, N_GROUPS), 0.01),
        "w_router": nrm(ks[16], (L, D_MODEL, N_EXPERTS), D_MODEL ** -0.5),
        "b_router": nrm(ks[17], (L, N_EXPERTS), 0.01),
        "w1": nrm(ks[18], (L, N_EXPERTS, D_MODEL, D_EXPERT), D_MODEL ** -0.5),
        "w3": nrm(ks[19], (L, N_EXPERTS, D_MODEL, D_EXPERT), D_MODEL ** -0.5),
        "w2": nrm(ks[20], (L, N_EXPERTS, D_EXPERT, D_MODEL), D_EXPERT ** -0.5),
        "g_ple": 1.0 + nrm(ks[21], (L, D_MODEL), 0.02),
        "w_ple_gate": nrm(ks[22], (L, D_MODEL, D_MODEL), D_MODEL ** -0.5),
        "b_ple_gate": nrm(ks[23], (L, D_MODEL), 0.02),
        "w_ple_proj": nrm(ks[24], (L, PLE_DIM, D_MODEL), PLE_DIM ** -0.5),
    }


def reference(x, p, g_mix, w_in, b_in, g_q, g_k, rel_bias, conv_w, conv_b, w_pa, w_pc, w_o,
              g_ffn, w_group, b_group, w_router, b_router, w1, w3, w2,
              g_ple, w_ple_gate, b_ple_gate, w_ple_proj):
    b, s, d = x.shape
    h = x
    for i in range(DEPTH):
        n = rms_norm(h, g_mix[i])
        z = n @ w_in[i] + b_in[i]
        q, k, v, u, bg, cg, ga, gc = jnp.split(z, SPLITS, axis=-1)
        ya = chunked_rel_attention(q.reshape(b, s, N_HEADS, HEAD_DIM),
                                   k.reshape(b, s, N_HEADS, HEAD_DIM),
                                   v.reshape(b, s, N_HEADS, HEAD_DIM),
                                   g_q[i], g_k[i], rel_bias[i])
        yc = short_conv(u, bg, cg, conv_w[i], conv_b[i])
        m = jax.nn.sigmoid(ga) * (ya @ w_pa[i]) + jax.nn.sigmoid(gc) * (yc @ w_pc[i])
        h = h + m @ w_o[i]
        n2 = rms_norm(h, g_ffn[i]).reshape(b * s, d)
        h = h + hier_moe(n2, w_group[i], b_group[i], w_router[i], b_router[i],
                         w1[i], w3[i], w2[i]).reshape(b, s, d)
        gate = jax.nn.sigmoid(rms_norm(h, g_ple[i]) @ w_ple_gate[i] + b_ple_gate[i])
        h = h + gate * (p[i] @ w_ple_proj[i])
    return h
```

```python
import functools

import jax
import jax.numpy as jnp
from jax import lax
from jax.experimental import pallas as pl
from jax.experimental.pallas import tpu as pltpu

D_MODEL = 1024
CHUNK = 64
LEFT_CHUNKS = 8
N_HEADS = 8
HEAD_DIM = 64
ATTN_W = N_HEADS * HEAD_DIM
CONV_W = D_MODEL // 2
CONV_K = 3
MAX_REL_PAST = 256
PLE_DIM = 256
N_GROUPS = 4
EXPERTS_PER_GROUP = 8
N_EXPERTS = N_GROUPS * EXPERTS_PER_GROUP
TOP_K = 2
D_EXPERT = 512
EPS = 1e-6
NEG = -1e30

LANES = 128
SUBLANES = 8
TM = 256
TQ = 256
KV_SLABS = 1 + (LEFT_CHUNKS * CHUNK) // TQ
ROW_BLK = 512
ROUTE_OFF = N_GROUPS
VMEM_LIMIT = 56 * 1024 * 1024

F32 = jnp.float32
BF16 = jnp.bfloat16


def _dot(a, b):
    return jnp.dot(a, b, preferred_element_type=F32)


def _rms(x, g):
    ms = jnp.mean(x * x, axis=-1, keepdims=True)
    return (x * lax.rsqrt(ms + EPS)) * g


def _sigmoid(x):
    return 1.0 / (1.0 + jnp.exp(-x))


def _inproj_kernel(x_ref, g_ref, wqkv_ref, wconv_ref, b_ref, gq_ref, gk_ref, hm_ref,
                   cw_ref, cb_ref, q_ref, k_ref, v_ref, yc_ref, carry_ref, *, tiles_per_seq):
    i = pl.program_id(0)
    nb = _rms(x_ref[...], g_ref[...]).astype(BF16)

    zq = _dot(nb, wqkv_ref[...]) + b_ref[:, 0:3 * ATTN_W]
    hm = hm_ref[...]

    def head_rms(t, g):
        sq = t * t
        hi = sq.astype(BF16)
        lo = (sq - hi.astype(F32)).astype(BF16)
        ms = _dot(hi, hm) + _dot(lo, hm)
        return (t * lax.rsqrt(ms + EPS)) * g

    q_ref[...] = head_rms(zq[:, 0:ATTN_W], gq_ref[...]).astype(BF16)
    k_ref[...] = head_rms(zq[:, ATTN_W:2 * ATTN_W], gk_ref[...]).astype(BF16)
    v_ref[...] = zq[:, 2 * ATTN_W:3 * ATTN_W].astype(BF16)

    zc = _dot(nb, wconv_ref[...]) + b_ref[:, 3 * ATTN_W:3 * ATTN_W + 3 * CONV_W]
    u = zc[:, 0:CONV_W]
    bg = zc[:, CONV_W:2 * CONV_W]
    cg = zc[:, 2 * CONV_W:3 * CONV_W]
    cu = cg * u

    @pl.when((i % tiles_per_seq) == 0)
    def _():
        carry_ref[...] = jnp.zeros_like(carry_ref)

    prev = carry_ref[...]
    carry_ref[...] = cu[TM - SUBLANES:TM, :]
    row = lax.broadcasted_iota(jnp.int32, (SUBLANES, CONV_W), 0)

    def shifted(s):
        r = pltpu.roll(cu, s, 0)
        p = pltpu.roll(prev, s, 0)
        top = jnp.where(row < s, p, r[0:SUBLANES, :])
        return jnp.concatenate([top, r[SUBLANES:, :]], axis=0)

    y = cb_ref[...] + cw_ref[0:1, :] * shifted(2)
    y = y + cw_ref[1:2, :] * shifted(1)
    y = y + cw_ref[2:3, :] * cu
    yc_ref[...] = (bg * y).astype(BF16)


def _inproj(x2, g_mix, wqkv, wconv, b_in, gq, gk, hmat, cw, cb, seq):
    t = x2.shape[0]
    const = lambda i: (0, 0)
    row = lambda i: (i, 0)
    out = jax.ShapeDtypeStruct((t, ATTN_W), BF16)
    return pl.pallas_call(
        functools.partial(_inproj_kernel, tiles_per_seq=seq // TM),
        grid=(t // TM,),
        in_specs=[
            pl.BlockSpec((TM, D_MODEL), row),
            pl.BlockSpec((1, D_MODEL), const),
            pl.BlockSpec((D_MODEL, 3 * ATTN_W), const),
            pl.BlockSpec((D_MODEL, 3 * CONV_W), const),
            pl.BlockSpec((1, 3 * ATTN_W + 3 * CONV_W), const),
            pl.BlockSpec((1, ATTN_W), const),
            pl.BlockSpec((1, ATTN_W), const),
            pl.BlockSpec((ATTN_W, ATTN_W), const),
            pl.BlockSpec((SUBLANES, CONV_W), const),
            pl.BlockSpec((1, CONV_W), const),
        ],
        out_specs=[pl.BlockSpec((TM, ATTN_W), row)] * 4,
        out_shape=[out] * 4,
        scratch_shapes=[pltpu.VMEM((SUBLANES, CONV_W), F32)],
        compiler_params=pltpu.CompilerParams(
            dimension_semantics=("arbitrary",), vmem_limit_bytes=VMEM_LIMIT),
        name="inproj",
    )(x2, g_mix, wqkv, wconv, b_in, gq, gk, hmat, cw, cb)


def _attn_kernel(q_ref, k0_ref, k1_ref, k2_ref, v0_ref, v1_ref, v2_ref, bias_ref, o_ref):
    i = pl.program_id(1)
    k_refs = (k0_ref, k1_ref, k2_ref)
    v_refs = (v0_ref, v1_ref, v2_ref)
    pens = [jnp.where(i >= KV_SLABS - 1 - j, 0.0, NEG).astype(F32) for j in range(KV_SLABS - 1)]
    for h in range(N_HEADS):
        hs = slice(h * HEAD_DIM, (h + 1) * HEAD_DIM)
        qh = q_ref[0, :, hs]
        s = []
        for j in range(KV_SLABS):
            sj = lax.dot_general(qh, k_refs[j][0, :, hs], (((1,), (1,)), ((), ())),
                                 preferred_element_type=F32)
            sj = sj + bias_ref[h, :, j * TQ:(j + 1) * TQ]
            if j < KV_SLABS - 1:
                sj = sj + pens[j]
            s.append(sj)
        m = s[0].max(axis=-1, keepdims=True)
        for j in range(1, KV_SLABS):
            m = jnp.maximum(m, s[j].max(axis=-1, keepdims=True))
        e = [jnp.exp(sj - m) for sj in s]
        l = e[0].sum(axis=-1, keepdims=True)
        for j in range(1, KV_SLABS):
            l = l + e[j].sum(axis=-1, keepdims=True)
        inv = 1.0 / l
        acc = None
        for j in range(KV_SLABS):
            pj = (e[j] * inv).astype(BF16)
            oj = _dot(pj, v_refs[j][0, :, hs])
            acc = oj if acc is None else acc + oj
        o_ref[0, :, hs] = acc.astype(BF16)


def _attention(q, k, v, bias):
    b, s, _ = q.shape
    blk = (1, TQ, ATTN_W)

    def kv_map(j):
        back = KV_SLABS - 1 - j
        return lambda bi, i: (bi, jnp.maximum(i - back, 0), 0)

    kv_specs = [pl.BlockSpec(blk, kv_map(j)) for j in range(KV_SLABS)]
    return pl.pallas_call(
        _attn_kernel,
        grid=(b, s // TQ),
        in_specs=[pl.BlockSpec(blk, lambda bi, i: (bi, i, 0))] + kv_specs + kv_specs + [
            pl.BlockSpec((N_HEADS, TQ, KV_SLABS * TQ), lambda bi, i: (0, 0, 0))],
        out_specs=pl.BlockSpec(blk, lambda bi, i: (bi, i, 0)),
        out_shape=jax.ShapeDtypeStruct((b, s, ATTN_W), BF16),
        compiler_params=pltpu.CompilerParams(
            dimension_semantics=("arbitrary", "arbitrary"), vmem_limit_bytes=VMEM_LIMIT),
        name="attn",
    )(q, k, k, k, v, v, v, bias)


def _attn_bias(rel_bias):
    nk = KV_SLABS * TQ
    r = jnp.arange(TQ)[:, None]
    c = jnp.arange(nk)[None, :]
    dist = r + (nk - TQ) - c
    idx = jnp.clip(dist, -(CHUNK - 1), MAX_REL_PAST) + (CHUNK - 1)
    qc = r // CHUNK
    kc = c // CHUNK
    lead = (nk - TQ) // CHUNK - LEFT_CHUNKS
    band = (kc >= qc + lead) & (kc <= qc + lead + LEFT_CHUNKS)
    return jnp.where(band[None], rel_bias[:, idx].astype(F32), NEG)


def _merge_kernel(x_ref, ya_ref, yc_ref, g_ref, wg_ref, bgate_ref, wpa_ref, wpc_ref, wo_ref,
                  gffn_ref, wrt_ref, brt_ref, h_ref, n2p_ref, route_ref, cnt_ref, carry_ref):
    i = pl.program_id(0)

    @pl.when(i == 0)
    def _():
        carry_ref[...] = jnp.zeros_like(carry_ref)

    x = x_ref[...]
    nb = _rms(x, g_ref[...]).astype(BF16)
    gates = _dot(nb, wg_ref[...]) + bgate_ref[...]
    sga = _sigmoid(gates[:, 0:D_MODEL])
    sgc = _sigmoid(gates[:, D_MODEL:2 * D_MODEL])
    m = sga * _dot(ya_ref[...], wpa_ref[...]) + sgc * _dot(yc_ref[...], wpc_ref[...])
    h = x + _dot(m.astype(BF16), wo_ref[...])
    h_ref[...] = h

    n2 = _rms(h, gffn_ref[...]).astype(BF16)
    half = D_MODEL // 2
    lo = lax.bitcast_convert_type(n2[:, 0:half].astype(F32), jnp.uint32) >> 16
    hi = lax.bitcast_convert_type(n2[:, half:].astype(F32), jnp.uint32) & jnp.uint32(0xFFFF0000)
    n2p_ref[...] = hi | lo

    logits = _dot(n2, wrt_ref[...]) + brt_ref[...]
    lane = lax.broadcasted_iota(jnp.int32, (TM, LANES), 1).astype(F32)
    ninf = -jnp.inf

    def argmax_first(vals):
        mx = vals.max(axis=-1, keepdims=True)
        idx = jnp.where(vals == mx, lane, float(LANES)).min(axis=-1, keepdims=True)
        return mx, idx

    gmask = lane < N_GROUPS
    gmax, grp = argmax_first(jnp.where(gmask, logits, ninf))
    gsum = jnp.where(gmask, jnp.exp(logits - gmax), 0.0).sum(axis=-1, keepdims=True)
    p_grp = 1.0 / gsum
    first = ROUTE_OFF + EXPERTS_PER_GROUP * grp
    el = jnp.where((lane >= first) & (lane < first + EXPERTS_PER_GROUP), logits, ninf)
    l1, i1 = argmax_first(el)
    l2, i2 = argmax_first(jnp.where(lane == i1, ninf, el))
    e2 = jnp.exp(l2 - l1)
    den = 1.0 + e2
    w1 = p_grp * (1.0 / den)
    w2 = p_grp * (e2 / den)

    oh1 = (lane == i1).astype(F32)
    oh2 = (lane == i2).astype(F32)
    oh = oh1 + oh2
    r = lax.broadcasted_iota(jnp.int32, (TM, TM), 0)
    c = lax.broadcasted_iota(jnp.int32, (TM, TM), 1)
    below = (c < r).astype(BF16)
    before = _dot(below, oh.astype(BF16)) + carry_ref[0:1, :]
    rank1 = (oh1 * before).sum(axis=-1, keepdims=True)
    rank2 = (oh2 * before).sum(axis=-1, keepdims=True)
    total = carry_ref[0:1, :] + oh.sum(axis=0, keepdims=True)
    carry_ref[...] = jnp.broadcast_to(total, carry_ref.shape)
    cnt_ref[...] = jnp.broadcast_to(total, cnt_ref.shape)

    cols = (i1 - ROUTE_OFF, i2 - ROUTE_OFF, w1, w2, rank1, rank2)
    route = jnp.zeros((TM, LANES), F32)
    for j, col in enumerate(cols):
        route = jnp.where(lane == j, col, route)
    route_ref[...] = route


def _merge(x2, ya, yc, g_mix, wgate, bgate, wpa, wpc, wo, gffn, wrt, brt):
    t = x2.shape[0]
    const = lambda i: (0, 0)
    row = lambda i: (i, 0)
    return pl.pallas_call(
        _merge_kernel,
        grid=(t // TM,),
        in_specs=[
            pl.BlockSpec((TM, D_MODEL), row),
            pl.BlockSpec((TM, ATTN_W), row),
            pl.BlockSpec((TM, CONV_W), row),
            pl.BlockSpec((1, D_MODEL), const),
            pl.BlockSpec((D_MODEL, 2 * D_MODEL), const),
            pl.BlockSpec((1, 2 * D_MODEL), const),
            pl.BlockSpec((ATTN_W, D_MODEL), const),
            pl.BlockSpec((CONV_W, D_MODEL), const),
            pl.BlockSpec((D_MODEL, D_MODEL), const),
            pl.BlockSpec((1, D_MODEL), const),
            pl.BlockSpec((D_MODEL, LANES), const),
            pl.BlockSpec((1, LANES), const),
        ],
        out_specs=[
            pl.BlockSpec((TM, D_MODEL), row),
            pl.BlockSpec((TM, D_MODEL // 2), row),
            pl.BlockSpec((TM, LANES), row),
            pl.BlockSpec((SUBLANES, LANES), const),
        ],
        out_shape=[
            jax.ShapeDtypeStruct((t, D_MODEL), F32),
            jax.ShapeDtypeStruct((t, D_MODEL // 2), jnp.uint32),
            jax.ShapeDtypeStruct((t, LANES), F32),
            jax.ShapeDtypeStruct((SUBLANES, LANES), F32),
        ],
        scratch_shapes=[pltpu.VMEM((SUBLANES, LANES), F32)],
        compiler_params=pltpu.CompilerParams(
            dimension_semantics=("arbitrary",), vmem_limit_bytes=VMEM_LIMIT),
        name="merge",
    )(x2, ya, yc, g_mix, wgate, bgate, wpa, wpc, wo, gffn, wrt, brt)


def _stage_indices(dest_hbm, idx_smem, idx_sem, i, n_steps):
    n = TOP_K * TM

    def fetch(step, slot):
        return pltpu.make_async_copy(dest_hbm.at[pl.ds(step * n, n)], idx_smem.at[slot], idx_sem.at[slot])

    @pl.when(i == 0)
    def _():
        fetch(0, 0).start()

    slot = i % 2
    fetch(i, slot).wait()

    @pl.when(i + 1 < n_steps)
    def _():
        fetch(i + 1, 1 - slot).start()

    return slot


def _dispatch_kernel(n2p_ref, dest_hbm, xs_hbm, idx_smem, idx_sem, row_sem, *, n_steps):
    i = pl.program_id(0)
    slot = _stage_indices(dest_hbm, idx_smem, idx_sem, i, n_steps)

    def row_copy(j, kk):
        d = idx_smem[slot, TOP_K * j + kk]
        return pltpu.make_async_copy(n2p_ref.at[pl.ds(j, 1), :], xs_hbm.at[pl.ds(d, 1), :], row_sem)

    def issue(j, carry):
        for kk in range(TOP_K):
            row_copy(j, kk).start()
        return carry

    lax.fori_loop(0, TM, issue, 0, unroll=8)

    def drain(j, carry):
        for kk in range(TOP_K):
            row_copy(j, kk).wait()
        return carry

    lax.fori_loop(0, TM, drain, 0, unroll=8)


def _dispatch(n2p, dest, n_rows):
    t = n2p.shape[0]
    n_steps = t // TM
    return pl.pallas_call(
        functools.partial(_dispatch_kernel, n_steps=n_steps),
        grid=(n_steps,),
        in_specs=[
            pl.BlockSpec((TM, D_MODEL // 2), lambda i: (i, 0)),
            pl.BlockSpec(memory_space=pl.ANY),
        ],
        out_specs=pl.BlockSpec(memory_space=pl.ANY),
        out_shape=jax.ShapeDtypeStruct((n_rows, D_MODEL // 2), jnp.uint32),
        scratch_shapes=[
            pltpu.SMEM((2, TOP_K * TM), jnp.int32),
            pltpu.SemaphoreType.DMA((2,)),
            pltpu.SemaphoreType.DMA,
        ],
        compiler_params=pltpu.CompilerParams(
            dimension_semantics=("arbitrary",)),
        name="dispatch",
    )(n2p, dest)


def _experts_kernel(be_ref, nv_ref, nb_ref, xs_ref, w1_ref, w3_ref, w2_ref, ys_ref,
                    w1b_ref, w3b_ref, w2b_ref):
    s = pl.program_id(0)

    @pl.when(s < nb_ref[0])
    def _():
        prev = be_ref[jnp.maximum(s - 1, 0)]

        @pl.when((s == 0) | (be_ref[s] != prev))
        def _():
            w1b_ref[...] = w1_ref[0].astype(BF16)
            w3b_ref[...] = w3_ref[0].astype(BF16)
            w2b_ref[...] = w2_ref[0].astype(BF16)

        half = D_MODEL // 2
        row = lax.broadcasted_iota(jnp.int32, (ROW_BLK, half), 0)
        xw = jnp.where(row < nv_ref[s], xs_ref[...], jnp.uint32(0))
        lo = lax.bitcast_convert_type(xw << 16, F32).astype(BF16)
        hi = lax.bitcast_convert_type(xw & jnp.uint32(0xFFFF0000), F32).astype(BF16)
        a = _dot(lo, w1b_ref[0:half, :]) + _dot(hi, w1b_ref[half:, :])
        g = _dot(lo, w3b_ref[0:half, :]) + _dot(hi, w3b_ref[half:, :])
        hdn = (a * _sigmoid(a)) * g
        ys_ref[...] = _dot(hdn.astype(BF16), w2b_ref[...])


def _experts(blk_e, nvalid, nblk, xs, w1, w3, w2):
    n_rows = xs.shape[0]
    n_blocks = n_rows // ROW_BLK

    def rows(s, be, nv, nb):
        return (jnp.minimum(s, nb[0] - 1), 0)

    def wsel(s, be, nv, nb):
        return (be[jnp.minimum(s, nb[0] - 1)], 0, 0)

    grid_spec = pltpu.PrefetchScalarGridSpec(
        num_scalar_prefetch=3,
        grid=(n_blocks,),
        in_specs=[
            pl.BlockSpec((ROW_BLK, D_MODEL // 2), rows),
            pl.BlockSpec((1, D_MODEL, D_EXPERT), wsel),
            pl.BlockSpec((1, D_MODEL, D_EXPERT), wsel),
            pl.BlockSpec((1, D_EXPERT, D_MODEL), wsel),
        ],
        out_specs=pl.BlockSpec((ROW_BLK, D_MODEL), rows),
        scratch_shapes=[
            pltpu.VMEM((D_MODEL, D_EXPERT), BF16),
            pltpu.VMEM((D_MODEL, D_EXPERT), BF16),
            pltpu.VMEM((D_EXPERT, D_MODEL), BF16),
        ],
    )
    return pl.pallas_call(
        _experts_kernel,
        grid_spec=grid_spec,
        out_shape=jax.ShapeDtypeStruct((n_rows, D_MODEL), F32),
        compiler_params=pltpu.CompilerParams(
            dimension_semantics=("arbitrary",), vmem_limit_bytes=VMEM_LIMIT),
        name="experts",
    )(blk_e, nvalid, nblk, xs, w1, w3, w2)


def _combine_kernel(h_ref, route_ref, p_ref, gple_ref, wpg_ref, bpg_ref, wpp_ref, dest_hbm, ys_hbm,
                    o_ref, ybuf_ref, idx_smem, idx_sem, row_sem, *, n_steps):
    i = pl.program_id(0)
    slot = _stage_indices(dest_hbm, idx_smem, idx_sem, i, n_steps)

    def row_copy(j, kk):
        d = idx_smem[slot, TOP_K * j + kk]
        return pltpu.make_async_copy(ys_hbm.at[pl.ds(d, 1), :], ybuf_ref.at[kk, pl.ds(j, 1), :], row_sem)

    def issue(j, carry):
        for kk in range(TOP_K):
            row_copy(j, kk).start()
        return carry

    lax.fori_loop(0, TM, issue, 0, unroll=8)

    pp = _dot(p_ref[...].astype(BF16), wpp_ref[...])

    def drain(j, carry):
        for kk in range(TOP_K):
            row_copy(j, kk).wait()
        return carry

    lax.fori_loop(0, TM, drain, 0, unroll=8)

    route = route_ref[...]
    moe = ybuf_ref[0] * route[:, 2:3] + ybuf_ref[1] * route[:, 3:4]
    h = h_ref[...] + moe
    gate = _sigmoid(_dot(_rms(h, gple_ref[...]).astype(BF16), wpg_ref[...]) + bpg_ref[...])
    o_ref[...] = h + gate * pp


def _combine(h1, route, p2, gple, wpg, bpg, wpp, dest, ys):
    t = h1.shape[0]
    n_steps = t // TM
    const = lambda i: (0, 0)
    row = lambda i: (i, 0)
    return pl.pallas_call(
        functools.partial(_combine_kernel, n_steps=n_steps),
        grid=(n_steps,),
        in_specs=[
            pl.BlockSpec((TM, D_MODEL), row),
            pl.BlockSpec((TM, LANES), row),
            pl.BlockSpec((TM, PLE_DIM), row),
            pl.BlockSpec((1, D_MODEL), const),
            pl.BlockSpec((D_MODEL, D_MODEL), const),
            pl.BlockSpec((1, D_MODEL), const),
            pl.BlockSpec((PLE_DIM, D_MODEL), const),
            pl.BlockSpec(memory_space=pl.ANY),
            pl.BlockSpec(memory_space=pl.ANY),
        ],
        out_specs=pl.BlockSpec((TM, D_MODEL), row),
        out_shape=jax.ShapeDtypeStruct((t, D_MODEL), F32),
        scratch_shapes=[
            pltpu.VMEM((TOP_K, TM, D_MODEL), F32),
            pltpu.SMEM((2, TOP_K * TM), jnp.int32),
            pltpu.SemaphoreType.DMA((2,)),
            pltpu.SemaphoreType.DMA,
        ],
        compiler_params=pltpu.CompilerParams(
            dimension_semantics=("arbitrary",), vmem_limit_bytes=VMEM_LIMIT),
        name="combine",
    )(h1, route, p2, gple, wpg, bpg, wpp, dest, ys)


def _layer(h, p_i, g_mix, w_in, b_in, g_q, g_k, rel_bias, conv_w, conv_b, w_pa, w_pc, w_o,
           g_ffn, w_group, b_group, w_router, b_router, w1, w3, w2,
           g_ple, w_ple_gate, b_ple_gate, w_ple_proj):
    b, s, d = h.shape
    t = b * s
    x2 = h.reshape(t, d)
    row2 = lambda a: a.reshape(1, -1).astype(F32)

    qkv_w = 3 * ATTN_W
    conv_end = qkv_w + 3 * CONV_W
    w_in_b = w_in.astype(BF16)
    scale = HEAD_DIM ** -0.5
    gq = row2(jnp.tile(g_q.astype(F32) * scale, N_HEADS))
    gk = row2(jnp.tile(g_k.astype(F32), N_HEADS))
    head = jnp.arange(ATTN_W) // HEAD_DIM
    hmat = jnp.where(head[:, None] == head[None, :], 1.0 / HEAD_DIM, 0.0).astype(BF16)
    cw = jnp.zeros((SUBLANES, CONV_W), F32).at[:CONV_K].set(conv_w.astype(F32))

    q, k, v, yc = _inproj(x2, row2(g_mix), w_in_b[:, :qkv_w], w_in_b[:, qkv_w:conv_end],
                          row2(b_in[:conv_end]), gq, gk, hmat, cw, row2(conv_b), s)

    ya = _attention(q.reshape(b, s, ATTN_W), k.reshape(b, s, ATTN_W), v.reshape(b, s, ATTN_W),
                    _attn_bias(rel_bias)).reshape(t, ATTN_W)

    wrt = jnp.zeros((d, LANES), F32).at[:, :N_GROUPS].set(w_group)
    wrt = wrt.at[:, ROUTE_OFF:ROUTE_OFF + N_EXPERTS].set(w_router).astype(BF16)
    brt = jnp.zeros((1, LANES), F32).at[0, :N_GROUPS].set(b_group)
    brt = brt.at[0, ROUTE_OFF:ROUTE_OFF + N_EXPERTS].set(b_router)
    h1, n2p, route, cnt = _merge(x2, ya, yc, row2(g_mix), w_in_b[:, conv_end:], row2(b_in[conv_end:]),
                                 w_pa.astype(BF16), w_pc.astype(BF16), w_o.astype(BF16),
                                 row2(g_ffn), wrt, brt)

    counts = cnt[0, ROUTE_OFF:ROUTE_OFF + N_EXPERTS].astype(jnp.int32)
    pcounts = (counts + ROW_BLK - 1) // ROW_BLK * ROW_BLK
    pends = jnp.cumsum(pcounts)
    pstarts = pends - pcounts
    n_rows = t * TOP_K + N_EXPERTS * ROW_BLK
    n_blocks = n_rows // ROW_BLK
    eid = route[:, 0:TOP_K].astype(jnp.int32)
    rank = route[:, 4:4 + TOP_K].astype(jnp.int32)
    dest = (pstarts[eid] + rank).reshape(-1)
    blk_start = jnp.arange(n_blocks, dtype=jnp.int32) * ROW_BLK
    blk_e = jnp.minimum(jnp.searchsorted(pends, blk_start, side="right"), N_EXPERTS - 1).astype(jnp.int32)
    nvalid = jnp.clip(counts[blk_e] - (blk_start - pstarts[blk_e]), 0, ROW_BLK).astype(jnp.int32)
    nblk = (pends[-1:] // ROW_BLK).astype(jnp.int32)

    xs = _dispatch(n2p, dest, n_rows)
    ys = _experts(blk_e, nvalid, nblk, xs, w1, w3, w2)
    out = _combine(h1, route, p_i.reshape(t, PLE_DIM), row2(g_ple), w_ple_gate.astype(BF16),
                   row2(b_ple_gate), w_ple_proj.astype(BF16), dest, ys)
    return out.reshape(b, s, d)


def kernel(x, p, g_mix, w_in, b_in, g_q, g_k, rel_bias, conv_w, conv_b, w_pa, w_pc, w_o, g_ffn, w_group, b_group, w_router, b_router, w1, w3, w2, g_ple, w_ple_gate, b_ple_gate, w_ple_proj):
    h = x
    for i in range(p.shape[0]):
        h = _layer(h, p[i], g_mix[i], w_in[i], b_in[i], g_q[i], g_k[i], rel_bias[i], conv_w[i], conv_b[i],
                   w_pa[i], w_pc[i], w_o[i], g_ffn[i], w_group[i], b_group[i], w_router[i], b_router[i],
                   w1[i], w3[i], w2[i], g_ple[i], w_ple_gate[i], b_ple_gate[i], w_ple_proj[i])
    return h
```

```python
import functools

import jax
import jax.numpy as jnp
from jax import lax
from jax.experimental import pallas as pl
from jax.experimental.pallas import tpu as pltpu

D_MODEL = 1024
CHUNK = 64
LEFT_CHUNKS = 8
N_HEADS = 8
HEAD_DIM = 64
ATTN_W = N_HEADS * HEAD_DIM
CONV_W = D_MODEL // 2
CONV_K = 3
MAX_REL_PAST = 256
PLE_DIM = 256
N_GROUPS = 4
EXPERTS_PER_GROUP = 8
N_EXPERTS = N_GROUPS * EXPERTS_PER_GROUP
TOP_K = 2
D_EXPERT = 512
EPS = 1e-6
NEG = -1e30
LOG2E = 1.4426950408889634

LANES = 128
SUBLANES = 8
TM = 256
TQ = 256
KV_SLABS = 1 + (LEFT_CHUNKS * CHUNK) // TQ
ROW_BLK = 512
TILE_ROWS = TOP_K * TM
ROUTE_OFF = N_GROUPS
PACK_ROWS = D_MODEL // (2 * LANES)
PACK_W = 2 * LANES
SEG_SIZES = (8, 4, 2, 1)
VMEM_LIMIT = 56 * 1024 * 1024

F32 = jnp.float32
BF16 = jnp.bfloat16
U32 = jnp.uint32


def _dot(a, b):
    return jnp.dot(a, b, preferred_element_type=F32)


def _rms(x, g):
    ms = jnp.mean(x * x, axis=-1, keepdims=True)
    return (x * lax.rsqrt(ms + EPS)) * g


def _sigmoid(x):
    return 1.0 / (1.0 + jnp.exp(-x))


def _pack_rows(ref, vals, n_rows, base=0):
    for a in range(PACK_ROWS):
        lo = vals[:, a * PACK_W:a * PACK_W + LANES].astype(BF16).astype(F32)
        hi = vals[:, a * PACK_W + LANES:(a + 1) * PACK_W].astype(BF16).astype(F32)
        word = (lax.bitcast_convert_type(hi, U32) & U32(0xFFFF0000)) | (lax.bitcast_convert_type(lo, U32) >> 16)
        ref[pl.ds(base + a, n_rows, stride=PACK_ROWS), :] = word


def _unpack_rows(ref, n_rows, base=0, n_valid=None):
    out = []
    if n_valid is not None:
        keep = lax.broadcasted_iota(jnp.int32, (n_rows, LANES), 0) < n_valid
    for a in range(PACK_ROWS):
        word = ref[pl.ds(base + a, n_rows, stride=PACK_ROWS), :]
        if n_valid is not None:
            word = jnp.where(keep, word, U32(0))
        lo = lax.bitcast_convert_type(word << 16, F32).astype(BF16)
        hi = lax.bitcast_convert_type(word & U32(0xFFFF0000), F32).astype(BF16)
        out.append(jnp.concatenate([lo, hi], axis=1))
    return out


def _inproj_kernel(x_ref, g_ref, wqkv_ref, wconv_ref, b_ref, gq_ref, gk_ref, hm_ref,
                   cw_ref, cb_ref, q_ref, k_ref, v_ref, yc_ref, carry_ref, *, tiles_per_seq):
    i = pl.program_id(0)
    nb = _rms(x_ref[...], g_ref[...]).astype(BF16)

    zq = _dot(nb, wqkv_ref[...]) + b_ref[:, 0:3 * ATTN_W]
    hm = hm_ref[...]

    def head_rms(t, g):
        sq = t * t
        hi = sq.astype(BF16)
        lo = (sq - hi.astype(F32)).astype(BF16)
        ms = _dot(hi, hm) + _dot(lo, hm)
        return (t * lax.rsqrt(ms + EPS)) * g

    q_ref[...] = head_rms(zq[:, 0:ATTN_W], gq_ref[...]).astype(BF16)
    k_ref[...] = head_rms(zq[:, ATTN_W:2 * ATTN_W], gk_ref[...]).astype(BF16)
    v_ref[...] = zq[:, 2 * ATTN_W:3 * ATTN_W].astype(BF16)

    zc = _dot(nb, wconv_ref[...]) + b_ref[:, 3 * ATTN_W:3 * ATTN_W + 3 * CONV_W]
    u = zc[:, 0:CONV_W]
    bg = zc[:, CONV_W:2 * CONV_W]
    cg = zc[:, 2 * CONV_W:3 * CONV_W]
    cu = cg * u

    @pl.when((i % tiles_per_seq) == 0)
    def _():
        carry_ref[...] = jnp.zeros_like(carry_ref)

    prev = carry_ref[...]
    carry_ref[...] = cu[TM - SUBLANES:TM, :]
    row = lax.broadcasted_iota(jnp.int32, (SUBLANES, CONV_W), 0)

    def shifted(s):
        r = pltpu.roll(cu, s, 0)
        p = pltpu.roll(prev, s, 0)
        top = jnp.where(row < s, p, r[0:SUBLANES, :])
        return jnp.concatenate([top, r[SUBLANES:, :]], axis=0)

    y = cb_ref[...] + cw_ref[0:1, :] * shifted(2)
    y = y + cw_ref[1:2, :] * shifted(1)
    y = y + cw_ref[2:3, :] * cu
    yc_ref[...] = (bg * y).astype(BF16)


def _inproj(x2, g_mix, wqkv, wconv, b_in, gq, gk, hmat, cw, cb, seq):
    t = x2.shape[0]
    const = lambda i: (0, 0)
    row = lambda i: (i, 0)
    out = jax.ShapeDtypeStruct((t, ATTN_W), BF16)
    return pl.pallas_call(
        functools.partial(_inproj_kernel, tiles_per_seq=seq // TM),
        grid=(t // TM,),
        in_specs=[
            pl.BlockSpec((TM, D_MODEL), row),
            pl.BlockSpec((1, D_MODEL), const),
            pl.BlockSpec((D_MODEL, 3 * ATTN_W), const),
            pl.BlockSpec((D_MODEL, 3 * CONV_W), const),
            pl.BlockSpec((1, 3 * ATTN_W + 3 * CONV_W), const),
            pl.BlockSpec((1, ATTN_W), const),
            pl.BlockSpec((1, ATTN_W), const),
            pl.BlockSpec((ATTN_W, ATTN_W), const),
            pl.BlockSpec((SUBLANES, CONV_W), const),
            pl.BlockSpec((1, CONV_W), const),
        ],
        out_specs=[pl.BlockSpec((TM, ATTN_W), row)] * 4,
        out_shape=[out] * 4,
        scratch_shapes=[pltpu.VMEM((SUBLANES, CONV_W), F32)],
        compiler_params=pltpu.CompilerParams(
            dimension_semantics=("arbitrary",), vmem_limit_bytes=VMEM_LIMIT),
        name="inproj",
    )(x2, g_mix, wqkv, wconv, b_in, gq, gk, hmat, cw, cb)


def _lane_fold(parts, op):
    acc = None
    for a in parts:
        for c in range(0, a.shape[1], LANES):
            piece = a[:, c:c + LANES]
            acc = piece if acc is None else op(acc, piece)
    return acc


def _attn_kernel(q_ref, k0_ref, k1_ref, k2_ref, v0_ref, v1_ref, v2_ref, bias_ref, o_ref):
    k_refs = (k0_ref, k1_ref, k2_ref)
    v_refs = (v0_ref, v1_ref, v2_ref)
    pair_w = 2 * HEAD_DIM
    lane = lax.broadcasted_iota(jnp.int32, (TQ, pair_w), 1)
    low = lane < HEAD_DIM

    def scores(h):
        ps = slice((h // 2) * pair_w, (h // 2 + 1) * pair_w)
        q_pair = q_ref[0, :, ps]
        own = low if h % 2 == 0 else jnp.logical_not(low)
        qh = jnp.where(own, q_pair, jnp.zeros_like(q_pair))
        return [lax.dot_general(qh, k_refs[j][0, :, ps], (((1,), (1,)), ((), ())),
                                preferred_element_type=F32) + bias_ref[0, h, :, j * TQ:(j + 1) * TQ]
                for j in range(KV_SLABS)]

    def weighted(h, s):
        ps = slice((h // 2) * pair_w, (h // 2 + 1) * pair_w)
        m = _lane_fold(s, jnp.maximum).max(axis=-1, keepdims=True)
        e = [jnp.exp2(sj - m) for sj in s]
        l = _lane_fold(e, jnp.add).sum(axis=-1, keepdims=True)
        acc = None
        for j in range(KV_SLABS):
            oj = _dot(e[j].astype(BF16), v_refs[j][0, :, ps])
            acc = oj if acc is None else acc + oj
        return acc * (1.0 / l)

    s_next = scores(0)
    o_even = None
    for h in range(N_HEADS):
        s_cur = s_next
        if h + 1 < N_HEADS:
            s_next = scores(h + 1)
        o = weighted(h, s_cur)
        if h % 2 == 0:
            o_even = o
        else:
            ps = slice((h // 2) * pair_w, (h // 2 + 1) * pair_w)
            o_ref[0, :, ps] = jnp.where(low, o_even, o).astype(BF16)


def _attention(q, k, v, bias):
    b, s, _ = q.shape
    blk = (1, TQ, ATTN_W)

    def kv_map(j):
        back = KV_SLABS - 1 - j
        return lambda bi, i: (bi, jnp.maximum(i - back, 0), 0)

    kv_specs = [pl.BlockSpec(blk, kv_map(j)) for j in range(KV_SLABS)]
    n_var = bias.shape[0]
    return pl.pallas_call(
        _attn_kernel,
        grid=(b, s // TQ),
        in_specs=[pl.BlockSpec(blk, lambda bi, i: (bi, i, 0))] + kv_specs + kv_specs + [
            pl.BlockSpec((1, N_HEADS, TQ, KV_SLABS * TQ),
                         lambda bi, i: (jnp.minimum(i, n_var - 1), 0, 0, 0))],
        out_specs=pl.BlockSpec(blk, lambda bi, i: (bi, i, 0)),
        out_shape=jax.ShapeDtypeStruct((b, s, ATTN_W), BF16),
        compiler_params=pltpu.CompilerParams(
            dimension_semantics=("arbitrary", "arbitrary"), vmem_limit_bytes=VMEM_LIMIT),
        name="attn",
    )(q, k, k, k, v, v, v, bias)


def _attn_bias(rel_bias):
    nk = KV_SLABS * TQ
    past = nk - TQ
    d = jnp.arange(TQ - 1 + past, -TQ, -1)
    idx = jnp.clip(d, -(CHUNK - 1), MAX_REL_PAST) + (CHUNK - 1)
    onehot = (idx[:, None] == jnp.arange(rel_bias.shape[1])[None, :]).astype(F32)
    per_dist = jnp.einsum("dn,hn->hd", onehot, rel_bias.astype(F32) * LOG2E,
                          precision=lax.Precision.HIGHEST)
    rows = [per_dist[:, TQ - 1 - r:TQ - 1 - r + nk] for r in range(TQ)]
    table = jnp.stack(rows, axis=1)
    r = jnp.arange(TQ)[:, None]
    c = jnp.arange(nk)[None, :]
    qc = r // CHUNK
    kc = c // CHUNK
    lead = past // CHUNK - LEFT_CHUNKS
    band = (kc >= qc + lead) & (kc <= qc + lead + LEFT_CHUNKS)
    variants = []
    for var in range(KV_SLABS):
        valid = band & (c >= (KV_SLABS - 1 - var) * TQ)
        variants.append(jnp.where(valid[None], table, NEG))
    return jnp.stack(variants, axis=0)


def _merge_kernel(x_ref, ya_ref, yc_ref, g_ref, wg_ref, bgate_ref, wpa_ref, wpc_ref, wo_ref,
                  gffn_ref, wrt_ref, brt_ref, h_ref, stage_ref, route_ref, cnt_ref):
    x = x_ref[...]
    nb = _rms(x, g_ref[...]).astype(BF16)
    gates = _dot(nb, wg_ref[...]) + bgate_ref[...]
    sga = _sigmoid(gates[:, 0:D_MODEL])
    sgc = _sigmoid(gates[:, D_MODEL:2 * D_MODEL])
    m = sga * _dot(ya_ref[...], wpa_ref[...]) + sgc * _dot(yc_ref[...], wpc_ref[...])
    h = x + _dot(m.astype(BF16), wo_ref[...])
    h_ref[...] = h

    n2 = _rms(h, gffn_ref[...]).astype(BF16)
    logits = _dot(n2, wrt_ref[...]) + brt_ref[...]
    lane = lax.broadcasted_iota(jnp.int32, (TM, LANES), 1).astype(F32)
    ninf = -jnp.inf

    def argmax_first(vals):
        mx = vals.max(axis=-1, keepdims=True)
        idx = jnp.where(vals == mx, lane, float(LANES)).min(axis=-1, keepdims=True)
        return mx, idx

    gmask = lane < N_GROUPS
    gmax, grp = argmax_first(jnp.where(gmask, logits, ninf))
    gsum = jnp.where(gmask, jnp.exp(logits - gmax), 0.0).sum(axis=-1, keepdims=True)
    p_grp = 1.0 / gsum
    first = ROUTE_OFF + EXPERTS_PER_GROUP * grp
    el = jnp.where((lane >= first) & (lane < first + EXPERTS_PER_GROUP), logits, ninf)
    l1, i1 = argmax_first(el)
    l2, i2 = argmax_first(jnp.where(lane == i1, ninf, el))
    e2 = jnp.exp(l2 - l1)
    den = 1.0 + e2
    w1 = p_grp * (1.0 / den)
    w2 = p_grp * (e2 / den)

    oh1 = (lane == i1).astype(F32)
    oh2 = (lane == i2).astype(F32)
    oh = (oh1 + oh2).astype(BF16)
    r = lax.broadcasted_iota(jnp.int32, (TM, TM), 0)
    c = lax.broadcasted_iota(jnp.int32, (TM, TM), 1)
    earlier_tok = _dot((c < r).astype(BF16), oh)
    er = lax.broadcasted_iota(jnp.int32, (LANES, LANES), 0)
    ec = lax.broadcasted_iota(jnp.int32, (LANES, LANES), 1)
    lower_exp = _dot(oh, (er < ec).astype(BF16)).sum(axis=0, keepdims=True)
    where = earlier_tok + lower_exp
    pos1 = (oh1 * where).sum(axis=-1, keepdims=True)
    pos2 = (oh2 * where).sum(axis=-1, keepdims=True)
    cnt_ref[0] = jnp.broadcast_to(oh.astype(F32).sum(axis=0, keepdims=True), (SUBLANES, LANES))

    slot = lax.broadcasted_iota(jnp.int32, (TM, TILE_ROWS), 1).astype(F32)
    place = ((slot == pos1) | (slot == pos2)).astype(BF16)
    sorted_rows = lax.dot_general(place, n2, (((0,), (0,)), ((), ())), preferred_element_type=F32)
    _pack_rows(stage_ref, sorted_rows, TILE_ROWS)

    cols = (i1 - ROUTE_OFF, i2 - ROUTE_OFF, w1, w2, pos1, pos2)
    route = jnp.zeros((TM, LANES), F32)
    for j, col in enumerate(cols):
        route = jnp.where(lane == j, col, route)
    route_ref[...] = route


def _merge(x2, ya, yc, g_mix, wgate, bgate, wpa, wpc, wo, gffn, wrt, brt):
    t = x2.shape[0]
    n_tiles = t // TM
    const = lambda i: (0, 0)
    row = lambda i: (i, 0)
    return pl.pallas_call(
        _merge_kernel,
        grid=(n_tiles,),
        in_specs=[
            pl.BlockSpec((TM, D_MODEL), row),
            pl.BlockSpec((TM, ATTN_W), row),
            pl.BlockSpec((TM, CONV_W), row),
            pl.BlockSpec((1, D_MODEL), const),
            pl.BlockSpec((D_MODEL, 2 * D_MODEL), const),
            pl.BlockSpec((1, 2 * D_MODEL), const),
            pl.BlockSpec((ATTN_W, D_MODEL), const),
            pl.BlockSpec((CONV_W, D_MODEL), const),
            pl.BlockSpec((D_MODEL, D_MODEL), const),
            pl.BlockSpec((1, D_MODEL), const),
            pl.BlockSpec((D_MODEL, LANES), const),
            pl.BlockSpec((1, LANES), const),
        ],
        out_specs=[
            pl.BlockSpec((TM, D_MODEL), row),
            pl.BlockSpec((TILE_ROWS * PACK_ROWS, LANES), row),
            pl.BlockSpec((TM, LANES), row),
            pl.BlockSpec((1, SUBLANES, LANES), lambda i: (i, 0, 0)),
        ],
        out_shape=[
            jax.ShapeDtypeStruct((t, D_MODEL), F32),
            jax.ShapeDtypeStruct((n_tiles * TILE_ROWS * PACK_ROWS, LANES), U32),
            jax.ShapeDtypeStruct((t, LANES), F32),
            jax.ShapeDtypeStruct((n_tiles, SUBLANES, LANES), F32),
        ],
        compiler_params=pltpu.CompilerParams(
            dimension_semantics=("arbitrary",), vmem_limit_bytes=VMEM_LIMIT),
        name="merge",
    )(x2, ya, yc, g_mix, wgate, bgate, wpa, wpc, wo, gffn, wrt, brt)


def _piece_counts(n):
    return [n // SEG_SIZES[0]] + [(n // size) & 1 for size in SEG_SIZES[1:]]


def _segment_pieces(n, visit):
    big = SEG_SIZES[0]
    n_big = n // big

    def body(t, carry):
        visit(0, t * big)
        return carry

    lax.fori_loop(0, n_big, body, 0)
    done = n_big * big
    for cls in range(1, len(SEG_SIZES)):
        size = SEG_SIZES[cls]

        @pl.when((n & size) != 0)
        def _(cls=cls, done=done):
            visit(cls, done)

        done = done + (n & size)
    return _piece_counts(n)


def _piece_copy(src_ref, dst_ref, sems, cls, src_row, dst_row):
    n = SEG_SIZES[cls] * PACK_ROWS
    return pltpu.make_async_copy(src_ref.at[pl.ds(src_row * PACK_ROWS, n), :],
                                 dst_ref.at[pl.ds(dst_row * PACK_ROWS, n), :], sems.at[cls])


def _drain(src_ref, dst_ref, sems, counts):
    for cls in range(len(SEG_SIZES)):
        def body(t, carry, cls=cls):
            _piece_copy(src_ref, dst_ref, sems, cls, 0, 0).wait()
            return carry

        lax.fori_loop(0, counts[cls], body, 0)


def _experts_kernel(be_ref, nb_ref, base_ref, jlo_ref, jhi_ref, nv_ref, cnt_ref, off_ref, cum_ref,
                    stage_hbm, w1_ref, w3_ref, w2_ref, ys_ref,
                    xbuf_ref, w1b_ref, w3b_ref, w2b_ref, sems, pend_ref):
    s = pl.program_id(0)
    nb = nb_ref[0]

    def gather_start(step, slot):
        e = be_ref[step]
        base = base_ref[step]
        buf = xbuf_ref.at[slot]
        sem = sems.at[slot]

        def segment(j, totals):
            g = j * N_EXPERTS + e
            seg0 = cum_ref[g]
            lo = jnp.maximum(seg0, base)
            hi = jnp.minimum(seg0 + cnt_ref[g], base + ROW_BLK)
            src = j * TILE_ROWS + off_ref[g] + (lo - seg0)
            dst = lo - base
            counts = _segment_pieces(
                jnp.maximum(hi - lo, 0),
                lambda cls, o: _piece_copy(stage_hbm, buf, sem, cls, src + o, dst + o).start())
            return [x + y for x, y in zip(totals, counts)]

        totals = lax.fori_loop(jlo_ref[step], jhi_ref[step], segment, [jnp.int32(0)] * len(SEG_SIZES))
        for cls in range(len(SEG_SIZES)):
            pend_ref[slot, cls] = totals[cls]

    @pl.when(s == 0)
    def _():
        xbuf_ref[...] = jnp.zeros_like(xbuf_ref)
        gather_start(0, 0)

    slot = s % 2

    @pl.when(s + 1 < nb)
    def _():
        gather_start(s + 1, 1 - slot)

    @pl.when(s < nb)
    def _():
        prev = be_ref[jnp.maximum(s - 1, 0)]

        @pl.when((s == 0) | (be_ref[s] != prev))
        def _():
            w1b_ref[...] = w1_ref[0].astype(BF16)
            w3b_ref[...] = w3_ref[0].astype(BF16)
            w2b_ref[...] = w2_ref[0].astype(BF16)

        _drain(stage_hbm, xbuf_ref.at[slot], sems.at[slot], [pend_ref[slot, c] for c in range(len(SEG_SIZES))])
        a = None
        g = None
        for blk, xa in enumerate(_unpack_rows(xbuf_ref.at[slot], ROW_BLK, n_valid=nv_ref[s])):
            rows = slice(blk * PACK_W, (blk + 1) * PACK_W)
            da = _dot(xa, w1b_ref[rows, :])
            dg = _dot(xa, w3b_ref[rows, :])
            a = da if a is None else a + da
            g = dg if g is None else g + dg
        hdn = (a * _sigmoid(a)) * g
        _pack_rows(ys_ref, _dot(hdn.astype(BF16), w2b_ref[...]), ROW_BLK)

    @pl.when(s >= nb)
    def _():
        ys_ref[...] = jnp.zeros_like(ys_ref)


def _experts(blk_e, nblk, base, jlo, jhi, nvalid, cnt, off, cum, stage, w1, w3, w2):
    n_blocks = blk_e.shape[0]

    def wsel(s, be, nb, *_):
        return (be[jnp.minimum(s, nb[0] - 1)], 0, 0)

    grid_spec = pltpu.PrefetchScalarGridSpec(
        num_scalar_prefetch=9,
        grid=(n_blocks,),
        in_specs=[
            pl.BlockSpec(memory_space=pl.ANY),
            pl.BlockSpec((1, D_MODEL, D_EXPERT), wsel),
            pl.BlockSpec((1, D_MODEL, D_EXPERT), wsel),
            pl.BlockSpec((1, D_EXPERT, D_MODEL), wsel),
        ],
        out_specs=pl.BlockSpec((ROW_BLK * PACK_ROWS, LANES), lambda s, *_: (s, 0)),
        scratch_shapes=[
            pltpu.VMEM((2, ROW_BLK * PACK_ROWS, LANES), U32),
            pltpu.VMEM((D_MODEL, D_EXPERT), BF16),
            pltpu.VMEM((D_MODEL, D_EXPERT), BF16),
            pltpu.VMEM((D_EXPERT, D_MODEL), BF16),
            pltpu.SemaphoreType.DMA((2, len(SEG_SIZES))),
            pltpu.SMEM((2, len(SEG_SIZES)), jnp.int32),
        ],
    )
    return pl.pallas_call(
        _experts_kernel,
        grid_spec=grid_spec,
        out_shape=jax.ShapeDtypeStruct((n_blocks * ROW_BLK * PACK_ROWS, LANES), U32),
        compiler_params=pltpu.CompilerParams(
            dimension_semantics=("arbitrary",), vmem_limit_bytes=VMEM_LIMIT),
        name="experts",
    )(blk_e, nblk, base, jlo, jhi, nvalid, cnt, off, cum, stage, w1, w3, w2)


def _combine_kernel(cnt_ref, off_ref, dst_ref, h_ref, route_ref, p_ref, gple_ref, wpg_ref, bpg_ref, wpp_ref,
                    ys_hbm, o_ref, ybuf_ref, sems, *, n_steps):
    i = pl.program_id(0)

    def gather_start(step, slot):
        buf = ybuf_ref.at[slot]
        sem = sems.at[slot]

        def segment(e, carry):
            g = step * N_EXPERTS + e
            off, dst = off_ref[g], dst_ref[g]
            _segment_pieces(cnt_ref[g],
                            lambda cls, o: _piece_copy(ys_hbm, buf, sem, cls, dst + o, off + o).start())
            return carry

        lax.fori_loop(0, N_EXPERTS, segment, 0)

    def gather_wait(step, slot):
        def segment(e, totals):
            return [x + y for x, y in zip(totals, _piece_counts(cnt_ref[step * N_EXPERTS + e]))]

        totals = lax.fori_loop(0, N_EXPERTS, segment, [jnp.int32(0)] * len(SEG_SIZES))
        _drain(ys_hbm, ybuf_ref.at[slot], sems.at[slot], totals)

    @pl.when(i == 0)
    def _():
        gather_start(0, 0)

    slot = i % 2

    @pl.when(i + 1 < n_steps)
    def _():
        gather_start(i + 1, 1 - slot)

    pp = _dot(p_ref[...].astype(BF16), wpp_ref[...])
    gather_wait(i, slot)

    route = route_ref[...]
    place = lax.broadcasted_iota(jnp.int32, (TM, TILE_ROWS), 1).astype(F32)
    sel = [(place == route[:, 4 + kk:5 + kk]).astype(BF16) for kk in range(TOP_K)]
    cols = _unpack_rows(ybuf_ref.at[slot], TILE_ROWS)
    picked = [jnp.concatenate([_dot(sel[kk], blk) for blk in cols], axis=1) for kk in range(TOP_K)]
    moe = picked[0] * route[:, 2:3] + picked[1] * route[:, 3:4]
    h = h_ref[...] + moe
    gate = _sigmoid(_dot(_rms(h, gple_ref[...]).astype(BF16), wpg_ref[...]) + bpg_ref[...])
    o_ref[...] = h + gate * pp


def _combine(cnt, off, dst, h1, route, p2, gple, wpg, bpg, wpp, ys):
    t = h1.shape[0]
    n_steps = t // TM
    const = lambda i, *_: (0, 0)
    row = lambda i, *_: (i, 0)
    grid_spec = pltpu.PrefetchScalarGridSpec(
        num_scalar_prefetch=3,
        grid=(n_steps,),
        in_specs=[
            pl.BlockSpec((TM, D_MODEL), row),
            pl.BlockSpec((TM, LANES), row),
            pl.BlockSpec((TM, PLE_DIM), row),
            pl.BlockSpec((1, D_MODEL), const),
            pl.BlockSpec((D_MODEL, D_MODEL), const),
            pl.BlockSpec((1, D_MODEL), const),
            pl.BlockSpec((PLE_DIM, D_MODEL), const),
            pl.BlockSpec(memory_space=pl.ANY),
        ],
        out_specs=pl.BlockSpec((TM, D_MODEL), row),
        scratch_shapes=[
            pltpu.VMEM((2, TILE_ROWS * PACK_ROWS, LANES), U32),
            pltpu.SemaphoreType.DMA((2, len(SEG_SIZES))),
        ],
    )
    return pl.pallas_call(
        functools.partial(_combine_kernel, n_steps=n_steps),
        grid_spec=grid_spec,
        out_shape=jax.ShapeDtypeStruct((t, D_MODEL), F32),
        compiler_params=pltpu.CompilerParams(
            dimension_semantics=("arbitrary",), vmem_limit_bytes=VMEM_LIMIT),
        name="combine",
    )(cnt, off, dst, h1, route, p2, gple, wpg, bpg, wpp, ys)


def _layer(h, p_i, g_mix, w_in, b_in, g_q, g_k, rel_bias, conv_w, conv_b, w_pa, w_pc, w_o,
           g_ffn, w_group, b_group, w_router, b_router, w1, w3, w2,
           g_ple, w_ple_gate, b_ple_gate, w_ple_proj):
    b, s, d = h.shape
    t = b * s
    x2 = h.reshape(t, d)
    row2 = lambda a: a.reshape(1, -1).astype(F32)

    qkv_w = 3 * ATTN_W
    conv_end = qkv_w + 3 * CONV_W
    w_in_b = w_in.astype(BF16)
    gq = row2(jnp.tile(g_q.astype(F32) * (HEAD_DIM ** -0.5 * LOG2E), N_HEADS))
    gk = row2(jnp.tile(g_k.astype(F32), N_HEADS))
    head = jnp.arange(ATTN_W) // HEAD_DIM
    hmat = jnp.where(head[:, None] == head[None, :], 1.0 / HEAD_DIM, 0.0).astype(BF16)
    cw = jnp.concatenate([conv_w.astype(F32), jnp.zeros((SUBLANES - CONV_K, CONV_W), F32)], axis=0)

    q, k, v, yc = _inproj(x2, row2(g_mix), w_in_b[:, :qkv_w], w_in_b[:, qkv_w:conv_end],
                          row2(b_in[:conv_end]), gq, gk, hmat, cw, row2(conv_b), s)

    ya = _attention(q.reshape(b, s, ATTN_W), k.reshape(b, s, ATTN_W), v.reshape(b, s, ATTN_W),
                    _attn_bias(rel_bias)).reshape(t, ATTN_W)

    n_pad = LANES - N_GROUPS - N_EXPERTS
    wrt = jnp.concatenate([w_group, w_router, jnp.zeros((d, n_pad), w_group.dtype)], axis=1).astype(BF16)
    brt = row2(jnp.concatenate([b_group, b_router, jnp.zeros((n_pad,), b_group.dtype)]))
    h1, stage, route, cnt_f = _merge(x2, ya, yc, row2(g_mix), w_in_b[:, conv_end:], row2(b_in[conv_end:]),
                                     w_pa.astype(BF16), w_pc.astype(BF16), w_o.astype(BF16),
                                     row2(g_ffn), wrt, brt)

    n_tiles = t // TM
    cnt = cnt_f[:, 0, ROUTE_OFF:ROUTE_OFF + N_EXPERTS].astype(jnp.int32)
    tile_off = jnp.cumsum(cnt, axis=1) - cnt
    tot = cnt.sum(axis=0)
    pcounts = (tot + ROW_BLK - 1) // ROW_BLK * ROW_BLK
    pends = jnp.cumsum(pcounts)
    pstarts = pends - pcounts
    cum = jnp.cumsum(cnt, axis=0) - cnt
    dst = pstarts[None, :] + cum
    n_blocks = (t * TOP_K) // ROW_BLK + N_EXPERTS
    blk_start = jnp.arange(n_blocks, dtype=jnp.int32) * ROW_BLK
    blk_e = jnp.minimum((pends[None, :] <= blk_start[:, None]).sum(axis=1), N_EXPERTS - 1).astype(jnp.int32)
    nblk = (pends[-1:] // ROW_BLK).astype(jnp.int32)
    sel = (jnp.arange(N_EXPERTS, dtype=jnp.int32)[:, None] == blk_e[None, :]).astype(jnp.int32)
    of_block = lambda a: (a[..., None] * sel).sum(axis=-2)
    base = blk_start - of_block(pstarts)
    nvalid = jnp.clip(of_block(tot) - base, 0, ROW_BLK)
    cum_e = of_block(cum)
    cnt_e = of_block(cnt)
    jlo = (cum_e + cnt_e <= base[None, :]).sum(axis=0)
    jhi = (cum_e < base[None, :] + ROW_BLK).sum(axis=0)
    flat = lambda a: a.reshape(-1).astype(jnp.int32)

    ys = _experts(blk_e, nblk, flat(base), flat(jlo), flat(jhi), flat(nvalid), flat(cnt), flat(tile_off),
                  flat(cum), stage, w1, w3, w2)
    out = _combine(flat(cnt), flat(tile_off), flat(dst), h1, route, p_i.reshape(t, PLE_DIM), row2(g_ple),
                   w_ple_gate.astype(BF16), row2(b_ple_gate), w_ple_proj.astype(BF16), ys)
    return out.reshape(b, s, d)


def kernel(x, p, g_mix, w_in, b_in, g_q, g_k, rel_bias, conv_w, conv_b, w_pa, w_pc, w_o, g_ffn, w_group, b_group, w_router, b_router, w1, w3, w2, g_ple, w_ple_gate, b_ple_gate, w_ple_proj):
    h = x
    for i in range(p.shape[0]):
        h = _layer(h, p[i], g_mix[i], w_in[i], b_in[i], g_q[i], g_k[i], rel_bias[i], conv_w[i], conv_b[i],
                   w_pa[i], w_pc[i], w_o[i], g_ffn[i], w_group[i], b_group[i], w_router[i], b_router[i],
                   w1[i], w3[i], w2[i], g_ple[i], w_ple_gate[i], b_ple_gate[i], w_ple_proj[i])
    return h
```

```python
import functools

import jax
import jax.numpy as jnp
from jax import lax
from jax.experimental import pallas as pl
from jax.experimental.pallas import tpu as pltpu

D_MODEL = 1024
CHUNK = 64
LEFT_CHUNKS = 8
N_HEADS = 8
HEAD_DIM = 64
ATTN_W = N_HEADS * HEAD_DIM
CONV_W = D_MODEL // 2
CONV_K = 3
MAX_REL_PAST = 256
PLE_DIM = 256
N_GROUPS = 4
EXPERTS_PER_GROUP = 8
N_EXPERTS = N_GROUPS * EXPERTS_PER_GROUP
TOP_K = 2
D_EXPERT = 512
EPS = 1e-6
NEG = -1e30
LOG2E = 1.4426950408889634

LANES = 128
SUBLANES = 8
TM = 256
TQ = 256
KV_SLABS = 1 + (LEFT_CHUNKS * CHUNK) // TQ
ROW_BLK = 512
TILE_ROWS = TOP_K * TM
ROUTE_OFF = N_GROUPS
PACK_ROWS = D_MODEL // (2 * LANES)
PACK_W = 2 * LANES
SEG_SIZES = tuple(TM >> k for k in range(TM.bit_length()))
GATHER_UNROLL = 40
VMEM_LIMIT = 56 * 1024 * 1024

F32 = jnp.float32
BF16 = jnp.bfloat16
U32 = jnp.uint32


def _dot(a, b):
    return jnp.dot(a, b, preferred_element_type=F32)


def _rms(x, g):
    ms = jnp.mean(x * x, axis=-1, keepdims=True)
    return (x * lax.rsqrt(ms + EPS)) * g


def _sigmoid(x):
    return 1.0 / (1.0 + jnp.exp(-x))


def _pack_rows(ref, vals, n_rows, base=0):
    for a in range(PACK_ROWS):
        lo = vals[:, a * PACK_W:a * PACK_W + LANES].astype(BF16).astype(F32)
        hi = vals[:, a * PACK_W + LANES:(a + 1) * PACK_W].astype(BF16).astype(F32)
        word = (lax.bitcast_convert_type(hi, U32) & U32(0xFFFF0000)) | (lax.bitcast_convert_type(lo, U32) >> 16)
        ref[pl.ds(base + a, n_rows, stride=PACK_ROWS), :] = word


def _unpack_rows(ref, n_rows, base=0, n_valid=None):
    out = []
    if n_valid is not None:
        keep = lax.broadcasted_iota(jnp.int32, (n_rows, LANES), 0) < n_valid
    for a in range(PACK_ROWS):
        word = ref[pl.ds(base + a, n_rows, stride=PACK_ROWS), :]
        if n_valid is not None:
            word = jnp.where(keep, word, U32(0))
        lo = lax.bitcast_convert_type(word << 16, F32).astype(BF16)
        hi = lax.bitcast_convert_type(word & U32(0xFFFF0000), F32).astype(BF16)
        out.append(jnp.concatenate([lo, hi], axis=1))
    return out


def _inproj_kernel(x_ref, g_ref, wqkv_ref, wconv_ref, b_ref, gq_ref, gk_ref, hm_ref,
                   cw_ref, cb_ref, q_ref, k_ref, v_ref, yc_ref, carry_ref, *, tiles_per_seq):
    i = pl.program_id(0)
    nb = _rms(x_ref[...], g_ref[...]).astype(BF16)

    zq = _dot(nb, wqkv_ref[...]) + b_ref[:, 0:3 * ATTN_W]
    hm = hm_ref[...]

    def head_rms(t, g):
        sq = t * t
        hi = sq.astype(BF16)
        lo = (sq - hi.astype(F32)).astype(BF16)
        ms = _dot(hi, hm) + _dot(lo, hm)
        return (t * lax.rsqrt(ms + EPS)) * g

    q_ref[...] = head_rms(zq[:, 0:ATTN_W], gq_ref[...]).astype(BF16)
    k_ref[...] = head_rms(zq[:, ATTN_W:2 * ATTN_W], gk_ref[...]).astype(BF16)
    v_ref[...] = zq[:, 2 * ATTN_W:3 * ATTN_W].astype(BF16)

    zc = _dot(nb, wconv_ref[...]) + b_ref[:, 3 * ATTN_W:3 * ATTN_W + 3 * CONV_W]
    u = zc[:, 0:CONV_W]
    bg = zc[:, CONV_W:2 * CONV_W]
    cg = zc[:, 2 * CONV_W:3 * CONV_W]
    cu = cg * u

    @pl.when((i % tiles_per_seq) == 0)
    def _():
        carry_ref[...] = jnp.zeros_like(carry_ref)

    prev = carry_ref[...]
    carry_ref[...] = cu[TM - SUBLANES:TM, :]
    row = lax.broadcasted_iota(jnp.int32, (SUBLANES, CONV_W), 0)

    def shifted(s):
        r = pltpu.roll(cu, s, 0)
        p = pltpu.roll(prev, s, 0)
        top = jnp.where(row < s, p, r[0:SUBLANES, :])
        return jnp.concatenate([top, r[SUBLANES:, :]], axis=0)

    y = cb_ref[...] + cw_ref[0:1, :] * shifted(2)
    y = y + cw_ref[1:2, :] * shifted(1)
    y = y + cw_ref[2:3, :] * cu
    yc_ref[...] = (bg * y).astype(BF16)


def _inproj(x2, g_mix, wqkv, wconv, b_in, gq, gk, hmat, cw, cb, seq):
    t = x2.shape[0]
    const = lambda i: (0, 0)
    row = lambda i: (i, 0)
    out = jax.ShapeDtypeStruct((t, ATTN_W), BF16)
    return pl.pallas_call(
        functools.partial(_inproj_kernel, tiles_per_seq=seq // TM),
        grid=(t // TM,),
        in_specs=[
            pl.BlockSpec((TM, D_MODEL), row),
            pl.BlockSpec((1, D_MODEL), const),
            pl.BlockSpec((D_MODEL, 3 * ATTN_W), const),
            pl.BlockSpec((D_MODEL, 3 * CONV_W), const),
            pl.BlockSpec((1, 3 * ATTN_W + 3 * CONV_W), const),
            pl.BlockSpec((1, ATTN_W), const),
            pl.BlockSpec((1, ATTN_W), const),
            pl.BlockSpec((ATTN_W, ATTN_W), const),
            pl.BlockSpec((SUBLANES, CONV_W), const),
            pl.BlockSpec((1, CONV_W), const),
        ],
        out_specs=[pl.BlockSpec((TM, ATTN_W), row)] * 4,
        out_shape=[out] * 4,
        scratch_shapes=[pltpu.VMEM((SUBLANES, CONV_W), F32)],
        compiler_params=pltpu.CompilerParams(
            dimension_semantics=("arbitrary",), vmem_limit_bytes=VMEM_LIMIT),
        name="inproj",
    )(x2, g_mix, wqkv, wconv, b_in, gq, gk, hmat, cw, cb)


def _lane_fold(parts, op):
    acc = None
    for a in parts:
        for c in range(0, a.shape[1], LANES):
            piece = a[:, c:c + LANES]
            acc = piece if acc is None else op(acc, piece)
    return acc


def _attn_kernel(q_ref, k0_ref, k1_ref, k2_ref, v0_ref, v1_ref, v2_ref, bias_ref, o_ref):
    k_refs = (k0_ref, k1_ref, k2_ref)
    v_refs = (v0_ref, v1_ref, v2_ref)
    pair_w = 2 * HEAD_DIM
    lane = lax.broadcasted_iota(jnp.int32, (TQ, pair_w), 1)
    low = lane < HEAD_DIM

    def scores(h):
        ps = slice((h // 2) * pair_w, (h // 2 + 1) * pair_w)
        q_pair = q_ref[0, :, ps]
        own = low if h % 2 == 0 else jnp.logical_not(low)
        qh = jnp.where(own, q_pair, jnp.zeros_like(q_pair))
        return [lax.dot_general(qh, k_refs[j][0, :, ps], (((1,), (1,)), ((), ())),
                                preferred_element_type=F32) + bias_ref[0, h, :, j * TQ:(j + 1) * TQ]
                for j in range(KV_SLABS)]

    def weighted(h, s):
        ps = slice((h // 2) * pair_w, (h // 2 + 1) * pair_w)
        m = _lane_fold(s, jnp.maximum).max(axis=-1, keepdims=True)
        e = [jnp.exp2(sj - m) for sj in s]
        l = _lane_fold(e, jnp.add).sum(axis=-1, keepdims=True)
        acc = None
        for j in range(KV_SLABS):
            oj = _dot(e[j].astype(BF16), v_refs[j][0, :, ps])
            acc = oj if acc is None else acc + oj
        return acc * (1.0 / l)

    s_next = scores(0)
    o_even = None
    for h in range(N_HEADS):
        s_cur = s_next
        if h + 1 < N_HEADS:
            s_next = scores(h + 1)
        o = weighted(h, s_cur)
        if h % 2 == 0:
            o_even = o
        else:
            ps = slice((h // 2) * pair_w, (h // 2 + 1) * pair_w)
            o_ref[0, :, ps] = jnp.where(low, o_even, o).astype(BF16)


def _attention(q, k, v, bias):
    b, s, _ = q.shape
    blk = (1, TQ, ATTN_W)

    def kv_map(j):
        back = KV_SLABS - 1 - j
        return lambda bi, i: (bi, jnp.maximum(i - back, 0), 0)

    kv_specs = [pl.BlockSpec(blk, kv_map(j)) for j in range(KV_SLABS)]
    n_var = bias.shape[0]
    return pl.pallas_call(
        _attn_kernel,
        grid=(b, s // TQ),
        in_specs=[pl.BlockSpec(blk, lambda bi, i: (bi, i, 0))] + kv_specs + kv_specs + [
            pl.BlockSpec((1, N_HEADS, TQ, KV_SLABS * TQ),
                         lambda bi, i: (jnp.minimum(i, n_var - 1), 0, 0, 0))],
        out_specs=pl.BlockSpec(blk, lambda bi, i: (bi, i, 0)),
        out_shape=jax.ShapeDtypeStruct((b, s, ATTN_W), BF16),
        compiler_params=pltpu.CompilerParams(
            dimension_semantics=("arbitrary", "arbitrary"), vmem_limit_bytes=VMEM_LIMIT),
        name="attn",
    )(q, k, k, k, v, v, v, bias)


def _attn_bias(rel_bias):
    nk = KV_SLABS * TQ
    past = nk - TQ
    d = jnp.arange(TQ - 1 + past, -TQ, -1)
    idx = jnp.clip(d, -(CHUNK - 1), MAX_REL_PAST) + (CHUNK - 1)
    onehot = (idx[:, None] == jnp.arange(rel_bias.shape[1])[None, :]).astype(F32)
    per_dist = jnp.einsum("dn,hn->hd", onehot, rel_bias.astype(F32) * LOG2E,
                          precision=lax.Precision.HIGHEST)
    rows = [per_dist[:, TQ - 1 - r:TQ - 1 - r + nk] for r in range(TQ)]
    table = jnp.stack(rows, axis=1)
    r = jnp.arange(TQ)[:, None]
    c = jnp.arange(nk)[None, :]
    qc = r // CHUNK
    kc = c // CHUNK
    lead = past // CHUNK - LEFT_CHUNKS
    band = (kc >= qc + lead) & (kc <= qc + lead + LEFT_CHUNKS)
    variants = []
    for var in range(KV_SLABS):
        valid = band & (c >= (KV_SLABS - 1 - var) * TQ)
        variants.append(jnp.where(valid[None], table, NEG))
    return jnp.stack(variants, axis=0)


def _merge_kernel(x_ref, ya_ref, yc_ref, g_ref, wg_ref, bgate_ref, wpa_ref, wpc_ref, wo_ref,
                  gffn_ref, wrt_ref, brt_ref, h_ref, stage_ref, route_ref, cnt_ref):
    x = x_ref[...]
    nb = _rms(x, g_ref[...]).astype(BF16)
    gates = _dot(nb, wg_ref[...]) + bgate_ref[...]
    sga = _sigmoid(gates[:, 0:D_MODEL])
    sgc = _sigmoid(gates[:, D_MODEL:2 * D_MODEL])
    m = sga * _dot(ya_ref[...], wpa_ref[...]) + sgc * _dot(yc_ref[...], wpc_ref[...])
    h = x + _dot(m.astype(BF16), wo_ref[...])
    h_ref[...] = h

    n2 = _rms(h, gffn_ref[...]).astype(BF16)
    logits = _dot(n2, wrt_ref[...]) + brt_ref[...]
    lane = lax.broadcasted_iota(jnp.int32, (TM, LANES), 1).astype(F32)
    ninf = -jnp.inf

    def argmax_first(vals):
        mx = vals.max(axis=-1, keepdims=True)
        idx = jnp.where(vals == mx, lane, float(LANES)).min(axis=-1, keepdims=True)
        return mx, idx

    gmask = lane < N_GROUPS
    gmax, grp = argmax_first(jnp.where(gmask, logits, ninf))
    gsum = jnp.where(gmask, jnp.exp(logits - gmax), 0.0).sum(axis=-1, keepdims=True)
    p_grp = 1.0 / gsum
    first = ROUTE_OFF + EXPERTS_PER_GROUP * grp
    el = jnp.where((lane >= first) & (lane < first + EXPERTS_PER_GROUP), logits, ninf)
    l1, i1 = argmax_first(el)
    l2, i2 = argmax_first(jnp.where(lane == i1, ninf, el))
    e2 = jnp.exp(l2 - l1)
    den = 1.0 + e2
    w1 = p_grp * (1.0 / den)
    w2 = p_grp * (e2 / den)

    oh1 = (lane == i1).astype(F32)
    oh2 = (lane == i2).astype(F32)
    oh = (oh1 + oh2).astype(BF16)
    r = lax.broadcasted_iota(jnp.int32, (TM, TM), 0)
    c = lax.broadcasted_iota(jnp.int32, (TM, TM), 1)
    earlier_tok = _dot((c < r).astype(BF16), oh)
    er = lax.broadcasted_iota(jnp.int32, (LANES, LANES), 0)
    ec = lax.broadcasted_iota(jnp.int32, (LANES, LANES), 1)
    lower_exp = _dot(oh, (er < ec).astype(BF16)).sum(axis=0, keepdims=True)
    where = earlier_tok + lower_exp
    pos1 = (oh1 * where).sum(axis=-1, keepdims=True)
    pos2 = (oh2 * where).sum(axis=-1, keepdims=True)
    cnt_ref[0] = jnp.broadcast_to(oh.astype(F32).sum(axis=0, keepdims=True), (SUBLANES, LANES))

    slot = lax.broadcasted_iota(jnp.int32, (TM, TILE_ROWS), 1).astype(F32)
    place = ((slot == pos1) | (slot == pos2)).astype(BF16)
    sorted_rows = lax.dot_general(place, n2, (((0,), (0,)), ((), ())), preferred_element_type=F32)
    _pack_rows(stage_ref, sorted_rows, TILE_ROWS)

    cols = (i1 - ROUTE_OFF, i2 - ROUTE_OFF, w1, w2, pos1, pos2)
    route = jnp.zeros((TM, LANES), F32)
    for j, col in enumerate(cols):
        route = jnp.where(lane == j, col, route)
    route_ref[...] = route


def _merge(x2, ya, yc, g_mix, wgate, bgate, wpa, wpc, wo, gffn, wrt, brt):
    t = x2.shape[0]
    n_tiles = t // TM
    const = lambda i: (0, 0)
    row = lambda i: (i, 0)
    return pl.pallas_call(
        _merge_kernel,
        grid=(n_tiles,),
        in_specs=[
            pl.BlockSpec((TM, D_MODEL), row),
            pl.BlockSpec((TM, ATTN_W), row),
            pl.BlockSpec((TM, CONV_W), row),
            pl.BlockSpec((1, D_MODEL), const),
            pl.BlockSpec((D_MODEL, 2 * D_MODEL), const),
            pl.BlockSpec((1, 2 * D_MODEL), const),
            pl.BlockSpec((ATTN_W, D_MODEL), const),
            pl.BlockSpec((CONV_W, D_MODEL), const),
            pl.BlockSpec((D_MODEL, D_MODEL), const),
            pl.BlockSpec((1, D_MODEL), const),
            pl.BlockSpec((D_MODEL, LANES), const),
            pl.BlockSpec((1, LANES), const),
        ],
        out_specs=[
            pl.BlockSpec((TM, D_MODEL), row),
            pl.BlockSpec((TILE_ROWS * PACK_ROWS, LANES), row),
            pl.BlockSpec((TM, LANES), row),
            pl.BlockSpec((1, SUBLANES, LANES), lambda i: (i, 0, 0)),
        ],
        out_shape=[
            jax.ShapeDtypeStruct((t, D_MODEL), F32),
            jax.ShapeDtypeStruct((n_tiles * TILE_ROWS * PACK_ROWS, LANES), U32),
            jax.ShapeDtypeStruct((t, LANES), F32),
            jax.ShapeDtypeStruct((n_tiles, SUBLANES, LANES), F32),
        ],
        compiler_params=pltpu.CompilerParams(
            dimension_semantics=("arbitrary",), vmem_limit_bytes=VMEM_LIMIT),
        name="merge",
    )(x2, ya, yc, g_mix, wgate, bgate, wpa, wpc, wo, gffn, wrt, brt)


def _piece_counts(n):
    return [(n >> (size.bit_length() - 1)) & 1 for size in SEG_SIZES]


def _segment_pieces(n, visit):
    for cls, size in enumerate(SEG_SIZES):
        @pl.when((n & size) != 0)
        def _(cls=cls, size=size):
            visit(cls, n & ~(2 * size - 1))


def _piece_copy(src_ref, dst_ref, sems, cls, src_row, dst_row):
    n = SEG_SIZES[cls] * PACK_ROWS
    return pltpu.make_async_copy(src_ref.at[pl.ds(src_row * PACK_ROWS, n), :],
                                 dst_ref.at[pl.ds(dst_row * PACK_ROWS, n), :], sems.at[cls])


def _drain(src_ref, dst_ref, sems, counts):
    unroll = 4
    for cls in range(len(SEG_SIZES)):
        def wait_some(k, cls=cls):
            def body(t, carry):
                for _ in range(k):
                    _piece_copy(src_ref, dst_ref, sems, cls, 0, 0).wait()
                return carry
            return body

        n = counts[cls]
        lax.fori_loop(0, n >> 2, wait_some(unroll), 0)
        lax.fori_loop(0, n & (unroll - 1), wait_some(1), 0)


def _experts_kernel(be_ref, nb_ref, base_ref, jlo_ref, jhi_ref, nv_ref, npiece_ref, cum_ref, end_ref, src_ref,
                    stage_hbm, w1_ref, w3_ref, w2_ref, ys_ref,
                    xbuf_a, xbuf_b, w1b_ref, w3b_ref, w2b_ref, sems, *, n_tiles, n_blocks):
    s = pl.program_id(0)
    nb = nb_ref[0]
    n_cls = len(SEG_SIZES)

    def segment_copies(step, j, live, buf, sem):
        base = base_ref[step]
        g = jnp.minimum(j, n_tiles - 1) * N_EXPERTS + be_ref[step]
        lo = jnp.maximum(cum_ref[g], base)
        hi = jnp.minimum(end_ref[g], base + ROW_BLK)
        src = src_ref[g] + lo
        dst = lo - base
        _segment_pieces(jnp.where(live, jnp.maximum(hi - lo, 0), 0),
                        lambda cls, o: _piece_copy(stage_hbm, buf, sem, cls, src + o, dst + o).start())

    def looped_copies(step, j0, j1, buf, sem):
        def body(j, carry):
            segment_copies(step, j, True, buf, sem)
            return carry

        lax.fori_loop(j0, j1, body, 0)

    @pl.when(s == 0)
    def _():
        xbuf_a[...] = jnp.zeros_like(xbuf_a)
        xbuf_b[...] = jnp.zeros_like(xbuf_b)
        looped_copies(0, jlo_ref[0], jhi_ref[0], xbuf_a, sems.at[0])

    def step(cur, cur_sem, nxt, nxt_sem):
        prev = be_ref[jnp.maximum(s - 1, 0)]

        @pl.when((s == 0) | (be_ref[s] != prev))
        def _():
            w1b_ref[...] = w1_ref[0].astype(BF16)
            w3b_ref[...] = w3_ref[0].astype(BF16)
            w2b_ref[...] = w2_ref[0].astype(BF16)

        _drain(stage_hbm, cur, cur_sem, [npiece_ref[s * n_cls + c] for c in range(n_cls)])

        nxt_step = jnp.minimum(s + 1, n_blocks - 1)
        live = s + 1 < nb
        j0 = jlo_ref[nxt_step]
        j1 = jnp.where(live, jhi_ref[nxt_step], j0)
        looped_copies(nxt_step, j0 + GATHER_UNROLL, j1, nxt, nxt_sem)
        for k in range(GATHER_UNROLL):
            segment_copies(nxt_step, j0 + k, j0 + k < j1, nxt, nxt_sem)

        a = None
        g = None
        for blk, xa in enumerate(_unpack_rows(cur, ROW_BLK, n_valid=nv_ref[s])):
            rows = slice(blk * PACK_W, (blk + 1) * PACK_W)
            da = _dot(xa, w1b_ref[rows, :])
            dg = _dot(xa, w3b_ref[rows, :])
            a = da if a is None else a + da
            g = dg if g is None else g + dg
        hdn = (a * _sigmoid(a)) * g
        _pack_rows(ys_ref, _dot(hdn.astype(BF16), w2b_ref[...]), ROW_BLK)

    @pl.when((s < nb) & (s % 2 == 0))
    def _():
        step(xbuf_a, sems.at[0], xbuf_b, sems.at[1])

    @pl.when((s < nb) & (s % 2 == 1))
    def _():
        step(xbuf_b, sems.at[1], xbuf_a, sems.at[0])

    @pl.when(s >= nb)
    def _():
        ys_ref[...] = jnp.zeros_like(ys_ref)


def _experts(blk_e, nblk, base, jlo, jhi, nvalid, npiece, cum, end, src, stage, w1, w3, w2):
    n_blocks = blk_e.shape[0]
    n_tiles = cum.shape[0] // N_EXPERTS

    def wsel(s, be, nb, *_):
        return (be[jnp.minimum(s, nb[0] - 1)], 0, 0)

    grid_spec = pltpu.PrefetchScalarGridSpec(
        num_scalar_prefetch=10,
        grid=(n_blocks,),
        in_specs=[
            pl.BlockSpec(memory_space=pl.ANY),
            pl.BlockSpec((1, D_MODEL, D_EXPERT), wsel),
            pl.BlockSpec((1, D_MODEL, D_EXPERT), wsel),
            pl.BlockSpec((1, D_EXPERT, D_MODEL), wsel),
        ],
        out_specs=pl.BlockSpec((ROW_BLK * PACK_ROWS, LANES), lambda s, *_: (s, 0)),
        scratch_shapes=[
            pltpu.VMEM((ROW_BLK * PACK_ROWS, LANES), U32),
            pltpu.VMEM((ROW_BLK * PACK_ROWS, LANES), U32),
            pltpu.VMEM((D_MODEL, D_EXPERT), BF16),
            pltpu.VMEM((D_MODEL, D_EXPERT), BF16),
            pltpu.VMEM((D_EXPERT, D_MODEL), BF16),
            pltpu.SemaphoreType.DMA((2, len(SEG_SIZES))),
        ],
    )
    return pl.pallas_call(
        functools.partial(_experts_kernel, n_tiles=n_tiles, n_blocks=n_blocks),
        grid_spec=grid_spec,
        out_shape=jax.ShapeDtypeStruct((n_blocks * ROW_BLK * PACK_ROWS, LANES), U32),
        compiler_params=pltpu.CompilerParams(
            dimension_semantics=("arbitrary",), vmem_limit_bytes=VMEM_LIMIT),
        name="experts",
    )(blk_e, nblk, base, jlo, jhi, nvalid, npiece, cum, end, src, stage, w1, w3, w2)


def _combine_kernel(cnt_ref, off_ref, dst_ref, npiece_ref, h_ref, route_ref, p_ref, gple_ref, wpg_ref, bpg_ref,
                    wpp_ref, ys_hbm, o_ref, ybuf_a, ybuf_b, sems, *, n_steps):
    i = pl.program_id(0)
    n_cls = len(SEG_SIZES)

    def segment_copies(step, e, live, buf, sem):
        g = step * N_EXPERTS + e
        off, dst = off_ref[g], dst_ref[g]
        _segment_pieces(jnp.where(live, cnt_ref[g], 0),
                        lambda cls, o: _piece_copy(ys_hbm, buf, sem, cls, dst + o, off + o).start())

    @pl.when(i == 0)
    def _():
        def body(e, carry):
            segment_copies(0, e, True, ybuf_a, sems.at[0])
            return carry

        lax.fori_loop(0, N_EXPERTS, body, 0)

    def step(cur, cur_sem, nxt, nxt_sem):
        _drain(ys_hbm, cur, cur_sem, [npiece_ref[i * n_cls + c] for c in range(n_cls)])
        nxt_step = jnp.minimum(i + 1, n_steps - 1)
        for e in range(N_EXPERTS):
            segment_copies(nxt_step, e, i + 1 < n_steps, nxt, nxt_sem)

        pp = _dot(p_ref[...].astype(BF16), wpp_ref[...])
        route = route_ref[...]
        place = lax.broadcasted_iota(jnp.int32, (TM, TILE_ROWS), 1).astype(F32)
        sel = [(place == route[:, 4 + kk:5 + kk]).astype(BF16) for kk in range(TOP_K)]
        cols = _unpack_rows(cur, TILE_ROWS)
        picked = [jnp.concatenate([_dot(sel[kk], blk) for blk in cols], axis=1) for kk in range(TOP_K)]
        moe = picked[0] * route[:, 2:3] + picked[1] * route[:, 3:4]
        h = h_ref[...] + moe
        gate = _sigmoid(_dot(_rms(h, gple_ref[...]).astype(BF16), wpg_ref[...]) + bpg_ref[...])
        o_ref[...] = h + gate * pp

    @pl.when(i % 2 == 0)
    def _():
        step(ybuf_a, sems.at[0], ybuf_b, sems.at[1])

    @pl.when(i % 2 == 1)
    def _():
        step(ybuf_b, sems.at[1], ybuf_a, sems.at[0])


def _combine(cnt, off, dst, npiece, h1, route, p2, gple, wpg, bpg, wpp, ys):
    t = h1.shape[0]
    n_steps = t // TM
    const = lambda i, *_: (0, 0)
    row = lambda i, *_: (i, 0)
    grid_spec = pltpu.PrefetchScalarGridSpec(
        num_scalar_prefetch=4,
        grid=(n_steps,),
        in_specs=[
            pl.BlockSpec((TM, D_MODEL), row),
            pl.BlockSpec((TM, LANES), row),
            pl.BlockSpec((TM, PLE_DIM), row),
            pl.BlockSpec((1, D_MODEL), const),
            pl.BlockSpec((D_MODEL, D_MODEL), const),
            pl.BlockSpec((1, D_MODEL), const),
            pl.BlockSpec((PLE_DIM, D_MODEL), const),
            pl.BlockSpec(memory_space=pl.ANY),
        ],
        out_specs=pl.BlockSpec((TM, D_MODEL), row),
        scratch_shapes=[
            pltpu.VMEM((TILE_ROWS * PACK_ROWS, LANES), U32),
            pltpu.VMEM((TILE_ROWS * PACK_ROWS, LANES), U32),
            pltpu.SemaphoreType.DMA((2, len(SEG_SIZES))),
        ],
    )
    return pl.pallas_call(
        functools.partial(_combine_kernel, n_steps=n_steps),
        grid_spec=grid_spec,
        out_shape=jax.ShapeDtypeStruct((t, D_MODEL), F32),
        compiler_params=pltpu.CompilerParams(
            dimension_semantics=("arbitrary",), vmem_limit_bytes=VMEM_LIMIT),
        name="combine",
    )(cnt, off, dst, npiece, h1, route, p2, gple, wpg, bpg, wpp, ys)


def _layer(h, p_i, g_mix, w_in, b_in, g_q, g_k, rel_bias, conv_w, conv_b, w_pa, w_pc, w_o,
           g_ffn, w_group, b_group, w_router, b_router, w1, w3, w2,
           g_ple, w_ple_gate, b_ple_gate, w_ple_proj):
    b, s, d = h.shape
    t = b * s
    x2 = h.reshape(t, d)
    row2 = lambda a: a.reshape(1, -1).astype(F32)

    qkv_w = 3 * ATTN_W
    conv_end = qkv_w + 3 * CONV_W
    w_in_b = w_in.astype(BF16)
    gq = row2(jnp.tile(g_q.astype(F32) * (HEAD_DIM ** -0.5 * LOG2E), N_HEADS))
    gk = row2(jnp.tile(g_k.astype(F32), N_HEADS))
    head = jnp.arange(ATTN_W) // HEAD_DIM
    hmat = jnp.where(head[:, None] == head[None, :], 1.0 / HEAD_DIM, 0.0).astype(BF16)
    cw = jnp.concatenate([conv_w.astype(F32), jnp.zeros((SUBLANES - CONV_K, CONV_W), F32)], axis=0)

    q, k, v, yc = _inproj(x2, row2(g_mix), w_in_b[:, :qkv_w], w_in_b[:, qkv_w:conv_end],
                          row2(b_in[:conv_end]), gq, gk, hmat, cw, row2(conv_b), s)

    ya = _attention(q.reshape(b, s, ATTN_W), k.reshape(b, s, ATTN_W), v.reshape(b, s, ATTN_W),
                    _attn_bias(rel_bias)).reshape(t, ATTN_W)

    n_pad = LANES - N_GROUPS - N_EXPERTS
    wrt = jnp.concatenate([w_group, w_router, jnp.zeros((d, n_pad), w_group.dtype)], axis=1).astype(BF16)
    brt = row2(jnp.concatenate([b_group, b_router, jnp.zeros((n_pad,), b_group.dtype)]))
    h1, stage, route, cnt_f = _merge(x2, ya, yc, row2(g_mix), w_in_b[:, conv_end:], row2(b_in[conv_end:]),
                                     w_pa.astype(BF16), w_pc.astype(BF16), w_o.astype(BF16),
                                     row2(g_ffn), wrt, brt)

    n_tiles = t // TM
    cnt = cnt_f[:, 0, ROUTE_OFF:ROUTE_OFF + N_EXPERTS].astype(jnp.int32)
    tile_off = jnp.cumsum(cnt, axis=1) - cnt
    tot = cnt.sum(axis=0)
    pcounts = (tot + ROW_BLK - 1) // ROW_BLK * ROW_BLK
    pends = jnp.cumsum(pcounts)
    pstarts = pends - pcounts
    cum = jnp.cumsum(cnt, axis=0) - cnt
    dst = pstarts[None, :] + cum
    n_blocks = (t * TOP_K) // ROW_BLK + N_EXPERTS
    blk_start = jnp.arange(n_blocks, dtype=jnp.int32) * ROW_BLK
    blk_e = jnp.minimum((pends[None, :] <= blk_start[:, None]).sum(axis=1), N_EXPERTS - 1).astype(jnp.int32)
    nblk = (pends[-1:] // ROW_BLK).astype(jnp.int32)
    sel = (jnp.arange(N_EXPERTS, dtype=jnp.int32)[:, None] == blk_e[None, :]).astype(jnp.int32)
    of_block = lambda a: (a[..., None] * sel).sum(axis=-2)
    base = blk_start - of_block(pstarts)
    nvalid = jnp.clip(of_block(tot) - base, 0, ROW_BLK)
    cum_e = of_block(cum)
    cnt_e = of_block(cnt)
    jlo = (cum_e + cnt_e <= base[None, :]).sum(axis=0)
    jhi = (cum_e < base[None, :] + ROW_BLK).sum(axis=0)
    part = jnp.clip(jnp.minimum(cum_e + cnt_e, base[None, :] + ROW_BLK) - jnp.maximum(cum_e, base[None, :]),
                    0, ROW_BLK)
    blk_pieces = jnp.stack(_piece_counts(part), axis=-1).sum(axis=0)
    tile_pieces = jnp.stack(_piece_counts(cnt), axis=-1).sum(axis=1)
    src = jnp.arange(n_tiles, dtype=jnp.int32)[:, None] * TILE_ROWS + tile_off - cum
    flat = lambda a: a.reshape(-1).astype(jnp.int32)

    ys = _experts(blk_e, nblk, flat(base), flat(jlo), flat(jhi), flat(nvalid), flat(blk_pieces), flat(cum),
                  flat(cum + cnt), flat(src), stage, w1, w3, w2)
    out = _combine(flat(cnt), flat(tile_off), flat(dst), flat(tile_pieces), h1, route, p_i.reshape(t, PLE_DIM),
                   row2(g_ple), w_ple_gate.astype(BF16), row2(b_ple_gate), w_ple_proj.astype(BF16), ys)
    return out.reshape(b, s, d)


def kernel(x, p, g_mix, w_in, b_in, g_q, g_k, rel_bias, conv_w, conv_b, w_pa, w_pc, w_o, g_ffn, w_group, b_group, w_router, b_router, w1, w3, w2, g_ple, w_ple_gate, b_ple_gate, w_ple_proj):
    h = x
    for i in range(p.shape[0]):
        h = _layer(h, p[i], g_mix[i], w_in[i], b_in[i], g_q[i], g_k[i], rel_bias[i], conv_w[i], conv_b[i],
                   w_pa[i], w_pc[i], w_o[i], g_ffn[i], w_group[i], b_group[i], w_router[i], b_router[i],
                   w1[i], w3[i], w2[i], g_ple[i], w_ple_gate[i], b_ple_gate[i], w_ple_proj[i])
    return h
```

```python
import functools

import jax
import jax.numpy as jnp
from jax import lax
from jax.experimental import pallas as pl
from jax.experimental.pallas import tpu as pltpu

D_MODEL = 1024
CHUNK = 64
LEFT_CHUNKS = 8
N_HEADS = 8
HEAD_DIM = 64
ATTN_W = N_HEADS * HEAD_DIM
CONV_W = D_MODEL // 2
CONV_K = 3
MAX_REL_PAST = 256
PLE_DIM = 256
N_GROUPS = 4
EXPERTS_PER_GROUP = 8
N_EXPERTS = N_GROUPS * EXPERTS_PER_GROUP
TOP_K = 2
D_EXPERT = 512
EPS = 1e-6
NEG = -1e30
LOG2E = 1.4426950408889634

LANES = 128
SUBLANES = 8
TM = 256
TQ = 256
KV_SLABS = 1 + (LEFT_CHUNKS * CHUNK) // TQ
ROW_BLK = 512
TILE_ROWS = TOP_K * TM
ROUTE_OFF = N_GROUPS
PACK_ROWS = D_MODEL // (2 * LANES)
PACK_W = 2 * LANES
SEG_SIZES = tuple(TM >> k for k in range(TM.bit_length()))
MERGE_PARTS = 1
GATHER_UNROLL = 36
VMEM_LIMIT = 56 * 1024 * 1024

F32 = jnp.float32
BF16 = jnp.bfloat16
U32 = jnp.uint32


def _dot(a, b):
    return jnp.dot(a, b, preferred_element_type=F32)


def _rms(x, g):
    ms = jnp.mean(x * x, axis=-1, keepdims=True)
    return (x * lax.rsqrt(ms + EPS)) * g


def _sigmoid(x):
    return 1.0 / (1.0 + jnp.exp(-x))


def _pack_rows(ref, vals, n_rows, base=0):
    for a in range(PACK_ROWS):
        lo = vals[:, a * PACK_W:a * PACK_W + LANES].astype(BF16).astype(F32)
        hi = vals[:, a * PACK_W + LANES:(a + 1) * PACK_W].astype(BF16).astype(F32)
        word = (lax.bitcast_convert_type(hi, U32) & U32(0xFFFF0000)) | (lax.bitcast_convert_type(lo, U32) >> 16)
        ref[pl.ds(base + a, n_rows, stride=PACK_ROWS), :] = word


def _unpack_rows(ref, n_rows, base=0, n_valid=None):
    out = []
    if n_valid is not None:
        keep = lax.broadcasted_iota(jnp.int32, (n_rows, LANES), 0) < n_valid
    for a in range(PACK_ROWS):
        word = ref[pl.ds(base + a, n_rows, stride=PACK_ROWS), :]
        if n_valid is not None:
            word = jnp.where(keep, word, U32(0))
        lo = lax.bitcast_convert_type(word << 16, F32).astype(BF16)
        hi = lax.bitcast_convert_type(word & U32(0xFFFF0000), F32).astype(BF16)
        out.append(jnp.concatenate([lo, hi], axis=1))
    return out


def _inproj_kernel(x_ref, g_ref, wqkv_ref, wconv_ref, b_ref, gq_ref, gk_ref, hm_ref,
                   cw_ref, cb_ref, q_ref, k_ref, v_ref, yc_ref, carry_ref, *, tiles_per_seq):
    i = pl.program_id(0)
    nb = _rms(x_ref[...], g_ref[...]).astype(BF16)

    zq = _dot(nb, wqkv_ref[...]) + b_ref[:, 0:3 * ATTN_W]
    hm = hm_ref[...]

    def head_rms(t, g):
        ms = _dot((t * t).astype(BF16), hm)
        return (t * lax.rsqrt(ms + EPS)) * g

    q_ref[...] = head_rms(zq[:, 0:ATTN_W], gq_ref[...]).astype(BF16)
    k_ref[...] = head_rms(zq[:, ATTN_W:2 * ATTN_W], gk_ref[...]).astype(BF16)
    v_ref[...] = zq[:, 2 * ATTN_W:3 * ATTN_W].astype(BF16)

    zc = _dot(nb, wconv_ref[...]) + b_ref[:, 3 * ATTN_W:3 * ATTN_W + 3 * CONV_W]
    u = zc[:, 0:CONV_W]
    bg = zc[:, CONV_W:2 * CONV_W]
    cg = zc[:, 2 * CONV_W:3 * CONV_W]
    cu = cg * u

    @pl.when((i % tiles_per_seq) == 0)
    def _():
        carry_ref[...] = jnp.zeros_like(carry_ref)

    prev = carry_ref[...]
    carry_ref[...] = cu[TM - SUBLANES:TM, :]
    row = lax.broadcasted_iota(jnp.int32, (SUBLANES, CONV_W), 0)

    def shifted(s):
        r = pltpu.roll(cu, s, 0)
        p = pltpu.roll(prev, s, 0)
        top = jnp.where(row < s, p, r[0:SUBLANES, :])
        return jnp.concatenate([top, r[SUBLANES:, :]], axis=0)

    y = cb_ref[...] + cw_ref[0:1, :] * shifted(2)
    y = y + cw_ref[1:2, :] * shifted(1)
    y = y + cw_ref[2:3, :] * cu
    yc_ref[...] = (bg * y).astype(BF16)


def _inproj(x2, g_mix, wqkv, wconv, b_in, gq, gk, hmat, cw, cb, seq):
    t = x2.shape[0]
    const = lambda i: (0, 0)
    row = lambda i: (i, 0)
    out = jax.ShapeDtypeStruct((t, ATTN_W), BF16)
    return pl.pallas_call(
        functools.partial(_inproj_kernel, tiles_per_seq=seq // TM),
        grid=(t // TM,),
        in_specs=[
            pl.BlockSpec((TM, D_MODEL), row),
            pl.BlockSpec((1, D_MODEL), const),
            pl.BlockSpec((D_MODEL, 3 * ATTN_W), const),
            pl.BlockSpec((D_MODEL, 3 * CONV_W), const),
            pl.BlockSpec((1, 3 * ATTN_W + 3 * CONV_W), const),
            pl.BlockSpec((1, ATTN_W), const),
            pl.BlockSpec((1, ATTN_W), const),
            pl.BlockSpec((ATTN_W, ATTN_W), const),
            pl.BlockSpec((SUBLANES, CONV_W), const),
            pl.BlockSpec((1, CONV_W), const),
        ],
        out_specs=[pl.BlockSpec((TM, ATTN_W), row)] * 4,
        out_shape=[out] * 4,
        scratch_shapes=[pltpu.VMEM((SUBLANES, CONV_W), F32)],
        compiler_params=pltpu.CompilerParams(
            dimension_semantics=("arbitrary",), vmem_limit_bytes=VMEM_LIMIT),
        name="inproj",
    )(x2, g_mix, wqkv, wconv, b_in, gq, gk, hmat, cw, cb)


def _lane_fold(parts, op):
    acc = None
    for a in parts:
        for c in range(0, a.shape[1], LANES):
            piece = a[:, c:c + LANES]
            acc = piece if acc is None else op(acc, piece)
    return acc


def _attn_kernel(q_ref, k0_ref, k1_ref, k2_ref, v0_ref, v1_ref, v2_ref, bias_ref, o_ref):
    k_refs = (k0_ref, k1_ref, k2_ref)
    v_refs = (v0_ref, v1_ref, v2_ref)
    pair_w = 2 * HEAD_DIM
    lane = lax.broadcasted_iota(jnp.int32, (TQ, pair_w), 1)
    low = lane < HEAD_DIM

    def scores(h):
        ps = slice((h // 2) * pair_w, (h // 2 + 1) * pair_w)
        q_pair = q_ref[0, :, ps]
        own = low if h % 2 == 0 else jnp.logical_not(low)
        qh = jnp.where(own, q_pair, jnp.zeros_like(q_pair))
        return [lax.dot_general(qh, k_refs[j][0, :, ps], (((1,), (1,)), ((), ())),
                                preferred_element_type=F32) + bias_ref[0, h, :, j * TQ:(j + 1) * TQ]
                for j in range(KV_SLABS)]

    def weighted(h, s):
        ps = slice((h // 2) * pair_w, (h // 2 + 1) * pair_w)
        m = _lane_fold(s, jnp.maximum).max(axis=-1, keepdims=True)
        e = [jnp.exp2(sj - m) for sj in s]
        l = _lane_fold(e, jnp.add).sum(axis=-1, keepdims=True)
        acc = None
        for j in range(KV_SLABS):
            oj = _dot(e[j].astype(BF16), v_refs[j][0, :, ps])
            acc = oj if acc is None else acc + oj
        return acc * (1.0 / l)

    s_next = scores(0)
    o_even = None
    for h in range(N_HEADS):
        s_cur = s_next
        if h + 1 < N_HEADS:
            s_next = scores(h + 1)
        o = weighted(h, s_cur)
        if h % 2 == 0:
            o_even = o
        else:
            ps = slice((h // 2) * pair_w, (h // 2 + 1) * pair_w)
            o_ref[0, :, ps] = jnp.where(low, o_even, o).astype(BF16)


def _attention(q, k, v, bias):
    b, s, _ = q.shape
    blk = (1, TQ, ATTN_W)

    def kv_map(j):
        back = KV_SLABS - 1 - j
        return lambda bi, i: (bi, jnp.maximum(i - back, 0), 0)

    kv_specs = [pl.BlockSpec(blk, kv_map(j)) for j in range(KV_SLABS)]
    n_var = bias.shape[0]
    return pl.pallas_call(
        _attn_kernel,
        grid=(b, s // TQ),
        in_specs=[pl.BlockSpec(blk, lambda bi, i: (bi, i, 0))] + kv_specs + kv_specs + [
            pl.BlockSpec((1, N_HEADS, TQ, KV_SLABS * TQ),
                         lambda bi, i: (jnp.minimum(i, n_var - 1), 0, 0, 0))],
        out_specs=pl.BlockSpec(blk, lambda bi, i: (bi, i, 0)),
        out_shape=jax.ShapeDtypeStruct((b, s, ATTN_W), BF16),
        compiler_params=pltpu.CompilerParams(
            dimension_semantics=("arbitrary", "arbitrary"), vmem_limit_bytes=VMEM_LIMIT),
        name="attn",
    )(q, k, k, k, v, v, v, bias)


def _attn_bias(rel_bias):
    nk = KV_SLABS * TQ
    past = nk - TQ
    d = jnp.arange(TQ - 1 + past, -TQ, -1)
    idx = jnp.clip(d, -(CHUNK - 1), MAX_REL_PAST) + (CHUNK - 1)
    onehot = (idx[:, None] == jnp.arange(rel_bias.shape[1])[None, :]).astype(F32)
    per_dist = jnp.einsum("dn,hn->hd", onehot, rel_bias.astype(F32) * LOG2E,
                          precision=lax.Precision.HIGHEST)
    n_h, span = per_dist.shape
    padded = jnp.pad(per_dist, ((0, 0), (0, 2)))
    skew = jnp.tile(padded, (1, TQ))[:, :TQ * (span + 1)].reshape(n_h, TQ, span + 1)
    table = skew[:, :, TQ - 1:TQ - 1 + nk]
    r = jnp.arange(TQ)[:, None]
    c = jnp.arange(nk)[None, :]
    qc = r // CHUNK
    kc = c // CHUNK
    lead = past // CHUNK - LEFT_CHUNKS
    band = (kc >= qc + lead) & (kc <= qc + lead + LEFT_CHUNKS)
    variants = []
    for var in range(KV_SLABS):
        valid = band & (c >= (KV_SLABS - 1 - var) * TQ)
        variants.append(jnp.where(valid[None], table, NEG))
    return jnp.stack(variants, axis=0)


def _merge_kernel(x_ref, ya_ref, yc_ref, g_ref, wg_ref, bgate_ref, wpa_ref, wpc_ref, wo_ref,
                  gffn_ref, wrt_ref, brt_ref, h_ref, stage_ref, route_ref, cnt_ref, n2_scr, logit_scr):
    @pl.when(pl.program_id(0) == 0)
    def _():
        n2_scr[...] = jnp.zeros_like(n2_scr)
        logit_scr[...] = jnp.zeros_like(logit_scr)

    n2 = n2_scr[...]
    logits = logit_scr[...]

    for part in range(MERGE_PARTS):
        rows = slice(part * (TM // MERGE_PARTS), (part + 1) * (TM // MERGE_PARTS))
        x = x_ref[rows, :]
        nb = _rms(x, g_ref[...]).astype(BF16)
        gates = _dot(nb, wg_ref[...]) + bgate_ref[...]
        sga = _sigmoid(gates[:, 0:D_MODEL])
        sgc = _sigmoid(gates[:, D_MODEL:2 * D_MODEL])
        m = sga * _dot(ya_ref[rows, :], wpa_ref[...]) + sgc * _dot(yc_ref[rows, :], wpc_ref[...])
        h = x + _dot(m.astype(BF16), wo_ref[...])
        h_ref[rows, :] = h
        n2_new = _rms(h, gffn_ref[...]).astype(BF16)
        n2_scr[rows, :] = n2_new
        logit_scr[rows, :] = _dot(n2_new, wrt_ref[...]) + brt_ref[...]

    lane = lax.broadcasted_iota(jnp.int32, (TM, LANES), 1).astype(F32)
    ninf = -jnp.inf

    def argmax_first(vals):
        mx = vals.max(axis=-1, keepdims=True)
        idx = jnp.where(vals == mx, lane, float(LANES)).min(axis=-1, keepdims=True)
        return mx, idx

    gmask = lane < N_GROUPS
    gmax, grp = argmax_first(jnp.where(gmask, logits, ninf))
    gsum = jnp.where(gmask, jnp.exp(logits - gmax), 0.0).sum(axis=-1, keepdims=True)
    p_grp = 1.0 / gsum
    first = ROUTE_OFF + EXPERTS_PER_GROUP * grp
    el = jnp.where((lane >= first) & (lane < first + EXPERTS_PER_GROUP), logits, ninf)
    l1, i1 = argmax_first(el)
    l2, i2 = argmax_first(jnp.where(lane == i1, ninf, el))
    e2 = jnp.exp(l2 - l1)
    den = 1.0 + e2
    w1 = p_grp * (1.0 / den)
    w2 = p_grp * (e2 / den)

    oh1 = (lane == i1).astype(F32)
    oh2 = (lane == i2).astype(F32)
    oh = (oh1 + oh2).astype(BF16)
    r = lax.broadcasted_iota(jnp.int32, (TM, TM), 0)
    c = lax.broadcasted_iota(jnp.int32, (TM, TM), 1)
    earlier_tok = _dot((c < r).astype(BF16), oh)
    er = lax.broadcasted_iota(jnp.int32, (LANES, LANES), 0)
    ec = lax.broadcasted_iota(jnp.int32, (LANES, LANES), 1)
    lower_exp = _dot(oh, (er < ec).astype(BF16)).sum(axis=0, keepdims=True)
    where = earlier_tok + lower_exp
    pos1 = (oh1 * where).sum(axis=-1, keepdims=True)
    pos2 = (oh2 * where).sum(axis=-1, keepdims=True)
    cnt_ref[0] = jnp.broadcast_to(oh.astype(F32).sum(axis=0, keepdims=True), (SUBLANES, LANES))

    cols = (i1 - ROUTE_OFF, i2 - ROUTE_OFF, w1, w2, pos1, pos2)
    route = jnp.zeros((TM, LANES), F32)
    for j, col in enumerate(cols):
        route = jnp.where(lane == j, col, route)
    route_ref[...] = route

    route_t = route.T
    slot = lax.broadcasted_iota(jnp.int32, (TILE_ROWS, TM), 0).astype(F32)
    place = ((slot == route_t[4:5, :]) | (slot == route_t[5:6, :])).astype(BF16)
    _pack_rows(stage_ref, _dot(place, n2), TILE_ROWS)


def _merge(x2, ya, yc, g_mix, wgate, bgate, wpa, wpc, wo, gffn, wrt, brt):
    t = x2.shape[0]
    n_tiles = t // TM
    const = lambda i: (0, 0)
    row = lambda i: (jnp.minimum(i, n_tiles - 1), 0)
    late = lambda i: (jnp.maximum(i - 1, 0), 0)
    return pl.pallas_call(
        _merge_kernel,
        grid=(n_tiles + 1,),
        in_specs=[
            pl.BlockSpec((TM, D_MODEL), row),
            pl.BlockSpec((TM, ATTN_W), row),
            pl.BlockSpec((TM, CONV_W), row),
            pl.BlockSpec((1, D_MODEL), const),
            pl.BlockSpec((D_MODEL, 2 * D_MODEL), const),
            pl.BlockSpec((1, 2 * D_MODEL), const),
            pl.BlockSpec((ATTN_W, D_MODEL), const),
            pl.BlockSpec((CONV_W, D_MODEL), const),
            pl.BlockSpec((D_MODEL, D_MODEL), const),
            pl.BlockSpec((1, D_MODEL), const),
            pl.BlockSpec((D_MODEL, LANES), const),
            pl.BlockSpec((1, LANES), const),
        ],
        out_specs=[
            pl.BlockSpec((TM, D_MODEL), row),
            pl.BlockSpec((TILE_ROWS * PACK_ROWS, LANES), late),
            pl.BlockSpec((TM, LANES), late),
            pl.BlockSpec((1, SUBLANES, LANES), lambda i: (jnp.maximum(i - 1, 0), 0, 0)),
        ],
        out_shape=[
            jax.ShapeDtypeStruct((t, D_MODEL), F32),
            jax.ShapeDtypeStruct((n_tiles * TILE_ROWS * PACK_ROWS, LANES), U32),
            jax.ShapeDtypeStruct((t, LANES), F32),
            jax.ShapeDtypeStruct((n_tiles, SUBLANES, LANES), F32),
        ],
        scratch_shapes=[pltpu.VMEM((TM, D_MODEL), BF16), pltpu.VMEM((TM, LANES), F32)],
        compiler_params=pltpu.CompilerParams(
            dimension_semantics=("arbitrary",), vmem_limit_bytes=VMEM_LIMIT),
        name="merge",
    )(x2, ya, yc, g_mix, wgate, bgate, wpa, wpc, wo, gffn, wrt, brt)


def _piece_counts(n):
    return [(n >> (size.bit_length() - 1)) & 1 for size in SEG_SIZES]


def _segment_pieces(n, visit):
    for cls, size in enumerate(SEG_SIZES):
        @pl.when((n & size) != 0)
        def _(cls=cls, size=size):
            visit(cls, n & ~(2 * size - 1))


def _piece_copy(src_ref, dst_ref, sems, cls, src_row, dst_row):
    n = SEG_SIZES[cls] * PACK_ROWS
    return pltpu.make_async_copy(src_ref.at[pl.ds(src_row * PACK_ROWS, n), :],
                                 dst_ref.at[pl.ds(dst_row * PACK_ROWS, n), :], sems.at[cls])


def _drain(src_ref, dst_ref, sems, counts):
    unroll = 4
    for cls in range(len(SEG_SIZES)):
        def wait_some(k, cls=cls):
            def body(t, carry):
                for _ in range(k):
                    _piece_copy(src_ref, dst_ref, sems, cls, 0, 0).wait()
                return carry
            return body

        n = counts[cls]
        lax.fori_loop(0, n >> 2, wait_some(unroll), 0)
        lax.fori_loop(0, n & (unroll - 1), wait_some(1), 0)


def _experts_kernel(be_ref, nb_ref, base_ref, jlo_ref, jhi_ref, nv_ref, npiece_ref, cum_ref, end_ref, src_ref,
                    stage_hbm, w1_ref, w3_ref, w2_ref, ys_ref,
                    xbuf_a, xbuf_b, w1b_ref, w3b_ref, w2b_ref, sems, *, n_tiles, n_blocks):
    s = pl.program_id(0)
    nb = nb_ref[0]
    n_cls = len(SEG_SIZES)

    def segment_copies(step, j, live, buf, sem):
        base = base_ref[step]
        g = jnp.minimum(j, n_tiles - 1) * N_EXPERTS + be_ref[step]
        lo = jnp.maximum(cum_ref[g], base)
        hi = jnp.minimum(end_ref[g], base + ROW_BLK)
        src = src_ref[g] + lo
        dst = lo - base
        _segment_pieces(jnp.where(live, jnp.maximum(hi - lo, 0), 0),
                        lambda cls, o: _piece_copy(stage_hbm, buf, sem, cls, src + o, dst + o).start())

    def looped_copies(step, j0, j1, buf, sem):
        def body(j, carry):
            segment_copies(step, j, True, buf, sem)
            return carry

        lax.fori_loop(j0, j1, body, 0)

    @pl.when(s == 0)
    def _():
        xbuf_a[...] = jnp.zeros_like(xbuf_a)
        xbuf_b[...] = jnp.zeros_like(xbuf_b)
        looped_copies(0, jlo_ref[0], jhi_ref[0], xbuf_a, sems.at[0])

    def step(cur, cur_sem, nxt, nxt_sem):
        prev = be_ref[jnp.maximum(s - 1, 0)]

        @pl.when((s == 0) | (be_ref[s] != prev))
        def _():
            w1b_ref[...] = w1_ref[0].astype(BF16)
            w3b_ref[...] = w3_ref[0].astype(BF16)
            w2b_ref[...] = w2_ref[0].astype(BF16)

        _drain(stage_hbm, cur, cur_sem, [npiece_ref[s * n_cls + c] for c in range(n_cls)])

        nxt_step = jnp.minimum(s + 1, n_blocks - 1)
        live = s + 1 < nb
        j0 = jlo_ref[nxt_step]
        j1 = jnp.where(live, jhi_ref[nxt_step], j0)
        looped_copies(nxt_step, j0 + GATHER_UNROLL, j1, nxt, nxt_sem)
        for k in range(GATHER_UNROLL):
            segment_copies(nxt_step, j0 + k, j0 + k < j1, nxt, nxt_sem)

        a = None
        g = None
        for blk, xa in enumerate(_unpack_rows(cur, ROW_BLK, n_valid=nv_ref[s])):
            rows = slice(blk * PACK_W, (blk + 1) * PACK_W)
            da = _dot(xa, w1b_ref[rows, :])
            dg = _dot(xa, w3b_ref[rows, :])
            a = da if a is None else a + da
            g = dg if g is None else g + dg
        hdn = (a * _sigmoid(a)) * g
        _pack_rows(ys_ref, _dot(hdn.astype(BF16), w2b_ref[...]), ROW_BLK)

    @pl.when((s < nb) & (s % 2 == 0))
    def _():
        step(xbuf_a, sems.at[0], xbuf_b, sems.at[1])

    @pl.when((s < nb) & (s % 2 == 1))
    def _():
        step(xbuf_b, sems.at[1], xbuf_a, sems.at[0])

    @pl.when(s >= nb)
    def _():
        ys_ref[...] = jnp.zeros_like(ys_ref)


def _experts(blk_e, nblk, base, jlo, jhi, nvalid, npiece, cum, end, src, stage, w1, w3, w2):
    n_blocks = blk_e.shape[0]
    n_tiles = cum.shape[0] // N_EXPERTS

    def wsel(s, be, nb, *_):
        return (be[jnp.minimum(s, nb[0] - 1)], 0, 0)

    grid_spec = pltpu.PrefetchScalarGridSpec(
        num_scalar_prefetch=10,
        grid=(n_blocks,),
        in_specs=[
            pl.BlockSpec(memory_space=pl.ANY),
            pl.BlockSpec((1, D_MODEL, D_EXPERT), wsel),
            pl.BlockSpec((1, D_MODEL, D_EXPERT), wsel),
            pl.BlockSpec((1, D_EXPERT, D_MODEL), wsel),
        ],
        out_specs=pl.BlockSpec((ROW_BLK * PACK_ROWS, LANES), lambda s, *_: (s, 0)),
        scratch_shapes=[
            pltpu.VMEM((ROW_BLK * PACK_ROWS, LANES), U32),
            pltpu.VMEM((ROW_BLK * PACK_ROWS, LANES), U32),
            pltpu.VMEM((D_MODEL, D_EXPERT), BF16),
            pltpu.VMEM((D_MODEL, D_EXPERT), BF16),
            pltpu.VMEM((D_EXPERT, D_MODEL), BF16),
            pltpu.SemaphoreType.DMA((2, len(SEG_SIZES))),
        ],
    )
    return pl.pallas_call(
        functools.partial(_experts_kernel, n_tiles=n_tiles, n_blocks=n_blocks),
        grid_spec=grid_spec,
        out_shape=jax.ShapeDtypeStruct((n_blocks * ROW_BLK * PACK_ROWS, LANES), U32),
        compiler_params=pltpu.CompilerParams(
            dimension_semantics=("arbitrary",), vmem_limit_bytes=VMEM_LIMIT),
        name="experts",
    )(blk_e, nblk, base, jlo, jhi, nvalid, npiece, cum, end, src, stage, w1, w3, w2)


def _combine_kernel(cnt_ref, off_ref, dst_ref, npiece_ref, h_ref, route_ref, p_ref, gple_ref, wpg_ref, bpg_ref,
                    wpp_ref, ys_hbm, o_ref, ybuf_a, ybuf_b, sems, *, n_steps):
    i = pl.program_id(0)
    n_cls = len(SEG_SIZES)

    def segment_copies(step, e, live, buf, sem):
        g = step * N_EXPERTS + e
        off, dst = off_ref[g], dst_ref[g]
        _segment_pieces(jnp.where(live, cnt_ref[g], 0),
                        lambda cls, o: _piece_copy(ys_hbm, buf, sem, cls, dst + o, off + o).start())

    @pl.when(i == 0)
    def _():
        def body(e, carry):
            segment_copies(0, e, True, ybuf_a, sems.at[0])
            return carry

        lax.fori_loop(0, N_EXPERTS, body, 0)

    def step(cur, cur_sem, nxt, nxt_sem):
        _drain(ys_hbm, cur, cur_sem, [npiece_ref[i * n_cls + c] for c in range(n_cls)])
        nxt_step = jnp.minimum(i + 1, n_steps - 1)
        for e in range(N_EXPERTS):
            segment_copies(nxt_step, e, i + 1 < n_steps, nxt, nxt_sem)

        pp = _dot(p_ref[...].astype(BF16), wpp_ref[...])
        route = route_ref[...]
        place = lax.broadcasted_iota(jnp.int32, (TM, TILE_ROWS), 1).astype(F32)
        sel = [(place == route[:, 4 + kk:5 + kk]).astype(BF16) for kk in range(TOP_K)]
        cols = _unpack_rows(cur, TILE_ROWS)
        picked = [jnp.concatenate([_dot(sel[kk], blk) for blk in cols], axis=1) for kk in range(TOP_K)]
        moe = picked[0] * route[:, 2:3] + picked[1] * route[:, 3:4]
        h = h_ref[...] + moe
        gate = _sigmoid(_dot(_rms(h, gple_ref[...]).astype(BF16), wpg_ref[...]) + bpg_ref[...])
        o_ref[...] = h + gate * pp

    @pl.when(i % 2 == 0)
    def _():
        step(ybuf_a, sems.at[0], ybuf_b, sems.at[1])

    @pl.when(i % 2 == 1)
    def _():
        step(ybuf_b, sems.at[1], ybuf_a, sems.at[0])


def _combine(cnt, off, dst, npiece, h1, route, p2, gple, wpg, bpg, wpp, ys):
    t = h1.shape[0]
    n_steps = t // TM
    const = lambda i, *_: (0, 0)
    row = lambda i, *_: (i, 0)
    grid_spec = pltpu.PrefetchScalarGridSpec(
        num_scalar_prefetch=4,
        grid=(n_steps,),
        in_specs=[
            pl.BlockSpec((TM, D_MODEL), row),
            pl.BlockSpec((TM, LANES), row),
            pl.BlockSpec((TM, PLE_DIM), row),
            pl.BlockSpec((1, D_MODEL), const),
            pl.BlockSpec((D_MODEL, D_MODEL), const),
            pl.BlockSpec((1, D_MODEL), const),
            pl.BlockSpec((PLE_DIM, D_MODEL), const),
            pl.BlockSpec(memory_space=pl.ANY),
        ],
        out_specs=pl.BlockSpec((TM, D_MODEL), row),
        scratch_shapes=[
            pltpu.VMEM((TILE_ROWS * PACK_ROWS, LANES), U32),
            pltpu.VMEM((TILE_ROWS * PACK_ROWS, LANES), U32),
            pltpu.SemaphoreType.DMA((2, len(SEG_SIZES))),
        ],
    )
    return pl.pallas_call(
        functools.partial(_combine_kernel, n_steps=n_steps),
        grid_spec=grid_spec,
        out_shape=jax.ShapeDtypeStruct((t, D_MODEL), F32),
        compiler_params=pltpu.CompilerParams(
            dimension_semantics=("arbitrary",), vmem_limit_bytes=VMEM_LIMIT),
        name="combine",
    )(cnt, off, dst, npiece, h1, route, p2, gple, wpg, bpg, wpp, ys)


def _layer(h, p_i, g_mix, w_in, b_in, g_q, g_k, rel_bias, conv_w, conv_b, w_pa, w_pc, w_o,
           g_ffn, w_group, b_group, w_router, b_router, w1, w3, w2,
           g_ple, w_ple_gate, b_ple_gate, w_ple_proj):
    b, s, d = h.shape
    t = b * s
    x2 = h.reshape(t, d)
    row2 = lambda a: a.reshape(1, -1).astype(F32)

    qkv_w = 3 * ATTN_W
    conv_end = qkv_w + 3 * CONV_W
    w_in_b = w_in.astype(BF16)
    gq = row2(jnp.tile(g_q.astype(F32) * (HEAD_DIM ** -0.5 * LOG2E), N_HEADS))
    gk = row2(jnp.tile(g_k.astype(F32), N_HEADS))
    head = jnp.arange(ATTN_W) // HEAD_DIM
    hmat = jnp.where(head[:, None] == head[None, :], 1.0 / HEAD_DIM, 0.0).astype(BF16)
    cw = jnp.concatenate([conv_w.astype(F32), jnp.zeros((SUBLANES - CONV_K, CONV_W), F32)], axis=0)

    q, k, v, yc = _inproj(x2, row2(g_mix), w_in_b[:, :qkv_w], w_in_b[:, qkv_w:conv_end],
                          row2(b_in[:conv_end]), gq, gk, hmat, cw, row2(conv_b), s)

    ya = _attention(q.reshape(b, s, ATTN_W), k.reshape(b, s, ATTN_W), v.reshape(b, s, ATTN_W),
                    _attn_bias(rel_bias)).reshape(t, ATTN_W)

    n_pad = LANES - N_GROUPS - N_EXPERTS
    wrt = jnp.concatenate([w_group, w_router, jnp.zeros((d, n_pad), w_group.dtype)], axis=1).astype(BF16)
    brt = row2(jnp.concatenate([b_group, b_router, jnp.zeros((n_pad,), b_group.dtype)]))
    h1, stage, route, cnt_f = _merge(x2, ya, yc, row2(g_mix), w_in_b[:, conv_end:], row2(b_in[conv_end:]),
                                     w_pa.astype(BF16), w_pc.astype(BF16), w_o.astype(BF16),
                                     row2(g_ffn), wrt, brt)

    n_tiles = t // TM
    cnt = cnt_f[:, 0, ROUTE_OFF:ROUTE_OFF + N_EXPERTS].astype(jnp.int32)
    tile_off = jnp.cumsum(cnt, axis=1) - cnt
    tot = cnt.sum(axis=0)
    pcounts = (tot + ROW_BLK - 1) // ROW_BLK * ROW_BLK
    pends = jnp.cumsum(pcounts)
    pstarts = pends - pcounts
    cum = jnp.cumsum(cnt, axis=0) - cnt
    dst = pstarts[None, :] + cum
    n_blocks = (t * TOP_K) // ROW_BLK + N_EXPERTS
    blk_start = jnp.arange(n_blocks, dtype=jnp.int32) * ROW_BLK
    blk_e = jnp.minimum((pends[None, :] <= blk_start[:, None]).sum(axis=1), N_EXPERTS - 1).astype(jnp.int32)
    nblk = (pends[-1:] // ROW_BLK).astype(jnp.int32)
    sel = (jnp.arange(N_EXPERTS, dtype=jnp.int32)[:, None] == blk_e[None, :]).astype(jnp.int32)
    of_block = lambda a: (a[..., None] * sel).sum(axis=-2)
    base = blk_start - of_block(pstarts)
    nvalid = jnp.clip(of_block(tot) - base, 0, ROW_BLK)
    cum_e = of_block(cum)
    cnt_e = of_block(cnt)
    jlo = (cum_e + cnt_e <= base[None, :]).sum(axis=0)
    jhi = (cum_e < base[None, :] + ROW_BLK).sum(axis=0)
    part = jnp.clip(jnp.minimum(cum_e + cnt_e, base[None, :] + ROW_BLK) - jnp.maximum(cum_e, base[None, :]),
                    0, ROW_BLK)
    blk_pieces = jnp.stack(_piece_counts(part), axis=-1).sum(axis=0)
    tile_pieces = jnp.stack(_piece_counts(cnt), axis=-1).sum(axis=1)
    src = jnp.arange(n_tiles, dtype=jnp.int32)[:, None] * TILE_ROWS + tile_off - cum
    flat = lambda a: a.reshape(-1).astype(jnp.int32)

    ys = _experts(blk_e, nblk, flat(base), flat(jlo), flat(jhi), flat(nvalid), flat(blk_pieces), flat(cum),
                  flat(cum + cnt), flat(src), stage, w1, w3, w2)
    out = _combine(flat(cnt), flat(tile_off), flat(dst), flat(tile_pieces), h1, route, p_i.reshape(t, PLE_DIM),
                   row2(g_ple), w_ple_gate.astype(BF16), row2(b_ple_gate), w_ple_proj.astype(BF16), ys)
    return out.reshape(b, s, d)


def kernel(x, p, g_mix, w_in, b_in, g_q, g_k, rel_bias, conv_w, conv_b, w_pa, w_pc, w_o, g_ffn, w_group, b_group, w_router, b_router, w1, w3, w2, g_ple, w_ple_gate, b_ple_gate, w_ple_proj):
    h = x
    for i in range(p.shape[0]):
        h = _layer(h, p[i], g_mix[i], w_in[i], b_in[i], g_q[i], g_k[i], rel_bias[i], conv_w[i], conv_b[i],
                   w_pa[i], w_pc[i], w_o[i], g_ffn[i], w_group[i], b_group[i], w_router[i], b_router[i],
                   w1[i], w3[i], w2[i], g_ple[i], w_ple_gate[i], b_ple_gate[i], w_ple_proj[i])
    return h
```

```python
import functools

import jax
import jax.numpy as jnp
from jax import lax
from jax.experimental import pallas as pl
from jax.experimental.pallas import tpu as pltpu

D_MODEL = 1024
CHUNK = 64
LEFT_CHUNKS = 8
N_HEADS = 8
HEAD_DIM = 64
ATTN_W = N_HEADS * HEAD_DIM
CONV_W = D_MODEL // 2
CONV_K = 3
MAX_REL_PAST = 256
PLE_DIM = 256
N_GROUPS = 4
EXPERTS_PER_GROUP = 8
N_EXPERTS = N_GROUPS * EXPERTS_PER_GROUP
TOP_K = 2
D_EXPERT = 512
EPS = 1e-6
NEG = -1e30
LOG2E = 1.4426950408889634

LANES = 128
SUBLANES = 8
TM = 256
TQ = 256
KV_SLABS = 1 + (LEFT_CHUNKS * CHUNK) // TQ
ROW_BLK = 512
TILE_ROWS = TOP_K * TM
ROUTE_OFF = N_GROUPS
PACK_ROWS = D_MODEL // (2 * LANES)
PACK_W = 2 * LANES
SEG_SIZES = tuple(TM >> k for k in range(TM.bit_length()))
BIG_PIECE = 64
SCORE_AHEAD = 2
MERGE_PARTS = 1
GATHER_UNROLL = 36
VMEM_LIMIT = 56 * 1024 * 1024

F32 = jnp.float32
BF16 = jnp.bfloat16
U32 = jnp.uint32


def _dot(a, b):
    return jnp.dot(a, b, preferred_element_type=F32)


def _rms(x, g):
    ms = jnp.mean(x * x, axis=-1, keepdims=True)
    return (x * lax.rsqrt(ms + EPS)) * g


def _sigmoid(x):
    return 1.0 / (1.0 + jnp.exp(-x))


def _pack_block(ref, vals, n_rows, a):
    lo = vals[:, 0:LANES].astype(BF16).astype(F32)
    hi = vals[:, LANES:PACK_W].astype(BF16).astype(F32)
    word = (lax.bitcast_convert_type(hi, U32) & U32(0xFFFF0000)) | (lax.bitcast_convert_type(lo, U32) >> 16)
    ref[pl.ds(a, n_rows, stride=PACK_ROWS), :] = word


def _pack_rows(ref, vals, n_rows):
    for a in range(PACK_ROWS):
        _pack_block(ref, vals[:, a * PACK_W:(a + 1) * PACK_W], n_rows, a)


def _unpack_block(ref, n_rows, a, n_valid=None):
    word = ref[pl.ds(a, n_rows, stride=PACK_ROWS), :]
    if n_valid is not None:
        word = jnp.where(lax.broadcasted_iota(jnp.int32, (n_rows, LANES), 0) < n_valid, word, U32(0))
    lo = lax.bitcast_convert_type(word << 16, F32).astype(BF16)
    hi = lax.bitcast_convert_type(word & U32(0xFFFF0000), F32).astype(BF16)
    return jnp.concatenate([lo, hi], axis=1)


def _split(items, n_groups):
    items = list(items)
    return [items[len(items) * g // n_groups:len(items) * (g + 1) // n_groups] for g in range(n_groups)]


def _inproj_kernel(x_ref, g_ref, wqkv_ref, wconv_ref, b_ref, gq_ref, gk_ref, hm_ref,
                   cw_ref, cb_ref, q_ref, k_ref, v_ref, yc_ref, carry_ref, *, tiles_per_seq):
    i = pl.program_id(0)
    nb = _rms(x_ref[...], g_ref[...]).astype(BF16)

    zq = _dot(nb, wqkv_ref[...]) + b_ref[:, 0:3 * ATTN_W]
    hm = hm_ref[...]

    def head_rms(t, g):
        ms = _dot((t * t).astype(BF16), hm)
        return (t * lax.rsqrt(ms + EPS)) * g

    q_ref[...] = head_rms(zq[:, 0:ATTN_W], gq_ref[...]).astype(BF16)
    k_ref[...] = head_rms(zq[:, ATTN_W:2 * ATTN_W], gk_ref[...]).astype(BF16)
    v_ref[...] = zq[:, 2 * ATTN_W:3 * ATTN_W].astype(BF16)

    zc = _dot(nb, wconv_ref[...]) + b_ref[:, 3 * ATTN_W:3 * ATTN_W + 3 * CONV_W]
    u = zc[:, 0:CONV_W]
    bg = zc[:, CONV_W:2 * CONV_W]
    cg = zc[:, 2 * CONV_W:3 * CONV_W]
    cu = cg * u

    @pl.when((i % tiles_per_seq) == 0)
    def _():
        carry_ref[...] = jnp.zeros_like(carry_ref)

    prev = carry_ref[...]
    carry_ref[...] = cu[TM - SUBLANES:TM, :]
    row = lax.broadcasted_iota(jnp.int32, (SUBLANES, CONV_W), 0)

    def shifted(s):
        r = pltpu.roll(cu, s, 0)
        p = pltpu.roll(prev, s, 0)
        top = jnp.where(row < s, p, r[0:SUBLANES, :])
        return jnp.concatenate([top, r[SUBLANES:, :]], axis=0)

    y = cb_ref[...] + cw_ref[0:1, :] * shifted(2)
    y = y + cw_ref[1:2, :] * shifted(1)
    y = y + cw_ref[2:3, :] * cu
    yc_ref[...] = (bg * y).astype(BF16)


def _inproj(x2, g_mix, wqkv, wconv, b_in, gq, gk, hmat, cw, cb, seq):
    t = x2.shape[0]
    const = lambda i: (0, 0)
    row = lambda i: (i, 0)
    out = jax.ShapeDtypeStruct((t, ATTN_W), BF16)
    return pl.pallas_call(
        functools.partial(_inproj_kernel, tiles_per_seq=seq // TM),
        grid=(t // TM,),
        in_specs=[
            pl.BlockSpec((TM, D_MODEL), row),
            pl.BlockSpec((1, D_MODEL), const),
            pl.BlockSpec((D_MODEL, 3 * ATTN_W), const),
            pl.BlockSpec((D_MODEL, 3 * CONV_W), const),
            pl.BlockSpec((1, 3 * ATTN_W + 3 * CONV_W), const),
            pl.BlockSpec((1, ATTN_W), const),
            pl.BlockSpec((1, ATTN_W), const),
            pl.BlockSpec((ATTN_W, ATTN_W), const),
            pl.BlockSpec((SUBLANES, CONV_W), const),
            pl.BlockSpec((1, CONV_W), const),
        ],
        out_specs=[pl.BlockSpec((TM, ATTN_W), row)] * 4,
        out_shape=[out] * 4,
        scratch_shapes=[pltpu.VMEM((SUBLANES, CONV_W), F32)],
        compiler_params=pltpu.CompilerParams(
            dimension_semantics=("arbitrary",), vmem_limit_bytes=VMEM_LIMIT),
        name="inproj",
    )(x2, g_mix, wqkv, wconv, b_in, gq, gk, hmat, cw, cb)


def _lane_fold(parts, op):
    acc = None
    for a in parts:
        for c in range(0, a.shape[1], LANES):
            piece = a[:, c:c + LANES]
            acc = piece if acc is None else op(acc, piece)
    return acc


def _attn_kernel(q_ref, k0_ref, k1_ref, k2_ref, v0_ref, v1_ref, v2_ref, bias_ref, o_ref):
    k_refs = (k0_ref, k1_ref, k2_ref)
    v_refs = (v0_ref, v1_ref, v2_ref)
    pair_w = 2 * HEAD_DIM
    lane = lax.broadcasted_iota(jnp.int32, (TQ, pair_w), 1)
    low = lane < HEAD_DIM

    def scores(h):
        ps = slice((h // 2) * pair_w, (h // 2 + 1) * pair_w)
        q_pair = q_ref[0, :, ps]
        own = low if h % 2 == 0 else jnp.logical_not(low)
        qh = jnp.where(own, q_pair, jnp.zeros_like(q_pair))
        return [lax.dot_general(qh, k_refs[j][0, :, ps], (((1,), (1,)), ((), ())),
                                preferred_element_type=F32) + bias_ref[0, h, :, j * TQ:(j + 1) * TQ]
                for j in range(KV_SLABS)]

    def weighted(h, s):
        ps = slice((h // 2) * pair_w, (h // 2 + 1) * pair_w)
        m = _lane_fold(s, jnp.maximum).max(axis=-1, keepdims=True)
        e = [jnp.exp2(sj - m) for sj in s]
        l = _lane_fold(e, jnp.add).sum(axis=-1, keepdims=True)
        acc = None
        for j in range(KV_SLABS):
            oj = _dot(e[j].astype(BF16), v_refs[j][0, :, ps])
            acc = oj if acc is None else acc + oj
        return acc * (1.0 / l)

    pending = [scores(h) for h in range(SCORE_AHEAD)]
    o_even = None
    for h in range(N_HEADS):
        if h + SCORE_AHEAD < N_HEADS:
            pending.append(scores(h + SCORE_AHEAD))
        o = weighted(h, pending.pop(0))
        if h % 2 == 0:
            o_even = o
        else:
            ps = slice((h // 2) * pair_w, (h // 2 + 1) * pair_w)
            o_ref[0, :, ps] = jnp.where(low, o_even, o).astype(BF16)


def _attention(q, k, v, bias):
    b, s, _ = q.shape
    blk = (1, TQ, ATTN_W)

    def kv_map(j):
        back = KV_SLABS - 1 - j
        return lambda bi, i: (bi, jnp.maximum(i - back, 0), 0)

    kv_specs = [pl.BlockSpec(blk, kv_map(j)) for j in range(KV_SLABS)]
    n_var = bias.shape[0]
    return pl.pallas_call(
        _attn_kernel,
        grid=(b, s // TQ),
        in_specs=[pl.BlockSpec(blk, lambda bi, i: (bi, i, 0))] + kv_specs + kv_specs + [
            pl.BlockSpec((1, N_HEADS, TQ, KV_SLABS * TQ),
                         lambda bi, i: (jnp.minimum(i, n_var - 1), 0, 0, 0))],
        out_specs=pl.BlockSpec(blk, lambda bi, i: (bi, i, 0)),
        out_shape=jax.ShapeDtypeStruct((b, s, ATTN_W), BF16),
        compiler_params=pltpu.CompilerParams(
            dimension_semantics=("arbitrary", "arbitrary"), vmem_limit_bytes=VMEM_LIMIT),
        name="attn",
    )(q, k, k, k, v, v, v, bias)


def _attn_bias(rel_bias):
    nk = KV_SLABS * TQ
    past = nk - TQ
    d = jnp.arange(TQ - 1 + past, -TQ, -1)
    idx = jnp.clip(d, -(CHUNK - 1), MAX_REL_PAST) + (CHUNK - 1)
    onehot = (idx[:, None] == jnp.arange(rel_bias.shape[1])[None, :]).astype(F32)
    per_dist = jnp.einsum("dn,hn->hd", onehot, rel_bias.astype(F32) * LOG2E,
                          precision=lax.Precision.HIGHEST)
    n_h, span = per_dist.shape
    padded = jnp.pad(per_dist, ((0, 0), (0, 2)))
    skew = jnp.tile(padded, (1, TQ))[:, :TQ * (span + 1)].reshape(n_h, TQ, span + 1)
    table = skew[:, :, TQ - 1:TQ - 1 + nk]
    r = jnp.arange(TQ)[:, None]
    c = jnp.arange(nk)[None, :]
    qc = r // CHUNK
    kc = c // CHUNK
    lead = past // CHUNK - LEFT_CHUNKS
    band = (kc >= qc + lead) & (kc <= qc + lead + LEFT_CHUNKS)
    variants = []
    for var in range(KV_SLABS):
        valid = band & (c >= (KV_SLABS - 1 - var) * TQ)
        variants.append(jnp.where(valid[None], table, NEG))
    return jnp.stack(variants, axis=0)


def _merge_kernel(x_ref, ya_ref, yc_ref, g_ref, wg_ref, bgate_ref, wpa_ref, wpc_ref, wo_ref,
                  gffn_ref, wrt_ref, brt_ref, h_ref, stage_ref, route_ref, cnt_ref, n2_scr, logit_scr):
    @pl.when(pl.program_id(0) == 0)
    def _():
        n2_scr[...] = jnp.zeros_like(n2_scr)
        logit_scr[...] = jnp.zeros_like(logit_scr)

    n2 = n2_scr[...]
    logits = logit_scr[...]

    for part in range(MERGE_PARTS):
        rows = slice(part * (TM // MERGE_PARTS), (part + 1) * (TM // MERGE_PARTS))
        x = x_ref[rows, :]
        nb = _rms(x, g_ref[...]).astype(BF16)
        gates = _dot(nb, wg_ref[...]) + bgate_ref[...]
        sga = _sigmoid(gates[:, 0:D_MODEL])
        sgc = _sigmoid(gates[:, D_MODEL:2 * D_MODEL])
        m = sga * _dot(ya_ref[rows, :], wpa_ref[...]) + sgc * _dot(yc_ref[rows, :], wpc_ref[...])
        h = x + _dot(m.astype(BF16), wo_ref[...])
        h_ref[rows, :] = h
        n2_new = _rms(h, gffn_ref[...]).astype(BF16)
        n2_scr[rows, :] = n2_new
        logit_scr[rows, :] = _dot(n2_new, wrt_ref[...]) + brt_ref[...]

    lane = lax.broadcasted_iota(jnp.int32, (TM, LANES), 1).astype(F32)
    ninf = -jnp.inf

    def argmax_first(vals):
        mx = vals.max(axis=-1, keepdims=True)
        idx = jnp.where(vals == mx, lane, float(LANES)).min(axis=-1, keepdims=True)
        return mx, idx

    gmask = lane < N_GROUPS
    gmax, grp = argmax_first(jnp.where(gmask, logits, ninf))
    gsum = jnp.where(gmask, jnp.exp(logits - gmax), 0.0).sum(axis=-1, keepdims=True)
    p_grp = 1.0 / gsum
    first = ROUTE_OFF + EXPERTS_PER_GROUP * grp
    el = jnp.where((lane >= first) & (lane < first + EXPERTS_PER_GROUP), logits, ninf)
    l1, i1 = argmax_first(el)
    l2, i2 = argmax_first(jnp.where(lane == i1, ninf, el))
    e2 = jnp.exp(l2 - l1)
    den = 1.0 + e2
    w1 = p_grp * (1.0 / den)
    w2 = p_grp * (e2 / den)

    oh1 = (lane == i1).astype(F32)
    oh2 = (lane == i2).astype(F32)
    oh = (oh1 + oh2).astype(BF16)
    r = lax.broadcasted_iota(jnp.int32, (TM, TM), 0)
    c = lax.broadcasted_iota(jnp.int32, (TM, TM), 1)
    earlier_tok = _dot((c < r).astype(BF16), oh)
    er = lax.broadcasted_iota(jnp.int32, (LANES, LANES), 0)
    ec = lax.broadcasted_iota(jnp.int32, (LANES, LANES), 1)
    lower_exp = _dot(oh, (er < ec).astype(BF16)).sum(axis=0, keepdims=True)
    where = earlier_tok + lower_exp
    pos1 = (oh1 * where).sum(axis=-1, keepdims=True)
    pos2 = (oh2 * where).sum(axis=-1, keepdims=True)
    cnt_ref[0] = jnp.broadcast_to(oh.astype(F32).sum(axis=0, keepdims=True), (SUBLANES, LANES))

    cols = (i1 - ROUTE_OFF, i2 - ROUTE_OFF, w1, w2, pos1, pos2)
    route = jnp.zeros((TM, LANES), F32)
    for j, col in enumerate(cols):
        route = jnp.where(lane == j, col, route)
    route_ref[...] = route

    route_t = route.T
    slot = lax.broadcasted_iota(jnp.int32, (TILE_ROWS, TM), 0).astype(F32)
    place = ((slot == route_t[4:5, :]) | (slot == route_t[5:6, :])).astype(BF16)
    _pack_rows(stage_ref, _dot(place, n2), TILE_ROWS)


def _merge(x2, ya, yc, g_mix, wgate, bgate, wpa, wpc, wo, gffn, wrt, brt):
    t = x2.shape[0]
    n_tiles = t // TM
    const = lambda i: (0, 0)
    row = lambda i: (jnp.minimum(i, n_tiles - 1), 0)
    late = lambda i: (jnp.maximum(i - 1, 0), 0)
    return pl.pallas_call(
        _merge_kernel,
        grid=(n_tiles + 1,),
        in_specs=[
            pl.BlockSpec((TM, D_MODEL), row),
            pl.BlockSpec((TM, ATTN_W), row),
            pl.BlockSpec((TM, CONV_W), row),
            pl.BlockSpec((1, D_MODEL), const),
            pl.BlockSpec((D_MODEL, 2 * D_MODEL), const),
            pl.BlockSpec((1, 2 * D_MODEL), const),
            pl.BlockSpec((ATTN_W, D_MODEL), const),
            pl.BlockSpec((CONV_W, D_MODEL), const),
            pl.BlockSpec((D_MODEL, D_MODEL), const),
            pl.BlockSpec((1, D_MODEL), const),
            pl.BlockSpec((D_MODEL, LANES), const),
            pl.BlockSpec((1, LANES), const),
        ],
        out_specs=[
            pl.BlockSpec((TM, D_MODEL), row),
            pl.BlockSpec((TILE_ROWS * PACK_ROWS, LANES), late),
            pl.BlockSpec((TM, LANES), late),
            pl.BlockSpec((1, SUBLANES, LANES), lambda i: (jnp.maximum(i - 1, 0), 0, 0)),
        ],
        out_shape=[
            jax.ShapeDtypeStruct((t, D_MODEL), F32),
            jax.ShapeDtypeStruct((n_tiles * TILE_ROWS * PACK_ROWS, LANES), U32),
            jax.ShapeDtypeStruct((t, LANES), F32),
            jax.ShapeDtypeStruct((n_tiles, SUBLANES, LANES), F32),
        ],
        scratch_shapes=[pltpu.VMEM((TM, D_MODEL), BF16), pltpu.VMEM((TM, LANES), F32)],
        compiler_params=pltpu.CompilerParams(
            dimension_semantics=("arbitrary",), vmem_limit_bytes=VMEM_LIMIT),
        name="merge",
    )(x2, ya, yc, g_mix, wgate, bgate, wpa, wpc, wo, gffn, wrt, brt)


def _piece_counts(n):
    return [(n >> (size.bit_length() - 1)) & 1 for size in SEG_SIZES]


ALL_CLASSES = tuple(range(len(SEG_SIZES)))
BIG_CLASSES = tuple(c for c in ALL_CLASSES if SEG_SIZES[c] >= BIG_PIECE)
SMALL_CLASSES = tuple(c for c in ALL_CLASSES if SEG_SIZES[c] < BIG_PIECE)


def _segment_pieces(n, visit, classes=ALL_CLASSES):
    for cls in classes:
        size = SEG_SIZES[cls]

        @pl.when((n & size) != 0)
        def _(cls=cls, size=size):
            visit(cls, n & ~(2 * size - 1))


def _piece_copy(src_ref, dst_ref, sems, cls, src_row, dst_row):
    n = SEG_SIZES[cls] * PACK_ROWS
    return pltpu.make_async_copy(src_ref.at[pl.ds(src_row * PACK_ROWS, n), :],
                                 dst_ref.at[pl.ds(dst_row * PACK_ROWS, n), :], sems.at[cls])


def _drain(src_ref, dst_ref, sems, counts):
    unroll = 4
    for cls in range(len(SEG_SIZES)):
        def wait_some(k, cls=cls):
            def body(t, carry):
                for _ in range(k):
                    _piece_copy(src_ref, dst_ref, sems, cls, 0, 0).wait()
                return carry
            return body

        n = counts[cls]
        lax.fori_loop(0, n >> 2, wait_some(unroll), 0)
        lax.fori_loop(0, n & (unroll - 1), wait_some(1), 0)


def _experts_kernel(be_ref, nb_ref, base_ref, jlo_ref, jhi_ref, nv_ref, big_ref, npiece_ref, cum_ref, end_ref,
                    src_ref, stage_hbm, w1_ref, w3_ref, w2_ref, ys_ref,
                    xbuf_a, xbuf_b, w1b_ref, w3b_ref, w2b_ref, sems, *, n_tiles, n_blocks):
    s = pl.program_id(0)
    nb = nb_ref[0]
    n_cls = len(SEG_SIZES)

    def segment_copies(step, j, live, buf, sem, classes=ALL_CLASSES):
        base = base_ref[step]
        g = jnp.minimum(j, n_tiles - 1) * N_EXPERTS + be_ref[step]
        lo = jnp.maximum(cum_ref[g], base)
        hi = jnp.minimum(end_ref[g], base + ROW_BLK)
        src = src_ref[g] + lo
        dst = lo - base
        _segment_pieces(jnp.where(live, jnp.maximum(hi - lo, 0), 0),
                        lambda cls, o: _piece_copy(stage_hbm, buf, sem, cls, src + o, dst + o).start(), classes)

    def looped_copies(step, j0, j1, buf, sem, classes=ALL_CLASSES):
        def body(j, carry):
            segment_copies(step, j, True, buf, sem, classes)
            return carry

        lax.fori_loop(j0, j1, body, 0)

    @pl.when(s == 0)
    def _():
        xbuf_a[...] = jnp.zeros_like(xbuf_a)
        xbuf_b[...] = jnp.zeros_like(xbuf_b)
        looped_copies(0, jlo_ref[0], jhi_ref[0], xbuf_a, sems.at[0])

    def step(cur, cur_sem, nxt, nxt_sem):
        prev = be_ref[jnp.maximum(s - 1, 0)]

        @pl.when((s == 0) | (be_ref[s] != prev))
        def _():
            w1b_ref[...] = w1_ref[0].astype(BF16)
            w3b_ref[...] = w3_ref[0].astype(BF16)
            w2b_ref[...] = w2_ref[0].astype(BF16)

        _drain(stage_hbm, cur, cur_sem, [npiece_ref[s * n_cls + c] for c in range(n_cls)])

        nxt_step = jnp.minimum(s + 1, n_blocks - 1)
        live = s + 1 < nb
        j0 = jlo_ref[nxt_step]
        j1 = jnp.where(live, jhi_ref[nxt_step], j0)
        looped_copies(nxt_step, j0 + GATHER_UNROLL, j1, nxt, nxt_sem, SMALL_CLASSES)
        looped_copies(nxt_step, j0, jnp.where(big_ref[nxt_step] != 0, j1, j0), nxt, nxt_sem, BIG_CLASSES)
        groups = iter(_split(range(GATHER_UNROLL), 2 * PACK_ROWS))

        def start_group():
            for k in next(groups):
                segment_copies(nxt_step, j0 + k, j0 + k < j1, nxt, nxt_sem, SMALL_CLASSES)

        a = None
        g = None
        for blk in range(PACK_ROWS):
            start_group()
            xa = _unpack_block(cur, ROW_BLK, blk, n_valid=nv_ref[s])
            rows = slice(blk * PACK_W, (blk + 1) * PACK_W)
            da = _dot(xa, w1b_ref[rows, :])
            dg = _dot(xa, w3b_ref[rows, :])
            a = da if a is None else a + da
            g = dg if g is None else g + dg
        hdn = ((a * _sigmoid(a)) * g).astype(BF16)
        for blk in range(PACK_ROWS):
            start_group()
            _pack_block(ys_ref, _dot(hdn, w2b_ref[:, blk * PACK_W:(blk + 1) * PACK_W]), ROW_BLK, blk)

    @pl.when((s < nb) & (s % 2 == 0))
    def _():
        step(xbuf_a, sems.at[0], xbuf_b, sems.at[1])

    @pl.when((s < nb) & (s % 2 == 1))
    def _():
        step(xbuf_b, sems.at[1], xbuf_a, sems.at[0])

    @pl.when(s >= nb)
    def _():
        ys_ref[...] = jnp.zeros_like(ys_ref)


def _experts(blk_e, nblk, base, jlo, jhi, nvalid, big, npiece, cum, end, src, stage, w1, w3, w2):
    n_blocks = blk_e.shape[0]
    n_tiles = cum.shape[0] // N_EXPERTS

    def wsel(s, be, nb, *_):
        return (be[jnp.minimum(s, nb[0] - 1)], 0, 0)

    grid_spec = pltpu.PrefetchScalarGridSpec(
        num_scalar_prefetch=11,
        grid=(n_blocks,),
        in_specs=[
            pl.BlockSpec(memory_space=pl.ANY),
            pl.BlockSpec((1, D_MODEL, D_EXPERT), wsel),
            pl.BlockSpec((1, D_MODEL, D_EXPERT), wsel),
            pl.BlockSpec((1, D_EXPERT, D_MODEL), wsel),
        ],
        out_specs=pl.BlockSpec((ROW_BLK * PACK_ROWS, LANES), lambda s, *_: (s, 0)),
        scratch_shapes=[
            pltpu.VMEM((ROW_BLK * PACK_ROWS, LANES), U32),
            pltpu.VMEM((ROW_BLK * PACK_ROWS, LANES), U32),
            pltpu.VMEM((D_MODEL, D_EXPERT), BF16),
            pltpu.VMEM((D_MODEL, D_EXPERT), BF16),
            pltpu.VMEM((D_EXPERT, D_MODEL), BF16),
            pltpu.SemaphoreType.DMA((2, len(SEG_SIZES))),
        ],
    )
    return pl.pallas_call(
        functools.partial(_experts_kernel, n_tiles=n_tiles, n_blocks=n_blocks),
        grid_spec=grid_spec,
        out_shape=jax.ShapeDtypeStruct((n_blocks * ROW_BLK * PACK_ROWS, LANES), U32),
        compiler_params=pltpu.CompilerParams(
            dimension_semantics=("arbitrary",), vmem_limit_bytes=VMEM_LIMIT),
        name="experts",
    )(blk_e, nblk, base, jlo, jhi, nvalid, big, npiece, cum, end, src, stage, w1, w3, w2)


def _combine_kernel(cnt_ref, off_ref, dst_ref, big_ref, npiece_ref, h_ref, route_ref, p_ref, gple_ref, wpg_ref,
                    bpg_ref, wpp_ref, ys_hbm, o_ref, ybuf_a, ybuf_b, sems, *, n_steps):
    i = pl.program_id(0)
    n_cls = len(SEG_SIZES)

    def segment_copies(step, e, live, buf, sem, classes=ALL_CLASSES):
        g = step * N_EXPERTS + e
        off, dst = off_ref[g], dst_ref[g]
        _segment_pieces(jnp.where(live, cnt_ref[g], 0),
                        lambda cls, o: _piece_copy(ys_hbm, buf, sem, cls, dst + o, off + o).start(), classes)

    def looped_copies(step, n_experts, buf, sem, classes=ALL_CLASSES):
        def body(e, carry):
            segment_copies(step, e, True, buf, sem, classes)
            return carry

        lax.fori_loop(0, n_experts, body, 0)

    @pl.when(i == 0)
    def _():
        looped_copies(0, N_EXPERTS, ybuf_a, sems.at[0])

    def step(cur, cur_sem, nxt, nxt_sem):
        _drain(ys_hbm, cur, cur_sem, [npiece_ref[i * n_cls + c] for c in range(n_cls)])
        nxt_step = jnp.minimum(i + 1, n_steps - 1)
        live = i + 1 < n_steps
        looped_copies(nxt_step, jnp.where(live & (big_ref[nxt_step] != 0), N_EXPERTS, 0), nxt, nxt_sem, BIG_CLASSES)
        groups = iter(_split(range(N_EXPERTS), PACK_ROWS + 2))

        def start_group():
            for e in next(groups):
                segment_copies(nxt_step, e, live, nxt, nxt_sem, SMALL_CLASSES)

        start_group()
        pp = _dot(p_ref[...].astype(BF16), wpp_ref[...])
        route = route_ref[...]
        place = lax.broadcasted_iota(jnp.int32, (TM, TILE_ROWS), 1).astype(F32)
        sel = [(place == route[:, 4 + kk:5 + kk]).astype(BF16) for kk in range(TOP_K)]
        moe = []
        for blk in range(PACK_ROWS):
            start_group()
            cols = _unpack_block(cur, TILE_ROWS, blk)
            moe.append(_dot(sel[0], cols) * route[:, 2:3] + _dot(sel[1], cols) * route[:, 3:4])
        start_group()
        h = h_ref[...] + jnp.concatenate(moe, axis=1)
        gate = _sigmoid(_dot(_rms(h, gple_ref[...]).astype(BF16), wpg_ref[...]) + bpg_ref[...])
        o_ref[...] = h + gate * pp

    @pl.when(i % 2 == 0)
    def _():
        step(ybuf_a, sems.at[0], ybuf_b, sems.at[1])

    @pl.when(i % 2 == 1)
    def _():
        step(ybuf_b, sems.at[1], ybuf_a, sems.at[0])


def _combine(cnt, off, dst, big, npiece, h1, route, p2, gple, wpg, bpg, wpp, ys):
    t = h1.shape[0]
    n_steps = t // TM
    const = lambda i, *_: (0, 0)
    row = lambda i, *_: (i, 0)
    grid_spec = pltpu.PrefetchScalarGridSpec(
        num_scalar_prefetch=5,
        grid=(n_steps,),
        in_specs=[
            pl.BlockSpec((TM, D_MODEL), row),
            pl.BlockSpec((TM, LANES), row),
            pl.BlockSpec((TM, PLE_DIM), row),
            pl.BlockSpec((1, D_MODEL), const),
            pl.BlockSpec((D_MODEL, D_MODEL), const),
            pl.BlockSpec((1, D_MODEL), const),
            pl.BlockSpec((PLE_DIM, D_MODEL), const),
            pl.BlockSpec(memory_space=pl.ANY),
        ],
        out_specs=pl.BlockSpec((TM, D_MODEL), row),
        scratch_shapes=[
            pltpu.VMEM((TILE_ROWS * PACK_ROWS, LANES), U32),
            pltpu.VMEM((TILE_ROWS * PACK_ROWS, LANES), U32),
            pltpu.SemaphoreType.DMA((2, len(SEG_SIZES))),
        ],
    )
    return pl.pallas_call(
        functools.partial(_combine_kernel, n_steps=n_steps),
        grid_spec=grid_spec,
        out_shape=jax.ShapeDtypeStruct((t, D_MODEL), F32),
        compiler_params=pltpu.CompilerParams(
            dimension_semantics=("arbitrary",), vmem_limit_bytes=VMEM_LIMIT),
        name="combine",
    )(cnt, off, dst, big, npiece, h1, route, p2, gple, wpg, bpg, wpp, ys)


def _layer(h, p_i, g_mix, w_in, b_in, g_q, g_k, rel_bias, conv_w, conv_b, w_pa, w_pc, w_o,
           g_ffn, w_group, b_group, w_router, b_router, w1, w3, w2,
           g_ple, w_ple_gate, b_ple_gate, w_ple_proj):
    b, s, d = h.shape
    t = b * s
    x2 = h.reshape(t, d)
    row2 = lambda a: a.reshape(1, -1).astype(F32)

    qkv_w = 3 * ATTN_W
    conv_end = qkv_w + 3 * CONV_W
    w_in_b = w_in.astype(BF16)
    gq = row2(jnp.tile(g_q.astype(F32) * (HEAD_DIM ** -0.5 * LOG2E), N_HEADS))
    gk = row2(jnp.tile(g_k.astype(F32), N_HEADS))
    head = jnp.arange(ATTN_W) // HEAD_DIM
    hmat = jnp.where(head[:, None] == head[None, :], 1.0 / HEAD_DIM, 0.0).astype(BF16)
    cw = jnp.concatenate([conv_w.astype(F32), jnp.zeros((SUBLANES - CONV_K, CONV_W), F32)], axis=0)

    q, k, v, yc = _inproj(x2, row2(g_mix), w_in_b[:, :qkv_w], w_in_b[:, qkv_w:conv_end],
                          row2(b_in[:conv_end]), gq, gk, hmat, cw, row2(conv_b), s)

    ya = _attention(q.reshape(b, s, ATTN_W), k.reshape(b, s, ATTN_W), v.reshape(b, s, ATTN_W),
                    _attn_bias(rel_bias)).reshape(t, ATTN_W)

    n_pad = LANES - N_GROUPS - N_EXPERTS
    wrt = jnp.concatenate([w_group, w_router, jnp.zeros((d, n_pad), w_group.dtype)], axis=1).astype(BF16)
    brt = row2(jnp.concatenate([b_group, b_router, jnp.zeros((n_pad,), b_group.dtype)]))
    h1, stage, route, cnt_f = _merge(x2, ya, yc, row2(g_mix), w_in_b[:, conv_end:], row2(b_in[conv_end:]),
                                     w_pa.astype(BF16), w_pc.astype(BF16), w_o.astype(BF16),
                                     row2(g_ffn), wrt, brt)

    n_tiles = t // TM
    cnt = cnt_f[:, 0, ROUTE_OFF:ROUTE_OFF + N_EXPERTS].astype(jnp.int32)
    tile_off = jnp.cumsum(cnt, axis=1) - cnt
    tot = cnt.sum(axis=0)
    pcounts = (tot + ROW_BLK - 1) // ROW_BLK * ROW_BLK
    pends = jnp.cumsum(pcounts)
    pstarts = pends - pcounts
    cum = jnp.cumsum(cnt, axis=0) - cnt
    dst = pstarts[None, :] + cum
    n_blocks = (t * TOP_K) // ROW_BLK + N_EXPERTS
    blk_start = jnp.arange(n_blocks, dtype=jnp.int32) * ROW_BLK
    blk_e = jnp.minimum((pends[None, :] <= blk_start[:, None]).sum(axis=1), N_EXPERTS - 1).astype(jnp.int32)
    nblk = (pends[-1:] // ROW_BLK).astype(jnp.int32)
    sel = (jnp.arange(N_EXPERTS, dtype=jnp.int32)[:, None] == blk_e[None, :]).astype(jnp.int32)
    of_block = lambda a: (a[..., None] * sel).sum(axis=-2)
    base = blk_start - of_block(pstarts)
    nvalid = jnp.clip(of_block(tot) - base, 0, ROW_BLK)
    cum_e = of_block(cum)
    cnt_e = of_block(cnt)
    jlo = (cum_e + cnt_e <= base[None, :]).sum(axis=0)
    jhi = (cum_e < base[None, :] + ROW_BLK).sum(axis=0)
    part = jnp.clip(jnp.minimum(cum_e + cnt_e, base[None, :] + ROW_BLK) - jnp.maximum(cum_e, base[None, :]),
                    0, ROW_BLK)
    blk_pieces = jnp.stack(_piece_counts(part), axis=-1).sum(axis=0)
    tile_pieces = jnp.stack(_piece_counts(cnt), axis=-1).sum(axis=1)
    src = jnp.arange(n_tiles, dtype=jnp.int32)[:, None] * TILE_ROWS + tile_off - cum
    flat = lambda a: a.reshape(-1).astype(jnp.int32)

    blk_big = (part >= BIG_PIECE).any(axis=0)
    tile_big = (cnt >= BIG_PIECE).any(axis=1)

    ys = _experts(blk_e, nblk, flat(base), flat(jlo), flat(jhi), flat(nvalid), flat(blk_big), flat(blk_pieces),
                  flat(cum), flat(cum + cnt), flat(src), stage, w1, w3, w2)
    out = _combine(flat(cnt), flat(tile_off), flat(dst), flat(tile_big), flat(tile_pieces), h1, route,
                   p_i.reshape(t, PLE_DIM), row2(g_ple), w_ple_gate.astype(BF16), row2(b_ple_gate),
                   w_ple_proj.astype(BF16), ys)
    return out.reshape(b, s, d)


def kernel(x, p, g_mix, w_in, b_in, g_q, g_k, rel_bias, conv_w, conv_b, w_pa, w_pc, w_o, g_ffn, w_group, b_group, w_router, b_router, w1, w3, w2, g_ple, w_ple_gate, b_ple_gate, w_ple_proj):
    h = x
    for i in range(p.shape[0]):
        h = _layer(h, p[i], g_mix[i], w_in[i], b_in[i], g_q[i], g_k[i], rel_bias[i], conv_w[i], conv_b[i],
                   w_pa[i], w_pc[i], w_o[i], g_ffn[i], w_group[i], b_group[i], w_router[i], b_router[i],
                   w1[i], w3[i], w2[i], g_ple[i], w_ple_gate[i], b_ple_gate[i], w_ple_proj[i])
    return h
```

```python
import functools

import jax
import jax.numpy as jnp
from jax import lax
from jax.experimental import pallas as pl
from jax.experimental.pallas import tpu as pltpu

D_MODEL = 1024
CHUNK = 64
LEFT_CHUNKS = 8
N_HEADS = 8
HEAD_DIM = 64
ATTN_W = N_HEADS * HEAD_DIM
CONV_W = D_MODEL // 2
CONV_K = 3
MAX_REL_PAST = 256
PLE_DIM = 256
N_GROUPS = 4
EXPERTS_PER_GROUP = 8
N_EXPERTS = N_GROUPS * EXPERTS_PER_GROUP
TOP_K = 2
D_EXPERT = 512
EPS = 1e-6
NEG = -1e30
LOG2E = 1.4426950408889634

LANES = 128
SUBLANES = 8
TM = 256
TQ = 256
KV_SLABS = 1 + (LEFT_CHUNKS * CHUNK) // TQ
ROW_BLK = 512
TILE_ROWS = TOP_K * TM
ROUTE_OFF = N_GROUPS
PACK_ROWS = D_MODEL // (2 * LANES)
PACK_W = 2 * LANES
SEG_SIZES = tuple(TM >> k for k in range(TM.bit_length()))
BIG_PIECE = 64
SCORE_AHEAD = 2
GATHER_AHEAD = 2
GATHER_UNROLL = 36
VMEM_LIMIT = 56 * 1024 * 1024

F32 = jnp.float32
BF16 = jnp.bfloat16
U32 = jnp.uint32


def _dot(a, b):
    return jnp.dot(a, b, preferred_element_type=F32)


def _rms(x, g):
    ms = jnp.mean(x * x, axis=-1, keepdims=True)
    return (x * lax.rsqrt(ms + EPS)) * g


def _sigmoid(x):
    return 1.0 / (1.0 + jnp.exp(-x))


def _pack_block(ref, vals, n_rows, a):
    lo = vals[:, 0:LANES].astype(BF16).astype(F32)
    hi = vals[:, LANES:PACK_W].astype(BF16).astype(F32)
    word = (lax.bitcast_convert_type(hi, U32) & U32(0xFFFF0000)) | (lax.bitcast_convert_type(lo, U32) >> 16)
    ref[pl.ds(a, n_rows, stride=PACK_ROWS), :] = word


def _pack_rows(ref, vals, n_rows):
    for a in range(PACK_ROWS):
        _pack_block(ref, vals[:, a * PACK_W:(a + 1) * PACK_W], n_rows, a)


def _unpack_block(ref, n_rows, a, n_valid=None):
    word = ref[pl.ds(a, n_rows, stride=PACK_ROWS), :]
    if n_valid is not None:
        word = jnp.where(lax.broadcasted_iota(jnp.int32, (n_rows, LANES), 0) < n_valid, word, U32(0))
    lo = lax.bitcast_convert_type(word << 16, F32).astype(BF16)
    hi = lax.bitcast_convert_type(word & U32(0xFFFF0000), F32).astype(BF16)
    return jnp.concatenate([lo, hi], axis=1)


def _split(items, n_groups):
    items = list(items)
    return [items[len(items) * g // n_groups:len(items) * (g + 1) // n_groups] for g in range(n_groups)]


def _inproj_kernel(x_ref, g_ref, wqkv_ref, wconv_ref, b_ref, gq_ref, gk_ref, hm_ref,
                   cw_ref, cb_ref, q_ref, k_ref, v_ref, yc_ref, carry_ref, *, tiles_per_seq):
    i = pl.program_id(0)
    nb = _rms(x_ref[...], g_ref[...]).astype(BF16)

    zq = _dot(nb, wqkv_ref[...]) + b_ref[:, 0:3 * ATTN_W]
    hm = hm_ref[...]

    def head_rms(t, g):
        ms = _dot((t * t).astype(BF16), hm)
        return (t * lax.rsqrt(ms + EPS)) * g

    q_ref[...] = head_rms(zq[:, 0:ATTN_W], gq_ref[...]).astype(BF16)
    k_ref[...] = head_rms(zq[:, ATTN_W:2 * ATTN_W], gk_ref[...]).astype(BF16)
    v_ref[...] = zq[:, 2 * ATTN_W:3 * ATTN_W].astype(BF16)

    zc = _dot(nb, wconv_ref[...]) + b_ref[:, 3 * ATTN_W:3 * ATTN_W + 3 * CONV_W]
    u = zc[:, 0:CONV_W]
    bg = zc[:, CONV_W:2 * CONV_W]
    cg = zc[:, 2 * CONV_W:3 * CONV_W]
    cu = cg * u

    @pl.when((i % tiles_per_seq) == 0)
    def _():
        carry_ref[...] = jnp.zeros_like(carry_ref)

    prev = carry_ref[...]
    carry_ref[...] = cu[TM - SUBLANES:TM, :]
    row = lax.broadcasted_iota(jnp.int32, (SUBLANES, CONV_W), 0)

    def shifted(s):
        r = pltpu.roll(cu, s, 0)
        p = pltpu.roll(prev, s, 0)
        top = jnp.where(row < s, p, r[0:SUBLANES, :])
        return jnp.concatenate([top, r[SUBLANES:, :]], axis=0)

    y = cb_ref[...] + cw_ref[0:1, :] * shifted(2)
    y = y + cw_ref[1:2, :] * shifted(1)
    y = y + cw_ref[2:3, :] * cu
    yc_ref[...] = (bg * y).astype(BF16)


def _inproj(x2, g_mix, wqkv, wconv, b_in, gq, gk, hmat, cw, cb, seq):
    t = x2.shape[0]
    const = lambda i: (0, 0)
    row = lambda i: (i, 0)
    out = jax.ShapeDtypeStruct((t, ATTN_W), BF16)
    return pl.pallas_call(
        functools.partial(_inproj_kernel, tiles_per_seq=seq // TM),
        grid=(t // TM,),
        in_specs=[
            pl.BlockSpec((TM, D_MODEL), row),
            pl.BlockSpec((1, D_MODEL), const),
            pl.BlockSpec((D_MODEL, 3 * ATTN_W), const),
            pl.BlockSpec((D_MODEL, 3 * CONV_W), const),
            pl.BlockSpec((1, 3 * ATTN_W + 3 * CONV_W), const),
            pl.BlockSpec((1, ATTN_W), const),
            pl.BlockSpec((1, ATTN_W), const),
            pl.BlockSpec((ATTN_W, ATTN_W), const),
            pl.BlockSpec((SUBLANES, CONV_W), const),
            pl.BlockSpec((1, CONV_W), const),
        ],
        out_specs=[pl.BlockSpec((TM, ATTN_W), row)] * 4,
        out_shape=[out] * 4,
        scratch_shapes=[pltpu.VMEM((SUBLANES, CONV_W), F32)],
        compiler_params=pltpu.CompilerParams(
            dimension_semantics=("arbitrary",), vmem_limit_bytes=VMEM_LIMIT),
        name="inproj",
    )(x2, g_mix, wqkv, wconv, b_in, gq, gk, hmat, cw, cb)


def _lane_fold(parts, op):
    acc = None
    for a in parts:
        for c in range(0, a.shape[1], LANES):
            piece = a[:, c:c + LANES]
            acc = piece if acc is None else op(acc, piece)
    return acc


def _attn_kernel(q_ref, k0_ref, k1_ref, k2_ref, v0_ref, v1_ref, v2_ref, bias_ref, o_ref):
    k_refs = (k0_ref, k1_ref, k2_ref)
    v_refs = (v0_ref, v1_ref, v2_ref)
    pair_w = 2 * HEAD_DIM
    lane = lax.broadcasted_iota(jnp.int32, (TQ, pair_w), 1)
    low = lane < HEAD_DIM

    def scores(h):
        ps = slice((h // 2) * pair_w, (h // 2 + 1) * pair_w)
        q_pair = q_ref[0, :, ps]
        own = low if h % 2 == 0 else jnp.logical_not(low)
        qh = jnp.where(own, q_pair, jnp.zeros_like(q_pair))
        return [lax.dot_general(qh, k_refs[j][0, :, ps], (((1,), (1,)), ((), ())),
                                preferred_element_type=F32) + bias_ref[0, h, :, j * TQ:(j + 1) * TQ]
                for j in range(KV_SLABS)]

    def weighted(h, s):
        ps = slice((h // 2) * pair_w, (h // 2 + 1) * pair_w)
        m = _lane_fold(s, jnp.maximum).max(axis=-1, keepdims=True)
        e = [jnp.exp2(sj - m) for sj in s]
        l = _lane_fold(e, jnp.add).sum(axis=-1, keepdims=True)
        acc = None
        for j in range(KV_SLABS):
            oj = _dot(e[j].astype(BF16), v_refs[j][0, :, ps])
            acc = oj if acc is None else acc + oj
        return acc * (1.0 / l)

    pending = [scores(h) for h in range(SCORE_AHEAD)]
    o_even = None
    for h in range(N_HEADS):
        if h + SCORE_AHEAD < N_HEADS:
            pending.append(scores(h + SCORE_AHEAD))
        o = weighted(h, pending.pop(0))
        if h % 2 == 0:
            o_even = o
        else:
            ps = slice((h // 2) * pair_w, (h // 2 + 1) * pair_w)
            o_ref[0, :, ps] = jnp.where(low, o_even, o).astype(BF16)


def _attention(q, k, v, bias):
    b, s, _ = q.shape
    blk = (1, TQ, ATTN_W)

    def kv_map(j):
        back = KV_SLABS - 1 - j
        return lambda bi, i: (bi, jnp.maximum(i - back, 0), 0)

    kv_specs = [pl.BlockSpec(blk, kv_map(j)) for j in range(KV_SLABS)]
    n_var = bias.shape[0]
    return pl.pallas_call(
        _attn_kernel,
        grid=(b, s // TQ),
        in_specs=[pl.BlockSpec(blk, lambda bi, i: (bi, i, 0))] + kv_specs + kv_specs + [
            pl.BlockSpec((1, N_HEADS, TQ, KV_SLABS * TQ),
                         lambda bi, i: (jnp.minimum(i, n_var - 1), 0, 0, 0))],
        out_specs=pl.BlockSpec(blk, lambda bi, i: (bi, i, 0)),
        out_shape=jax.ShapeDtypeStruct((b, s, ATTN_W), BF16),
        compiler_params=pltpu.CompilerParams(
            dimension_semantics=("arbitrary", "arbitrary"), vmem_limit_bytes=VMEM_LIMIT),
        name="attn",
    )(q, k, k, k, v, v, v, bias)


def _attn_bias(rel_bias):
    nk = KV_SLABS * TQ
    past = nk - TQ
    d = jnp.arange(TQ - 1 + past, -TQ, -1)
    idx = jnp.clip(d, -(CHUNK - 1), MAX_REL_PAST) + (CHUNK - 1)
    onehot = (idx[:, None] == jnp.arange(rel_bias.shape[1])[None, :]).astype(F32)
    per_dist = jnp.einsum("dn,hn->hd", onehot, rel_bias.astype(F32) * LOG2E,
                          precision=lax.Precision.HIGHEST)
    n_h, span = per_dist.shape
    padded = jnp.pad(per_dist, ((0, 0), (0, 2)))
    skew = jnp.tile(padded, (1, TQ))[:, :TQ * (span + 1)].reshape(n_h, TQ, span + 1)
    table = skew[:, :, TQ - 1:TQ - 1 + nk]
    r = jnp.arange(TQ)[:, None]
    c = jnp.arange(nk)[None, :]
    qc = r // CHUNK
    kc = c // CHUNK
    lead = past // CHUNK - LEFT_CHUNKS
    band = (kc >= qc + lead) & (kc <= qc + lead + LEFT_CHUNKS)
    variants = []
    for var in range(KV_SLABS):
        valid = band & (c >= (KV_SLABS - 1 - var) * TQ)
        variants.append(jnp.where(valid[None], table, NEG))
    return jnp.stack(variants, axis=0)


def _merge_kernel(x_ref, ya_ref, yc_ref, g_ref, wg_ref, bgate_ref, wpa_ref, wpc_ref, wo_ref,
                  gffn_ref, wrt_ref, brt_ref, h_ref, stage_ref, route_ref, cnt_ref, n2_scr, logit_scr):
    @pl.when(pl.program_id(0) == 0)
    def _():
        n2_scr[...] = jnp.zeros_like(n2_scr)
        logit_scr[...] = jnp.zeros_like(logit_scr)

    n2 = n2_scr[...]
    logits = logit_scr[...]

    x = x_ref[...]
    nb = _rms(x, g_ref[...]).astype(BF16)
    sga = _sigmoid(_dot(nb, wg_ref[:, 0:D_MODEL]) + bgate_ref[:, 0:D_MODEL])
    ma = sga * _dot(ya_ref[...], wpa_ref[...])
    sgc = _sigmoid(_dot(nb, wg_ref[:, D_MODEL:2 * D_MODEL]) + bgate_ref[:, D_MODEL:2 * D_MODEL])
    m = ma + sgc * _dot(yc_ref[...], wpc_ref[...])
    h = x + _dot(m.astype(BF16), wo_ref[...])
    h_ref[...] = h
    n2_new = _rms(h, gffn_ref[...]).astype(BF16)
    n2_scr[...] = n2_new
    logit_scr[...] = _dot(n2_new, wrt_ref[...]) + brt_ref[...]

    lane = lax.broadcasted_iota(jnp.int32, (TM, LANES), 1).astype(F32)
    ninf = -jnp.inf

    def argmax_first(vals):
        mx = vals.max(axis=-1, keepdims=True)
        idx = jnp.where(vals == mx, lane, float(LANES)).min(axis=-1, keepdims=True)
        return mx, idx

    gmask = lane < N_GROUPS
    gmax, grp = argmax_first(jnp.where(gmask, logits, ninf))
    gsum = jnp.where(gmask, jnp.exp(logits - gmax), 0.0).sum(axis=-1, keepdims=True)
    p_grp = 1.0 / gsum
    first = ROUTE_OFF + EXPERTS_PER_GROUP * grp
    el = jnp.where((lane >= first) & (lane < first + EXPERTS_PER_GROUP), logits, ninf)
    l1, i1 = argmax_first(el)
    l2, i2 = argmax_first(jnp.where(lane == i1, ninf, el))
    e2 = jnp.exp(l2 - l1)
    den = 1.0 + e2
    w1 = p_grp * (1.0 / den)
    w2 = p_grp * (e2 / den)

    oh1 = (lane == i1).astype(F32)
    oh2 = (lane == i2).astype(F32)
    oh = (oh1 + oh2).astype(BF16)
    r = lax.broadcasted_iota(jnp.int32, (TM, TM), 0)
    c = lax.broadcasted_iota(jnp.int32, (TM, TM), 1)
    earlier_tok = _dot((c < r).astype(BF16), oh)
    er = lax.broadcasted_iota(jnp.int32, (LANES, LANES), 0)
    ec = lax.broadcasted_iota(jnp.int32, (LANES, LANES), 1)
    lower_exp = _dot(oh, (er < ec).astype(BF16)).sum(axis=0, keepdims=True)
    where = earlier_tok + lower_exp
    pos1 = (oh1 * where).sum(axis=-1, keepdims=True)
    pos2 = (oh2 * where).sum(axis=-1, keepdims=True)
    cnt_ref[0] = jnp.broadcast_to(oh.astype(F32).sum(axis=0, keepdims=True), (SUBLANES, LANES))

    cols = (i1 - ROUTE_OFF, i2 - ROUTE_OFF, w1, w2, pos1, pos2)
    route = jnp.zeros((TM, LANES), F32)
    for j, col in enumerate(cols):
        route = jnp.where(lane == j, col, route)
    route_ref[...] = route

    route_t = route.T
    slot = lax.broadcasted_iota(jnp.int32, (TILE_ROWS, TM), 0).astype(F32)
    place = ((slot == route_t[4:5, :]) | (slot == route_t[5:6, :])).astype(BF16)
    _pack_rows(stage_ref, _dot(place, n2), TILE_ROWS)


def _merge(x2, ya, yc, g_mix, wgate, bgate, wpa, wpc, wo, gffn, wrt, brt):
    t = x2.shape[0]
    n_tiles = t // TM
    const = lambda i: (0, 0)
    row = lambda i: (jnp.minimum(i, n_tiles - 1), 0)
    late = lambda i: (jnp.maximum(i - 1, 0), 0)
    return pl.pallas_call(
        _merge_kernel,
        grid=(n_tiles + 1,),
        in_specs=[
            pl.BlockSpec((TM, D_MODEL), row),
            pl.BlockSpec((TM, ATTN_W), row),
            pl.BlockSpec((TM, CONV_W), row),
            pl.BlockSpec((1, D_MODEL), const),
            pl.BlockSpec((D_MODEL, 2 * D_MODEL), const),
            pl.BlockSpec((1, 2 * D_MODEL), const),
            pl.BlockSpec((ATTN_W, D_MODEL), const),
            pl.BlockSpec((CONV_W, D_MODEL), const),
            pl.BlockSpec((D_MODEL, D_MODEL), const),
            pl.BlockSpec((1, D_MODEL), const),
            pl.BlockSpec((D_MODEL, LANES), const),
            pl.BlockSpec((1, LANES), const),
        ],
        out_specs=[
            pl.BlockSpec((TM, D_MODEL), row),
            pl.BlockSpec((TILE_ROWS * PACK_ROWS, LANES), late),
            pl.BlockSpec((TM, LANES), late),
            pl.BlockSpec((1, SUBLANES, LANES), lambda i: (jnp.maximum(i - 1, 0), 0, 0)),
        ],
        out_shape=[
            jax.ShapeDtypeStruct((t, D_MODEL), F32),
            jax.ShapeDtypeStruct((n_tiles * TILE_ROWS * PACK_ROWS, LANES), U32),
            jax.ShapeDtypeStruct((t, LANES), F32),
            jax.ShapeDtypeStruct((n_tiles, SUBLANES, LANES), F32),
        ],
        scratch_shapes=[pltpu.VMEM((TM, D_MODEL), BF16), pltpu.VMEM((TM, LANES), F32)],
        compiler_params=pltpu.CompilerParams(
            dimension_semantics=("arbitrary",), vmem_limit_bytes=VMEM_LIMIT),
        name="merge",
    )(x2, ya, yc, g_mix, wgate, bgate, wpa, wpc, wo, gffn, wrt, brt)


def _piece_counts(n):
    return [(n >> (size.bit_length() - 1)) & 1 for size in SEG_SIZES]


ALL_CLASSES = tuple(range(len(SEG_SIZES)))
BIG_CLASSES = tuple(c for c in ALL_CLASSES if SEG_SIZES[c] >= BIG_PIECE)
SMALL_CLASSES = tuple(c for c in ALL_CLASSES if SEG_SIZES[c] < BIG_PIECE)


def _segment_pieces(n, visit, classes=ALL_CLASSES):
    for cls in classes:
        size = SEG_SIZES[cls]

        @pl.when((n & size) != 0)
        def _(cls=cls, size=size):
            visit(cls, n & ~(2 * size - 1))


def _piece_copy(src_ref, dst_ref, sems, cls, src_row, dst_row):
    n = SEG_SIZES[cls] * PACK_ROWS
    return pltpu.make_async_copy(src_ref.at[pl.ds(src_row * PACK_ROWS, n), :],
                                 dst_ref.at[pl.ds(dst_row * PACK_ROWS, n), :], sems.at[cls])


def _drain(src_ref, dst_ref, sems, counts):
    unroll = 4
    for cls in range(len(SEG_SIZES)):
        def wait_some(k, cls=cls):
            def body(t, carry):
                for _ in range(k):
                    _piece_copy(src_ref, dst_ref, sems, cls, 0, 0).wait()
                return carry
            return body

        n = counts[cls]
        lax.fori_loop(0, n >> 2, wait_some(unroll), 0)
        lax.fori_loop(0, n & (unroll - 1), wait_some(1), 0)


def _experts_kernel(be_ref, nb_ref, base_ref, jlo_ref, jhi_ref, nv_ref, big_ref, npiece_ref, cum_ref, end_ref,
                    src_ref, stage_hbm, w1_ref, w3_ref, w2_ref, ys_ref, *scratch, n_tiles, n_blocks):
    xbufs = scratch[:GATHER_AHEAD + 1]
    w1b_ref, w3b_ref, w2b_ref, sems = scratch[GATHER_AHEAD + 1:]
    s = pl.program_id(0)
    nb = nb_ref[0]
    n_cls = len(SEG_SIZES)

    def segment_copies(step, j, live, buf, sem, classes=ALL_CLASSES):
        base = base_ref[step]
        g = jnp.minimum(j, n_tiles - 1) * N_EXPERTS + be_ref[step]
        lo = jnp.maximum(cum_ref[g], base)
        hi = jnp.minimum(end_ref[g], base + ROW_BLK)
        src = src_ref[g] + lo
        dst = lo - base
        _segment_pieces(jnp.where(live, jnp.maximum(hi - lo, 0), 0),
                        lambda cls, o: _piece_copy(stage_hbm, buf, sem, cls, src + o, dst + o).start(), classes)

    def looped_copies(step, j0, j1, buf, sem, classes=ALL_CLASSES):
        def body(j, carry):
            segment_copies(step, j, True, buf, sem, classes)
            return carry

        lax.fori_loop(j0, j1, body, 0)

    @pl.when(s == 0)
    def _():
        for buf in xbufs:
            buf[...] = jnp.zeros_like(buf)
        for first in range(GATHER_AHEAD):
            blk = min(first, n_blocks - 1)
            looped_copies(blk, jlo_ref[blk], jnp.where(first < nb, jhi_ref[blk], jlo_ref[blk]),
                          xbufs[first], sems.at[first])

    def step(cur, cur_sem, nxt, nxt_sem):
        prev = be_ref[jnp.maximum(s - 1, 0)]

        @pl.when((s == 0) | (be_ref[s] != prev))
        def _():
            w1b_ref[...] = w1_ref[0].astype(BF16)
            w3b_ref[...] = w3_ref[0].astype(BF16)
            w2b_ref[...] = w2_ref[0].astype(BF16)

        _drain(stage_hbm, cur, cur_sem, [npiece_ref[s * n_cls + c] for c in range(n_cls)])

        nxt_step = jnp.minimum(s + GATHER_AHEAD, n_blocks - 1)
        live = s + GATHER_AHEAD < nb
        j0 = jlo_ref[nxt_step]
        j1 = jnp.where(live, jhi_ref[nxt_step], j0)
        looped_copies(nxt_step, j0 + GATHER_UNROLL, j1, nxt, nxt_sem, SMALL_CLASSES)
        looped_copies(nxt_step, j0, jnp.where(big_ref[nxt_step] != 0, j1, j0), nxt, nxt_sem, BIG_CLASSES)
        groups = iter(_split(range(GATHER_UNROLL), 2 * PACK_ROWS))

        def start_group():
            for k in next(groups):
                segment_copies(nxt_step, j0 + k, j0 + k < j1, nxt, nxt_sem, SMALL_CLASSES)

        a = None
        g = None
        for blk in range(PACK_ROWS):
            start_group()
            xa = _unpack_block(cur, ROW_BLK, blk, n_valid=nv_ref[s])
            rows = slice(blk * PACK_W, (blk + 1) * PACK_W)
            da = _dot(xa, w1b_ref[rows, :])
            dg = _dot(xa, w3b_ref[rows, :])
            a = da if a is None else a + da
            g = dg if g is None else g + dg
        hdn = ((a * _sigmoid(a)) * g).astype(BF16)
        for blk in range(PACK_ROWS):
            start_group()
            _pack_block(ys_ref, _dot(hdn, w2b_ref[:, blk * PACK_W:(blk + 1) * PACK_W]), ROW_BLK, blk)

    n_buf = len(xbufs)
    for slot in range(n_buf):
        @pl.when((s < nb) & (s % n_buf == slot))
        def _(slot=slot):
            ahead = (slot + GATHER_AHEAD) % n_buf
            step(xbufs[slot], sems.at[slot], xbufs[ahead], sems.at[ahead])

    @pl.when(s >= nb)
    def _():
        ys_ref[...] = jnp.zeros_like(ys_ref)


def _experts(blk_e, nblk, base, jlo, jhi, nvalid, big, npiece, cum, end, src, stage, w1, w3, w2):
    n_blocks = blk_e.shape[0]
    n_tiles = cum.shape[0] // N_EXPERTS

    def wsel(s, be, nb, *_):
        return (be[jnp.minimum(s, nb[0] - 1)], 0, 0)

    grid_spec = pltpu.PrefetchScalarGridSpec(
        num_scalar_prefetch=11,
        grid=(n_blocks,),
        in_specs=[
            pl.BlockSpec(memory_space=pl.ANY),
            pl.BlockSpec((1, D_MODEL, D_EXPERT), wsel),
            pl.BlockSpec((1, D_MODEL, D_EXPERT), wsel),
            pl.BlockSpec((1, D_EXPERT, D_MODEL), wsel),
        ],
        out_specs=pl.BlockSpec((ROW_BLK * PACK_ROWS, LANES), lambda s, *_: (s, 0)),
        scratch_shapes=[pltpu.VMEM((ROW_BLK * PACK_ROWS, LANES), U32)] * (GATHER_AHEAD + 1) + [
            pltpu.VMEM((D_MODEL, D_EXPERT), BF16),
            pltpu.VMEM((D_MODEL, D_EXPERT), BF16),
            pltpu.VMEM((D_EXPERT, D_MODEL), BF16),
            pltpu.SemaphoreType.DMA((GATHER_AHEAD + 1, len(SEG_SIZES))),
        ],
    )
    return pl.pallas_call(
        functools.partial(_experts_kernel, n_tiles=n_tiles, n_blocks=n_blocks),
        grid_spec=grid_spec,
        out_shape=jax.ShapeDtypeStruct((n_blocks * ROW_BLK * PACK_ROWS, LANES), U32),
        compiler_params=pltpu.CompilerParams(
            dimension_semantics=("arbitrary",), vmem_limit_bytes=VMEM_LIMIT),
        name="experts",
    )(blk_e, nblk, base, jlo, jhi, nvalid, big, npiece, cum, end, src, stage, w1, w3, w2)


def _combine_kernel(cnt_ref, off_ref, dst_ref, big_ref, npiece_ref, h_ref, route_ref, p_ref, gple_ref, wpg_ref,
                    bpg_ref, wpp_ref, ys_hbm, o_ref, *scratch, n_steps):
    ybufs, sems = scratch[:-1], scratch[-1]
    i = pl.program_id(0)
    n_cls = len(SEG_SIZES)

    def segment_copies(step, e, live, buf, sem, classes=ALL_CLASSES):
        g = step * N_EXPERTS + e
        off, dst = off_ref[g], dst_ref[g]
        _segment_pieces(jnp.where(live, cnt_ref[g], 0),
                        lambda cls, o: _piece_copy(ys_hbm, buf, sem, cls, dst + o, off + o).start(), classes)

    def looped_copies(step, n_experts, buf, sem, classes=ALL_CLASSES):
        def body(e, carry):
            segment_copies(step, e, True, buf, sem, classes)
            return carry

        lax.fori_loop(0, n_experts, body, 0)

    @pl.when(i == 0)
    def _():
        for first in range(min(GATHER_AHEAD, n_steps)):
            looped_copies(first, N_EXPERTS, ybufs[first], sems.at[first])

    def step(cur, cur_sem, nxt, nxt_sem):
        _drain(ys_hbm, cur, cur_sem, [npiece_ref[i * n_cls + c] for c in range(n_cls)])
        nxt_step = jnp.minimum(i + GATHER_AHEAD, n_steps - 1)
        live = i + GATHER_AHEAD < n_steps
        looped_copies(nxt_step, jnp.where(live & (big_ref[nxt_step] != 0), N_EXPERTS, 0), nxt, nxt_sem, BIG_CLASSES)
        groups = iter(_split(range(N_EXPERTS), PACK_ROWS + 2))

        def start_group():
            for e in next(groups):
                segment_copies(nxt_step, e, live, nxt, nxt_sem, SMALL_CLASSES)

        start_group()
        pp = _dot(p_ref[...].astype(BF16), wpp_ref[...])
        route = route_ref[...]
        place = lax.broadcasted_iota(jnp.int32, (TM, TILE_ROWS), 1).astype(F32)
        sel = [(place == route[:, 4 + kk:5 + kk]).astype(BF16) for kk in range(TOP_K)]
        moe = []
        for blk in range(PACK_ROWS):
            start_group()
            cols = _unpack_block(cur, TILE_ROWS, blk)
            moe.append(_dot(sel[0], cols) * route[:, 2:3] + _dot(sel[1], cols) * route[:, 3:4])
        start_group()
        h = h_ref[...] + jnp.concatenate(moe, axis=1)
        gate = _sigmoid(_dot(_rms(h, gple_ref[...]).astype(BF16), wpg_ref[...]) + bpg_ref[...])
        o_ref[...] = h + gate * pp

    n_buf = len(ybufs)
    for slot in range(n_buf):
        @pl.when(i % n_buf == slot)
        def _(slot=slot):
            ahead = (slot + GATHER_AHEAD) % n_buf
            step(ybufs[slot], sems.at[slot], ybufs[ahead], sems.at[ahead])


def _combine(cnt, off, dst, big, npiece, h1, route, p2, gple, wpg, bpg, wpp, ys):
    t = h1.shape[0]
    n_steps = t // TM
    const = lambda i, *_: (0, 0)
    row = lambda i, *_: (i, 0)
    grid_spec = pltpu.PrefetchScalarGridSpec(
        num_scalar_prefetch=5,
        grid=(n_steps,),
        in_specs=[
            pl.BlockSpec((TM, D_MODEL), row),
            pl.BlockSpec((TM, LANES), row),
            pl.BlockSpec((TM, PLE_DIM), row),
            pl.BlockSpec((1, D_MODEL), const),
            pl.BlockSpec((D_MODEL, D_MODEL), const),
            pl.BlockSpec((1, D_MODEL), const),
            pl.BlockSpec((PLE_DIM, D_MODEL), const),
            pl.BlockSpec(memory_space=pl.ANY),
        ],
        out_specs=pl.BlockSpec((TM, D_MODEL), row),
        scratch_shapes=[pltpu.VMEM((TILE_ROWS * PACK_ROWS, LANES), U32)] * (GATHER_AHEAD + 1) + [
            pltpu.SemaphoreType.DMA((GATHER_AHEAD + 1, len(SEG_SIZES))),
        ],
    )
    return pl.pallas_call(
        functools.partial(_combine_kernel, n_steps=n_steps),
        grid_spec=grid_spec,
        out_shape=jax.ShapeDtypeStruct((t, D_MODEL), F32),
        compiler_params=pltpu.CompilerParams(
            dimension_semantics=("arbitrary",), vmem_limit_bytes=VMEM_LIMIT),
        name="combine",
    )(cnt, off, dst, big, npiece, h1, route, p2, gple, wpg, bpg, wpp, ys)


def _layer(h, p_i, g_mix, w_in, b_in, g_q, g_k, rel_bias, conv_w, conv_b, w_pa, w_pc, w_o,
           g_ffn, w_group, b_group, w_router, b_router, w1, w3, w2,
           g_ple, w_ple_gate, b_ple_gate, w_ple_proj):
    b, s, d = h.shape
    t = b * s
    x2 = h.reshape(t, d)
    row2 = lambda a: a.reshape(1, -1).astype(F32)

    qkv_w = 3 * ATTN_W
    conv_end = qkv_w + 3 * CONV_W
    w_in_b = w_in.astype(BF16)
    gq = row2(jnp.tile(g_q.astype(F32) * (HEAD_DIM ** -0.5 * LOG2E), N_HEADS))
    gk = row2(jnp.tile(g_k.astype(F32), N_HEADS))
    head = jnp.arange(ATTN_W) // HEAD_DIM
    hmat = jnp.where(head[:, None] == head[None, :], 1.0 / HEAD_DIM, 0.0).astype(BF16)
    cw = jnp.concatenate([conv_w.astype(F32), jnp.zeros((SUBLANES - CONV_K, CONV_W), F32)], axis=0)

    q, k, v, yc = _inproj(x2, row2(g_mix), w_in_b[:, :qkv_w], w_in_b[:, qkv_w:conv_end],
                          row2(b_in[:conv_end]), gq, gk, hmat, cw, row2(conv_b), s)

    ya = _attention(q.reshape(b, s, ATTN_W), k.reshape(b, s, ATTN_W), v.reshape(b, s, ATTN_W),
                    _attn_bias(rel_bias)).reshape(t, ATTN_W)

    n_pad = LANES - N_GROUPS - N_EXPERTS
    wrt = jnp.concatenate([w_group, w_router, jnp.zeros((d, n_pad), w_group.dtype)], axis=1).astype(BF16)
    brt = row2(jnp.concatenate([b_group, b_router, jnp.zeros((n_pad,), b_group.dtype)]))
    h1, stage, route, cnt_f = _merge(x2, ya, yc, row2(g_mix), w_in_b[:, conv_end:], row2(b_in[conv_end:]),
                                     w_pa.astype(BF16), w_pc.astype(BF16), w_o.astype(BF16),
                                     row2(g_ffn), wrt, brt)

    n_tiles = t // TM
    cnt = cnt_f[:, 0, ROUTE_OFF:ROUTE_OFF + N_EXPERTS].astype(jnp.int32)
    tile_off = jnp.cumsum(cnt, axis=1) - cnt
    tot = cnt.sum(axis=0)
    pcounts = (tot + ROW_BLK - 1) // ROW_BLK * ROW_BLK
    pends = jnp.cumsum(pcounts)
    pstarts = pends - pcounts
    cum = jnp.cumsum(cnt, axis=0) - cnt
    dst = pstarts[None, :] + cum
    n_blocks = (t * TOP_K) // ROW_BLK + N_EXPERTS
    blk_start = jnp.arange(n_blocks, dtype=jnp.int32) * ROW_BLK
    blk_e = jnp.minimum((pends[None, :] <= blk_start[:, None]).sum(axis=1), N_EXPERTS - 1).astype(jnp.int32)
    nblk = (pends[-1:] // ROW_BLK).astype(jnp.int32)
    sel = (jnp.arange(N_EXPERTS, dtype=jnp.int32)[:, None] == blk_e[None, :]).astype(jnp.int32)
    of_block = lambda a: (a[..., None] * sel).sum(axis=-2)
    base = blk_start - of_block(pstarts)
    nvalid = jnp.clip(of_block(tot) - base, 0, ROW_BLK)
    cum_e = of_block(cum)
    cnt_e = of_block(cnt)
    jlo = (cum_e + cnt_e <= base[None, :]).sum(axis=0)
    jhi = (cum_e < base[None, :] + ROW_BLK).sum(axis=0)
    part = jnp.clip(jnp.minimum(cum_e + cnt_e, base[None, :] + ROW_BLK) - jnp.maximum(cum_e, base[None, :]),
                    0, ROW_BLK)
    blk_pieces = jnp.stack(_piece_counts(part), axis=-1).sum(axis=0)
    tile_pieces = jnp.stack(_piece_counts(cnt), axis=-1).sum(axis=1)
    src = jnp.arange(n_tiles, dtype=jnp.int32)[:, None] * TILE_ROWS + tile_off - cum
    flat = lambda a: a.reshape(-1).astype(jnp.int32)

    blk_big = (part >= BIG_PIECE).any(axis=0)
    tile_big = (cnt >= BIG_PIECE).any(axis=1)

    ys = _experts(blk_e, nblk, flat(base), flat(jlo), flat(jhi), flat(nvalid), flat(blk_big), flat(blk_pieces),
                  flat(cum), flat(cum + cnt), flat(src), stage, w1, w3, w2)
    out = _combine(flat(cnt), flat(tile_off), flat(dst), flat(tile_big), flat(tile_pieces), h1, route,
                   p_i.reshape(t, PLE_DIM), row2(g_ple), w_ple_gate.astype(BF16), row2(b_ple_gate),
                   w_ple_proj.astype(BF16), ys)
    return out.reshape(b, s, d)


def kernel(x, p, g_mix, w_in, b_in, g_q, g_k, rel_bias, conv_w, conv_b, w_pa, w_pc, w_o, g_ffn, w_group, b_group, w_router, b_router, w1, w3, w2, g_ple, w_ple_gate, b_ple_gate, w_ple_proj):
    h = x
    for i in range(p.shape[0]):
        h = _layer(h, p[i], g_mix[i], w_in[i], b_in[i], g_q[i], g_k[i], rel_bias[i], conv_w[i], conv_b[i],
                   w_pa[i], w_pc[i], w_o[i], g_ffn[i], w_group[i], b_group[i], w_router[i], b_router[i],
                   w1[i], w3[i], w2[i], g_ple[i], w_ple_gate[i], b_ple_gate[i], w_ple_proj[i])
    return h
```

```python
import functools

import jax
import jax.numpy as jnp
from jax import lax
from jax.experimental import pallas as pl
from jax.experimental.pallas import tpu as pltpu

D_MODEL = 1024
CHUNK = 64
LEFT_CHUNKS = 8
N_HEADS = 8
HEAD_DIM = 64
ATTN_W = N_HEADS * HEAD_DIM
CONV_W = D_MODEL // 2
CONV_K = 3
MAX_REL_PAST = 256
PLE_DIM = 256
N_GROUPS = 4
EXPERTS_PER_GROUP = 8
N_EXPERTS = N_GROUPS * EXPERTS_PER_GROUP
TOP_K = 2
D_EXPERT = 512
EPS = 1e-6
NEG = -1e30
LOG2E = 1.4426950408889634

LANES = 128
SUBLANES = 8
TM = 256
TQ = 256
KV_SLABS = 1 + (LEFT_CHUNKS * CHUNK) // TQ
ROW_BLK = 512
TILE_ROWS = TOP_K * TM
ROUTE_OFF = N_GROUPS
PACK_ROWS = D_MODEL // (2 * LANES)
PACK_W = 2 * LANES
SEG_SIZES = tuple(TM >> k for k in range(TM.bit_length()))
BIG_PIECE = 64
SCORE_AHEAD = 2
GATHER_AHEAD = 2
GATHER_UNROLL = 36
VMEM_LIMIT = 56 * 1024 * 1024

F32 = jnp.float32
BF16 = jnp.bfloat16
U32 = jnp.uint32


def _dot(a, b):
    return jnp.dot(a, b, preferred_element_type=F32)


def _rms(x, g):
    ms = jnp.mean(x * x, axis=-1, keepdims=True)
    return (x * lax.rsqrt(ms + EPS)) * g


def _sigmoid(x):
    return 1.0 / (1.0 + jnp.exp(-x))


def _pack_block(ref, vals, n_rows, a, is_bf16=False):
    lo = vals[:, 0:LANES]
    hi = vals[:, LANES:PACK_W]
    if not is_bf16:
        lo = lo.astype(BF16).astype(F32)
        hi = hi.astype(BF16).astype(F32)
    ref[pl.ds(a, n_rows, stride=PACK_ROWS), :] = (
        lax.bitcast_convert_type(hi, U32) | (lax.bitcast_convert_type(lo, U32) >> 16))


def _pack_rows(ref, vals, n_rows, is_bf16=False):
    for a in range(PACK_ROWS):
        _pack_block(ref, vals[:, a * PACK_W:(a + 1) * PACK_W], n_rows, a, is_bf16)


def _unpack_block(ref, n_rows, a, n_valid=None):
    word = ref[pl.ds(a, n_rows, stride=PACK_ROWS), :]
    if n_valid is not None:
        word = jnp.where(lax.broadcasted_iota(jnp.int32, (n_rows, LANES), 0) < n_valid, word, U32(0))
    lo = lax.bitcast_convert_type(word << 16, F32).astype(BF16)
    hi = lax.bitcast_convert_type(word & U32(0xFFFF0000), F32).astype(BF16)
    return jnp.concatenate([lo, hi], axis=1)


def _split(items, n_groups):
    items = list(items)
    return [items[len(items) * g // n_groups:len(items) * (g + 1) // n_groups] for g in range(n_groups)]


def _inproj_kernel(x_ref, g_ref, wqkv_ref, wconv_ref, b_ref, gq_ref, gk_ref, hm_ref,
                   cw_ref, cb_ref, q_ref, k_ref, v_ref, yc_ref, carry_ref, *, tiles_per_seq):
    i = pl.program_id(0)
    nb = _rms(x_ref[...], g_ref[...]).astype(BF16)

    zq = _dot(nb, wqkv_ref[...]) + b_ref[:, 0:3 * ATTN_W]
    hm = hm_ref[...]

    def head_rms(t, g):
        ms = _dot((t * t).astype(BF16), hm)
        return (t * lax.rsqrt(ms + EPS)) * g

    q_ref[...] = head_rms(zq[:, 0:ATTN_W], gq_ref[...]).astype(BF16)
    k_ref[...] = head_rms(zq[:, ATTN_W:2 * ATTN_W], gk_ref[...]).astype(BF16)
    v_ref[...] = zq[:, 2 * ATTN_W:3 * ATTN_W].astype(BF16)

    zc = _dot(nb, wconv_ref[...]) + b_ref[:, 3 * ATTN_W:3 * ATTN_W + 3 * CONV_W]
    u = zc[:, 0:CONV_W]
    bg = zc[:, CONV_W:2 * CONV_W]
    cg = zc[:, 2 * CONV_W:3 * CONV_W]
    cu = cg * u

    @pl.when((i % tiles_per_seq) == 0)
    def _():
        carry_ref[...] = jnp.zeros_like(carry_ref)

    prev = carry_ref[...]
    carry_ref[...] = cu[TM - SUBLANES:TM, :]
    row = lax.broadcasted_iota(jnp.int32, (SUBLANES, CONV_W), 0)

    def shifted(s):
        r = pltpu.roll(cu, s, 0)
        p = pltpu.roll(prev, s, 0)
        top = jnp.where(row < s, p, r[0:SUBLANES, :])
        return jnp.concatenate([top, r[SUBLANES:, :]], axis=0)

    y = cb_ref[...] + cw_ref[0:1, :] * shifted(2)
    y = y + cw_ref[1:2, :] * shifted(1)
    y = y + cw_ref[2:3, :] * cu
    yc_ref[...] = (bg * y).astype(BF16)


def _inproj(x2, g_mix, wqkv, wconv, b_in, gq, gk, hmat, cw, cb, seq):
    t = x2.shape[0]
    const = lambda i: (0, 0)
    row = lambda i: (i, 0)
    out = jax.ShapeDtypeStruct((t, ATTN_W), BF16)
    return pl.pallas_call(
        functools.partial(_inproj_kernel, tiles_per_seq=seq // TM),
        grid=(t // TM,),
        in_specs=[
            pl.BlockSpec((TM, D_MODEL), row),
            pl.BlockSpec((1, D_MODEL), const),
            pl.BlockSpec((D_MODEL, 3 * ATTN_W), const),
            pl.BlockSpec((D_MODEL, 3 * CONV_W), const),
            pl.BlockSpec((1, 3 * ATTN_W + 3 * CONV_W), const),
            pl.BlockSpec((1, ATTN_W), const),
            pl.BlockSpec((1, ATTN_W), const),
            pl.BlockSpec((ATTN_W, ATTN_W), const),
            pl.BlockSpec((SUBLANES, CONV_W), const),
            pl.BlockSpec((1, CONV_W), const),
        ],
        out_specs=[pl.BlockSpec((TM, ATTN_W), row)] * 4,
        out_shape=[out] * 4,
        scratch_shapes=[pltpu.VMEM((SUBLANES, CONV_W), F32)],
        compiler_params=pltpu.CompilerParams(
            dimension_semantics=("arbitrary",), vmem_limit_bytes=VMEM_LIMIT),
        name="inproj",
    )(x2, g_mix, wqkv, wconv, b_in, gq, gk, hmat, cw, cb)


def _lane_fold(parts, op):
    acc = None
    for a in parts:
        for c in range(0, a.shape[1], LANES):
            piece = a[:, c:c + LANES]
            acc = piece if acc is None else op(acc, piece)
    return acc


def _attn_kernel(q_ref, k0_ref, k1_ref, k2_ref, v0_ref, v1_ref, v2_ref, bias_ref, o_ref):
    k_refs = (k0_ref, k1_ref, k2_ref)
    v_refs = (v0_ref, v1_ref, v2_ref)
    pair_w = 2 * HEAD_DIM
    lane = lax.broadcasted_iota(jnp.int32, (TQ, pair_w), 1)
    low = lane < HEAD_DIM

    def scores(h, pens):
        ps = slice((h // 2) * pair_w, (h // 2 + 1) * pair_w)
        q_pair = q_ref[0, :, ps]
        own = low if h % 2 == 0 else jnp.logical_not(low)
        qh = jnp.where(own, q_pair, jnp.zeros_like(q_pair))
        s = [lax.dot_general(qh, k_refs[j][0, :, ps], (((1,), (1,)), ((), ())),
                             preferred_element_type=F32) + bias_ref[h, :, j * TQ:(j + 1) * TQ]
             for j in range(KV_SLABS)]
        return s if pens is None else [sj + pens[j] for j, sj in enumerate(s)]

    def weighted(h, s):
        ps = slice((h // 2) * pair_w, (h // 2 + 1) * pair_w)
        m = _lane_fold(s, jnp.maximum).max(axis=-1, keepdims=True)
        e = [jnp.exp2(sj - m) for sj in s]
        l = _lane_fold(e, jnp.add).sum(axis=-1, keepdims=True)
        acc = None
        for j in range(KV_SLABS):
            oj = _dot(e[j].astype(BF16), v_refs[j][0, :, ps])
            acc = oj if acc is None else acc + oj
        return acc * (1.0 / l)

    def all_heads(pens):
        pending = [scores(h, pens) for h in range(SCORE_AHEAD)]
        o_even = None
        for h in range(N_HEADS):
            if h + SCORE_AHEAD < N_HEADS:
                pending.append(scores(h + SCORE_AHEAD, pens))
            o = weighted(h, pending.pop(0))
            if h % 2 == 0:
                o_even = o
            else:
                ps = slice((h // 2) * pair_w, (h // 2 + 1) * pair_w)
                o_ref[0, :, ps] = jnp.where(low, o_even, o).astype(BF16)

    i = pl.program_id(1)

    @pl.when(i >= KV_SLABS - 1)
    def _():
        all_heads(None)

    @pl.when(i < KV_SLABS - 1)
    def _():
        all_heads([jnp.where(i >= KV_SLABS - 1 - j, 0.0, NEG).astype(F32) for j in range(KV_SLABS)])


def _attention(q, k, v, bias):
    b, s, _ = q.shape
    blk = (1, TQ, ATTN_W)

    def kv_map(j):
        back = KV_SLABS - 1 - j
        return lambda bi, i: (bi, jnp.maximum(i - back, 0), 0)

    kv_specs = [pl.BlockSpec(blk, kv_map(j)) for j in range(KV_SLABS)]
    return pl.pallas_call(
        _attn_kernel,
        grid=(b, s // TQ),
        in_specs=[pl.BlockSpec(blk, lambda bi, i: (bi, i, 0))] + kv_specs + kv_specs + [
            pl.BlockSpec((N_HEADS, TQ, KV_SLABS * TQ), lambda bi, i: (0, 0, 0))],
        out_specs=pl.BlockSpec(blk, lambda bi, i: (bi, i, 0)),
        out_shape=jax.ShapeDtypeStruct((b, s, ATTN_W), BF16),
        compiler_params=pltpu.CompilerParams(
            dimension_semantics=("arbitrary", "arbitrary"), vmem_limit_bytes=VMEM_LIMIT),
        name="attn",
    )(q, k, k, k, v, v, v, bias)


def _attn_bias(rel_bias):
    nk = KV_SLABS * TQ
    past = nk - TQ
    d = jnp.arange(TQ - 1 + past, -TQ, -1)
    idx = jnp.clip(d, -(CHUNK - 1), MAX_REL_PAST) + (CHUNK - 1)
    onehot = (idx[:, None] == jnp.arange(rel_bias.shape[1])[None, :]).astype(F32)
    per_dist = jnp.einsum("dn,hn->hd", onehot, rel_bias.astype(F32) * LOG2E,
                          precision=lax.Precision.HIGHEST)
    n_h, span = per_dist.shape
    padded = jnp.pad(per_dist, ((0, 0), (0, 2)))
    skew = jnp.tile(padded, (1, TQ))[:, :TQ * (span + 1)].reshape(n_h, TQ, span + 1)
    table = skew[:, :, TQ - 1:TQ - 1 + nk]
    r = jnp.arange(TQ)[:, None]
    c = jnp.arange(nk)[None, :]
    qc = r // CHUNK
    kc = c // CHUNK
    lead = past // CHUNK - LEFT_CHUNKS
    band = (kc >= qc + lead) & (kc <= qc + lead + LEFT_CHUNKS)
    return jnp.where(band[None], table, NEG)


def _merge_kernel(x_ref, ya_ref, yc_ref, g_ref, wg_ref, bgate_ref, wpa_ref, wpc_ref, wo_ref,
                  gffn_ref, wrt_ref, brt_ref, h_ref, stage_ref, route_ref, cnt_ref, n2_scr, logit_scr):
    @pl.when(pl.program_id(0) == 0)
    def _():
        n2_scr[...] = jnp.zeros_like(n2_scr)
        logit_scr[...] = jnp.zeros_like(logit_scr)

    n2 = n2_scr[...]
    logits = logit_scr[...]

    lane = lax.broadcasted_iota(jnp.int32, (TM, LANES), 1).astype(F32)
    ninf = -jnp.inf

    def argmax_first(vals):
        mx = vals.max(axis=-1, keepdims=True)
        idx = jnp.where(vals == mx, lane, float(LANES)).min(axis=-1, keepdims=True)
        return mx, idx

    gmask = lane < N_GROUPS
    gmax, grp = argmax_first(jnp.where(gmask, logits, ninf))
    gsum = jnp.where(gmask, jnp.exp(logits - gmax), 0.0).sum(axis=-1, keepdims=True)
    p_grp = 1.0 / gsum
    first = ROUTE_OFF + EXPERTS_PER_GROUP * grp
    el = jnp.where((lane >= first) & (lane < first + EXPERTS_PER_GROUP), logits, ninf)
    l1, i1 = argmax_first(el)
    l2, i2 = argmax_first(jnp.where(lane == i1, ninf, el))
    e2 = jnp.exp(l2 - l1)
    den = 1.0 + e2
    w1 = p_grp * (1.0 / den)
    w2 = p_grp * (e2 / den)

    oh1 = (lane == i1).astype(F32)
    oh2 = (lane == i2).astype(F32)
    oh = (oh1 + oh2).astype(BF16)
    r = lax.broadcasted_iota(jnp.int32, (TM, TM), 0)
    c = lax.broadcasted_iota(jnp.int32, (TM, TM), 1)
    earlier_tok = _dot((c < r).astype(BF16), oh)
    er = lax.broadcasted_iota(jnp.int32, (LANES, LANES), 0)
    ec = lax.broadcasted_iota(jnp.int32, (LANES, LANES), 1)
    lower_exp = _dot(oh, (er < ec).astype(BF16)).sum(axis=0, keepdims=True)
    where = earlier_tok + lower_exp
    pos1 = (oh1 * where).sum(axis=-1, keepdims=True)
    pos2 = (oh2 * where).sum(axis=-1, keepdims=True)
    cnt_ref[0] = jnp.broadcast_to(oh.astype(F32).sum(axis=0, keepdims=True), (SUBLANES, LANES))

    cols = (i1 - ROUTE_OFF, i2 - ROUTE_OFF, w1, w2, pos1, pos2)
    route = jnp.zeros((TM, LANES), F32)
    for j, col in enumerate(cols):
        route = jnp.where(lane == j, col, route)
    route_ref[...] = route

    x = x_ref[...]
    nb = _rms(x, g_ref[...]).astype(BF16)
    sga = _sigmoid(_dot(nb, wg_ref[:, 0:D_MODEL]) + bgate_ref[:, 0:D_MODEL])
    ma = sga * _dot(ya_ref[...], wpa_ref[...])
    sgc = _sigmoid(_dot(nb, wg_ref[:, D_MODEL:2 * D_MODEL]) + bgate_ref[:, D_MODEL:2 * D_MODEL])
    m = ma + sgc * _dot(yc_ref[...], wpc_ref[...])
    h = x + _dot(m.astype(BF16), wo_ref[...])
    h_ref[...] = h
    n2_new = _rms(h, gffn_ref[...]).astype(BF16)
    n2_scr[...] = n2_new
    logit_scr[...] = _dot(n2_new, wrt_ref[...]) + brt_ref[...]

    route_t = route.T
    slot = lax.broadcasted_iota(jnp.int32, (TILE_ROWS, TM), 0).astype(F32)
    place = ((slot == route_t[4:5, :]) | (slot == route_t[5:6, :])).astype(BF16)
    _pack_rows(stage_ref, _dot(place, n2), TILE_ROWS, is_bf16=True)


def _merge(x2, ya, yc, g_mix, wgate, bgate, wpa, wpc, wo, gffn, wrt, brt):
    t = x2.shape[0]
    n_tiles = t // TM
    const = lambda i: (0, 0)
    row = lambda i: (jnp.minimum(i, n_tiles - 1), 0)
    late = lambda i: (jnp.maximum(i - 1, 0), 0)
    return pl.pallas_call(
        _merge_kernel,
        grid=(n_tiles + 1,),
        in_specs=[
            pl.BlockSpec((TM, D_MODEL), row),
            pl.BlockSpec((TM, ATTN_W), row),
            pl.BlockSpec((TM, CONV_W), row),
            pl.BlockSpec((1, D_MODEL), const),
            pl.BlockSpec((D_MODEL, 2 * D_MODEL), const),
            pl.BlockSpec((1, 2 * D_MODEL), const),
            pl.BlockSpec((ATTN_W, D_MODEL), const),
            pl.BlockSpec((CONV_W, D_MODEL), const),
            pl.BlockSpec((D_MODEL, D_MODEL), const),
            pl.BlockSpec((1, D_MODEL), const),
            pl.BlockSpec((D_MODEL, LANES), const),
            pl.BlockSpec((1, LANES), const),
        ],
        out_specs=[
            pl.BlockSpec((TM, D_MODEL), row),
            pl.BlockSpec((TILE_ROWS * PACK_ROWS, LANES), late),
            pl.BlockSpec((TM, LANES), late),
            pl.BlockSpec((1, SUBLANES, LANES), lambda i: (jnp.maximum(i - 1, 0), 0, 0)),
        ],
        out_shape=[
            jax.ShapeDtypeStruct((t, D_MODEL), F32),
            jax.ShapeDtypeStruct((n_tiles * TILE_ROWS * PACK_ROWS, LANES), U32),
            jax.ShapeDtypeStruct((t, LANES), F32),
            jax.ShapeDtypeStruct((n_tiles, SUBLANES, LANES), F32),
        ],
        scratch_shapes=[pltpu.VMEM((TM, D_MODEL), BF16), pltpu.VMEM((TM, LANES), F32)],
        compiler_params=pltpu.CompilerParams(
            dimension_semantics=("arbitrary",), vmem_limit_bytes=VMEM_LIMIT),
        name="merge",
    )(x2, ya, yc, g_mix, wgate, bgate, wpa, wpc, wo, gffn, wrt, brt)


def _piece_counts(n):
    return [(n >> (size.bit_length() - 1)) & 1 for size in SEG_SIZES]


ALL_CLASSES = tuple(range(len(SEG_SIZES)))
BIG_CLASSES = tuple(c for c in ALL_CLASSES if SEG_SIZES[c] >= BIG_PIECE)
SMALL_CLASSES = tuple(c for c in ALL_CLASSES if SEG_SIZES[c] < BIG_PIECE)


def _segment_pieces(n, visit, classes=ALL_CLASSES):
    for cls in classes:
        size = SEG_SIZES[cls]

        @pl.when((n & size) != 0)
        def _(cls=cls, size=size):
            visit(cls, n & ~(2 * size - 1))


def _piece_copy(src_ref, dst_ref, sems, cls, src_row, dst_row):
    n = SEG_SIZES[cls] * PACK_ROWS
    return pltpu.make_async_copy(src_ref.at[pl.ds(src_row * PACK_ROWS, n), :],
                                 dst_ref.at[pl.ds(dst_row * PACK_ROWS, n), :], sems.at[cls])


def _drain(src_ref, dst_ref, sems, counts, has_big):
    unroll = 4

    def wait_classes(classes):
        for cls in classes:
            def wait_some(k, cls=cls):
                def body(t, carry):
                    for _ in range(k):
                        _piece_copy(src_ref, dst_ref, sems, cls, 0, 0).wait()
                    return carry
                return body

            n = counts[cls]
            lax.fori_loop(0, n >> 2, wait_some(unroll), 0)
            lax.fori_loop(0, n & (unroll - 1), wait_some(1), 0)

    @pl.when(has_big)
    def _():
        wait_classes(BIG_CLASSES)

    wait_classes(SMALL_CLASSES)


def _experts_kernel(be_ref, nb_ref, base_ref, jlo_ref, jhi_ref, nv_ref, big_ref, npiece_ref, cum_ref, end_ref,
                    src_ref, stage_hbm, w1_ref, w3_ref, w2_ref, ys_ref, *scratch, n_tiles, n_blocks):
    xbufs = scratch[:GATHER_AHEAD + 1]
    w1b_ref, w3b_ref, w2b_ref, sems = scratch[GATHER_AHEAD + 1:]
    s = pl.program_id(0)
    nb = nb_ref[0]
    n_cls = len(SEG_SIZES)

    def segment_copies(step, j, live, buf, sem, classes=ALL_CLASSES):
        base = base_ref[step]
        g = jnp.minimum(j, n_tiles - 1) * N_EXPERTS + be_ref[step]
        lo = jnp.maximum(cum_ref[g], base)
        hi = jnp.minimum(end_ref[g], base + ROW_BLK)
        src = src_ref[g] + lo
        dst = lo - base
        _segment_pieces(jnp.where(live, jnp.maximum(hi - lo, 0), 0),
                        lambda cls, o: _piece_copy(stage_hbm, buf, sem, cls, src + o, dst + o).start(), classes)

    def looped_copies(step, j0, j1, buf, sem, classes=ALL_CLASSES):
        def body(j, carry):
            segment_copies(step, j, True, buf, sem, classes)
            return carry

        lax.fori_loop(j0, j1, body, 0)

    @pl.when(s == 0)
    def _():
        for buf in xbufs:
            buf[...] = jnp.zeros_like(buf)
        for first in range(GATHER_AHEAD):
            blk = min(first, n_blocks - 1)
            looped_copies(blk, jlo_ref[blk], jnp.where(first < nb, jhi_ref[blk], jlo_ref[blk]),
                          xbufs[first], sems.at[first])

    def step(cur, cur_sem, nxt, nxt_sem):
        prev = be_ref[jnp.maximum(s - 1, 0)]

        @pl.when((s == 0) | (be_ref[s] != prev))
        def _():
            w1b_ref[...] = w1_ref[0].astype(BF16)
            w3b_ref[...] = w3_ref[0].astype(BF16)
            w2b_ref[...] = w2_ref[0].astype(BF16)

        _drain(stage_hbm, cur, cur_sem, [npiece_ref[s * n_cls + c] for c in range(n_cls)], big_ref[s] != 0)

        nxt_step = jnp.minimum(s + GATHER_AHEAD, n_blocks - 1)
        live = s + GATHER_AHEAD < nb
        j0 = jlo_ref[nxt_step]
        j1 = jnp.where(live, jhi_ref[nxt_step], j0)
        looped_copies(nxt_step, j0 + GATHER_UNROLL, j1, nxt, nxt_sem, SMALL_CLASSES)
        looped_copies(nxt_step, j0, jnp.where(big_ref[nxt_step] != 0, j1, j0), nxt, nxt_sem, BIG_CLASSES)
        groups = iter(_split(range(GATHER_UNROLL), 2 * PACK_ROWS))

        def start_group():
            for k in next(groups):
                segment_copies(nxt_step, j0 + k, j0 + k < j1, nxt, nxt_sem, SMALL_CLASSES)

        a = None
        g = None
        for blk in range(PACK_ROWS):
            start_group()
            xa = _unpack_block(cur, ROW_BLK, blk, n_valid=nv_ref[s])
            rows = slice(blk * PACK_W, (blk + 1) * PACK_W)
            da = _dot(xa, w1b_ref[rows, :])
            dg = _dot(xa, w3b_ref[rows, :])
            a = da if a is None else a + da
            g = dg if g is None else g + dg
        hdn = ((a * _sigmoid(a)) * g).astype(BF16)
        for blk in range(PACK_ROWS):
            start_group()
            _pack_block(ys_ref, _dot(hdn, w2b_ref[:, blk * PACK_W:(blk + 1) * PACK_W]), ROW_BLK, blk)

    n_buf = len(xbufs)
    for slot in range(n_buf):
        @pl.when((s < nb) & (s % n_buf == slot))
        def _(slot=slot):
            ahead = (slot + GATHER_AHEAD) % n_buf
            step(xbufs[slot], sems.at[slot], xbufs[ahead], sems.at[ahead])

    @pl.when(s >= nb)
    def _():
        ys_ref[...] = jnp.zeros_like(ys_ref)


def _experts(blk_e, nblk, base, jlo, jhi, nvalid, big, npiece, cum, end, src, stage, w1, w3, w2):
    n_blocks = blk_e.shape[0]
    n_tiles = cum.shape[0] // N_EXPERTS

    def wsel(s, be, nb, *_):
        return (be[jnp.minimum(s, nb[0] - 1)], 0, 0)

    grid_spec = pltpu.PrefetchScalarGridSpec(
        num_scalar_prefetch=11,
        grid=(n_blocks,),
        in_specs=[
            pl.BlockSpec(memory_space=pl.ANY),
            pl.BlockSpec((1, D_MODEL, D_EXPERT), wsel),
            pl.BlockSpec((1, D_MODEL, D_EXPERT), wsel),
            pl.BlockSpec((1, D_EXPERT, D_MODEL), wsel),
        ],
        out_specs=pl.BlockSpec((ROW_BLK * PACK_ROWS, LANES), lambda s, *_: (s, 0)),
        scratch_shapes=[pltpu.VMEM((ROW_BLK * PACK_ROWS, LANES), U32)] * (GATHER_AHEAD + 1) + [
            pltpu.VMEM((D_MODEL, D_EXPERT), BF16),
            pltpu.VMEM((D_MODEL, D_EXPERT), BF16),
            pltpu.VMEM((D_EXPERT, D_MODEL), BF16),
            pltpu.SemaphoreType.DMA((GATHER_AHEAD + 1, len(SEG_SIZES))),
        ],
    )
    return pl.pallas_call(
        functools.partial(_experts_kernel, n_tiles=n_tiles, n_blocks=n_blocks),
        grid_spec=grid_spec,
        out_shape=jax.ShapeDtypeStruct((n_blocks * ROW_BLK * PACK_ROWS, LANES), U32),
        compiler_params=pltpu.CompilerParams(
            dimension_semantics=("arbitrary",), vmem_limit_bytes=VMEM_LIMIT),
        name="experts",
    )(blk_e, nblk, base, jlo, jhi, nvalid, big, npiece, cum, end, src, stage, w1, w3, w2)


def _combine_kernel(cnt_ref, off_ref, dst_ref, big_ref, npiece_ref, h_ref, route_ref, p_ref, gple_ref, wpg_ref,
                    bpg_ref, wpp_ref, ys_hbm, o_ref, *scratch, n_steps):
    ybufs, sems = scratch[:-1], scratch[-1]
    i = pl.program_id(0)
    n_cls = len(SEG_SIZES)

    def segment_copies(step, e, live, buf, sem, classes=ALL_CLASSES):
        g = step * N_EXPERTS + e
        off, dst = off_ref[g], dst_ref[g]
        _segment_pieces(jnp.where(live, cnt_ref[g], 0),
                        lambda cls, o: _piece_copy(ys_hbm, buf, sem, cls, dst + o, off + o).start(), classes)

    def looped_copies(step, n_experts, buf, sem, classes=ALL_CLASSES):
        def body(e, carry):
            segment_copies(step, e, True, buf, sem, classes)
            return carry

        lax.fori_loop(0, n_experts, body, 0)

    @pl.when(i == 0)
    def _():
        for first in range(min(GATHER_AHEAD, n_steps)):
            looped_copies(first, N_EXPERTS, ybufs[first], sems.at[first])

    def step(cur, cur_sem, nxt, nxt_sem):
        _drain(ys_hbm, cur, cur_sem, [npiece_ref[i * n_cls + c] for c in range(n_cls)], big_ref[i] != 0)
        nxt_step = jnp.minimum(i + GATHER_AHEAD, n_steps - 1)
        live = i + GATHER_AHEAD < n_steps
        looped_copies(nxt_step, jnp.where(live & (big_ref[nxt_step] != 0), N_EXPERTS, 0), nxt, nxt_sem, BIG_CLASSES)
        groups = iter(_split(range(N_EXPERTS), PACK_ROWS + 2))

        def start_group():
            for e in next(groups):
                segment_copies(nxt_step, e, live, nxt, nxt_sem, SMALL_CLASSES)

        start_group()
        pp = _dot(p_ref[...].astype(BF16), wpp_ref[...])
        route = route_ref[...]
        place = lax.broadcasted_iota(jnp.int32, (TM, TILE_ROWS), 1).astype(F32)
        sel = [(place == route[:, 4 + kk:5 + kk]).astype(BF16) for kk in range(TOP_K)]
        moe = []
        for blk in range(PACK_ROWS):
            start_group()
            cols = _unpack_block(cur, TILE_ROWS, blk)
            moe.append(_dot(sel[0], cols) * route[:, 2:3] + _dot(sel[1], cols) * route[:, 3:4])
        start_group()
        h = h_ref[...] + jnp.concatenate(moe, axis=1)
        gate = _sigmoid(_dot(_rms(h, gple_ref[...]).astype(BF16), wpg_ref[...]) + bpg_ref[...])
        o_ref[...] = h + gate * pp

    n_buf = len(ybufs)
    for slot in range(n_buf):
        @pl.when(i % n_buf == slot)
        def _(slot=slot):
            ahead = (slot + GATHER_AHEAD) % n_buf
            step(ybufs[slot], sems.at[slot], ybufs[ahead], sems.at[ahead])


def _combine(cnt, off, dst, big, npiece, h1, route, p2, gple, wpg, bpg, wpp, ys):
    t = h1.shape[0]
    n_steps = t // TM
    const = lambda i, *_: (0, 0)
    row = lambda i, *_: (i, 0)
    grid_spec = pltpu.PrefetchScalarGridSpec(
        num_scalar_prefetch=5,
        grid=(n_steps,),
        in_specs=[
            pl.BlockSpec((TM, D_MODEL), row),
            pl.BlockSpec((TM, LANES), row),
            pl.BlockSpec((TM, PLE_DIM), row),
            pl.BlockSpec((1, D_MODEL), const),
            pl.BlockSpec((D_MODEL, D_MODEL), const),
            pl.BlockSpec((1, D_MODEL), const),
            pl.BlockSpec((PLE_DIM, D_MODEL), const),
            pl.BlockSpec(memory_space=pl.ANY),
        ],
        out_specs=pl.BlockSpec((TM, D_MODEL), row),
        scratch_shapes=[pltpu.VMEM((TILE_ROWS * PACK_ROWS, LANES), U32)] * (GATHER_AHEAD + 1) + [
            pltpu.SemaphoreType.DMA((GATHER_AHEAD + 1, len(SEG_SIZES))),
        ],
    )
    return pl.pallas_call(
        functools.partial(_combine_kernel, n_steps=n_steps),
        grid_spec=grid_spec,
        out_shape=jax.ShapeDtypeStruct((t, D_MODEL), F32),
        compiler_params=pltpu.CompilerParams(
            dimension_semantics=("arbitrary",), vmem_limit_bytes=VMEM_LIMIT),
        name="combine",
    )(cnt, off, dst, big, npiece, h1, route, p2, gple, wpg, bpg, wpp, ys)


def _layer(h, p_i, g_mix, w_in, b_in, g_q, g_k, rel_bias, conv_w, conv_b, w_pa, w_pc, w_o,
           g_ffn, w_group, b_group, w_router, b_router, w1, w3, w2,
           g_ple, w_ple_gate, b_ple_gate, w_ple_proj):
    b, s, d = h.shape
    t = b * s
    x2 = h.reshape(t, d)
    row2 = lambda a: a.reshape(1, -1).astype(F32)

    qkv_w = 3 * ATTN_W
    conv_end = qkv_w + 3 * CONV_W
    w_in_b = w_in.astype(BF16)
    gq = row2(jnp.tile(g_q.astype(F32) * (HEAD_DIM ** -0.5 * LOG2E), N_HEADS))
    gk = row2(jnp.tile(g_k.astype(F32), N_HEADS))
    head = jnp.arange(ATTN_W) // HEAD_DIM
    hmat = jnp.where(head[:, None] == head[None, :], 1.0 / HEAD_DIM, 0.0).astype(BF16)
    cw = jnp.concatenate([conv_w.astype(F32), jnp.zeros((SUBLANES - CONV_K, CONV_W), F32)], axis=0)

    q, k, v, yc = _inproj(x2, row2(g_mix), w_in_b[:, :qkv_w], w_in_b[:, qkv_w:conv_end],
                          row2(b_in[:conv_end]), gq, gk, hmat, cw, row2(conv_b), s)

    ya = _attention(q.reshape(b, s, ATTN_W), k.reshape(b, s, ATTN_W), v.reshape(b, s, ATTN_W),
                    _attn_bias(rel_bias)).reshape(t, ATTN_W)

    n_pad = LANES - N_GROUPS - N_EXPERTS
    wrt = jnp.concatenate([w_group, w_router, jnp.zeros((d, n_pad), w_group.dtype)], axis=1).astype(BF16)
    brt = row2(jnp.concatenate([b_group, b_router, jnp.zeros((n_pad,), b_group.dtype)]))
    h1, stage, route, cnt_f = _merge(x2, ya, yc, row2(g_mix), w_in_b[:, conv_end:], row2(b_in[conv_end:]),
                                     w_pa.astype(BF16), w_pc.astype(BF16), w_o.astype(BF16),
                                     row2(g_ffn), wrt, brt)

    n_tiles = t // TM
    cnt = cnt_f[:, 0, ROUTE_OFF:ROUTE_OFF + N_EXPERTS].astype(jnp.int32)
    tile_off = jnp.cumsum(cnt, axis=1) - cnt
    tot = cnt.sum(axis=0)
    pcounts = (tot + ROW_BLK - 1) // ROW_BLK * ROW_BLK
    pends = jnp.cumsum(pcounts)
    pstarts = pends - pcounts
    cum = jnp.cumsum(cnt, axis=0) - cnt
    dst = pstarts[None, :] + cum
    n_blocks = (t * TOP_K) // ROW_BLK + N_EXPERTS
    blk_start = jnp.arange(n_blocks, dtype=jnp.int32) * ROW_BLK
    blk_e = jnp.minimum((pends[None, :] <= blk_start[:, None]).sum(axis=1), N_EXPERTS - 1).astype(jnp.int32)
    nblk = (pends[-1:] // ROW_BLK).astype(jnp.int32)
    sel = (jnp.arange(N_EXPERTS, dtype=jnp.int32)[:, None] == blk_e[None, :]).astype(jnp.int32)
    of_block = lambda a: (a[..., None] * sel).sum(axis=-2)
    base = blk_start - of_block(pstarts)
    nvalid = jnp.clip(of_block(tot) - base, 0, ROW_BLK)
    cum_e = of_block(cum)
    cnt_e = of_block(cnt)
    jlo = (cum_e + cnt_e <= base[None, :]).sum(axis=0)
    jhi = (cum_e < base[None, :] + ROW_BLK).sum(axis=0)
    part = jnp.clip(jnp.minimum(cum_e + cnt_e, base[None, :] + ROW_BLK) - jnp.maximum(cum_e, base[None, :]),
                    0, ROW_BLK)
    blk_pieces = jnp.stack(_piece_counts(part), axis=-1).sum(axis=0)
    tile_pieces = jnp.stack(_piece_counts(cnt), axis=-1).sum(axis=1)
    src = jnp.arange(n_tiles, dtype=jnp.int32)[:, None] * TILE_ROWS + tile_off - cum
    flat = lambda a: a.reshape(-1).astype(jnp.int32)

    blk_big = (part >= BIG_PIECE).any(axis=0)
    tile_big = (cnt >= BIG_PIECE).any(axis=1)

    ys = _experts(blk_e, nblk, flat(base), flat(jlo), flat(jhi), flat(nvalid), flat(blk_big), flat(blk_pieces),
                  flat(cum), flat(cum + cnt), flat(src), stage, w1, w3, w2)
    out = _combine(flat(cnt), flat(tile_off), flat(dst), flat(tile_big), flat(tile_pieces), h1, route,
                   p_i.reshape(t, PLE_DIM), row2(g_ple), w_ple_gate.astype(BF16), row2(b_ple_gate),
                   w_ple_proj.astype(BF16), ys)
    return out.reshape(b, s, d)


def kernel(x, p, g_mix, w_in, b_in, g_q, g_k, rel_bias, conv_w, conv_b, w_pa, w_pc, w_o, g_ffn, w_group, b_group, w_router, b_router, w1, w3, w2, g_ple, w_ple_gate, b_ple_gate, w_ple_proj):
    h = x
    for i in range(p.shape[0]):
        h = _layer(h, p[i], g_mix[i], w_in[i], b_in[i], g_q[i], g_k[i], rel_bias[i], conv_w[i], conv_b[i],
                   w_pa[i], w_pc[i], w_o[i], g_ffn[i], w_group[i], b_group[i], w_router[i], b_router[i],
                   w1[i], w3[i], w2[i], g_ple[i], w_ple_gate[i], b_ple_gate[i], w_ple_proj[i])
    return h
```

```python
import functools

import jax
import jax.numpy as jnp
from jax import lax
from jax.experimental import pallas as pl
from jax.experimental.pallas import tpu as pltpu

D_MODEL = 1024
CHUNK = 64
LEFT_CHUNKS = 8
N_HEADS = 8
HEAD_DIM = 64
ATTN_W = N_HEADS * HEAD_DIM
CONV_W = D_MODEL // 2
CONV_K = 3
MAX_REL_PAST = 256
PLE_DIM = 256
N_GROUPS = 4
EXPERTS_PER_GROUP = 8
N_EXPERTS = N_GROUPS * EXPERTS_PER_GROUP
TOP_K = 2
D_EXPERT = 512
EPS = 1e-6
NEG = -1e30
LOG2E = 1.4426950408889634

LANES = 128
SUBLANES = 8
TM = 256
TQ = 256
KV_SLABS = 1 + (LEFT_CHUNKS * CHUNK) // TQ
ROW_BLK = 512
TILE_ROWS = TOP_K * TM
PACK_ROWS = D_MODEL // (2 * LANES)
PACK_W = 2 * LANES
SEG_SIZES = tuple(TM >> k for k in range(TM.bit_length()))
BIG_PIECE = 32
SCORE_AHEAD = 2
GATHER_AHEAD = 2
GATHER_UNROLL = 36
VMEM_LIMIT = 56 * 1024 * 1024

F32 = jnp.float32
BF16 = jnp.bfloat16
U32 = jnp.uint32


def _dot(a, b):
    return jnp.dot(a, b, preferred_element_type=F32)


def _rms(x, g):
    ms = jnp.mean(x * x, axis=-1, keepdims=True)
    return (x * lax.rsqrt(ms + EPS)) * g


def _sigmoid(x):
    return 1.0 / (1.0 + jnp.exp(-x))


def _pack_block(ref, vals, n_rows, a, is_bf16=False):
    lo = vals[:, 0:LANES]
    hi = vals[:, LANES:PACK_W]
    if not is_bf16:
        lo = lo.astype(BF16).astype(F32)
        hi = hi.astype(BF16).astype(F32)
    ref[pl.ds(a, n_rows, stride=PACK_ROWS), :] = (
        lax.bitcast_convert_type(hi, U32) | (lax.bitcast_convert_type(lo, U32) >> 16))


def _pack_rows(ref, vals, n_rows, is_bf16=False):
    for a in range(PACK_ROWS):
        _pack_block(ref, vals[:, a * PACK_W:(a + 1) * PACK_W], n_rows, a, is_bf16)


def _unpack_block(ref, n_rows, a, n_valid=None):
    word = ref[pl.ds(a, n_rows, stride=PACK_ROWS), :]
    if n_valid is not None:
        word = jnp.where(lax.broadcasted_iota(jnp.int32, (n_rows, LANES), 0) < n_valid, word, U32(0))
    lo = lax.bitcast_convert_type(word << 16, F32).astype(BF16)
    hi = lax.bitcast_convert_type(word & U32(0xFFFF0000), F32).astype(BF16)
    return jnp.concatenate([lo, hi], axis=1)


def _split(items, n_groups):
    items = list(items)
    return [items[len(items) * g // n_groups:len(items) * (g + 1) // n_groups] for g in range(n_groups)]


def _inproj_kernel(x_ref, g_ref, wqkv_ref, wconv_ref, b_ref, gq_ref, gk_ref, hm_ref,
                   cw_ref, cb_ref, q_ref, k_ref, v_ref, yc_ref, carry_ref, *, tiles_per_seq):
    i = pl.program_id(0)
    nb = _rms(x_ref[...], g_ref[...]).astype(BF16)

    zq = _dot(nb, wqkv_ref[...]) + b_ref[:, 0:3 * ATTN_W]
    hm = hm_ref[...]

    def head_rms(t, g):
        ms = _dot((t * t).astype(BF16), hm)
        return (t * lax.rsqrt(ms + EPS)) * g

    q_ref[...] = head_rms(zq[:, 0:ATTN_W], gq_ref[...]).astype(BF16)
    k_ref[...] = head_rms(zq[:, ATTN_W:2 * ATTN_W], gk_ref[...]).astype(BF16)
    v_ref[...] = zq[:, 2 * ATTN_W:3 * ATTN_W].astype(BF16)

    zc = _dot(nb, wconv_ref[...]) + b_ref[:, 3 * ATTN_W:3 * ATTN_W + 3 * CONV_W]
    u = zc[:, 0:CONV_W]
    bg = zc[:, CONV_W:2 * CONV_W]
    cg = zc[:, 2 * CONV_W:3 * CONV_W]
    cu = cg * u

    @pl.when((i % tiles_per_seq) == 0)
    def _():
        carry_ref[...] = jnp.zeros_like(carry_ref)

    prev = carry_ref[...]
    carry_ref[...] = cu[TM - SUBLANES:TM, :]
    row = lax.broadcasted_iota(jnp.int32, (SUBLANES, CONV_W), 0)

    def shifted(s):
        r = pltpu.roll(cu, s, 0)
        p = pltpu.roll(prev, s, 0)
        top = jnp.where(row < s, p, r[0:SUBLANES, :])
        return jnp.concatenate([top, r[SUBLANES:, :]], axis=0)

    y = cb_ref[...] + cw_ref[0:1, :] * shifted(2)
    y = y + cw_ref[1:2, :] * shifted(1)
    y = y + cw_ref[2:3, :] * cu
    yc_ref[...] = (bg * y).astype(BF16)


def _inproj(x2, g_mix, wqkv, wconv, b_in, gq, gk, hmat, cw, cb, seq):
    t = x2.shape[0]
    const = lambda i: (0, 0)
    row = lambda i: (i, 0)
    out = jax.ShapeDtypeStruct((t, ATTN_W), BF16)
    return pl.pallas_call(
        functools.partial(_inproj_kernel, tiles_per_seq=seq // TM),
        grid=(t // TM,),
        in_specs=[
            pl.BlockSpec((TM, D_MODEL), row),
            pl.BlockSpec((1, D_MODEL), const),
            pl.BlockSpec((D_MODEL, 3 * ATTN_W), const),
            pl.BlockSpec((D_MODEL, 3 * CONV_W), const),
            pl.BlockSpec((1, 3 * ATTN_W + 3 * CONV_W), const),
            pl.BlockSpec((1, ATTN_W), const),
            pl.BlockSpec((1, ATTN_W), const),
            pl.BlockSpec((ATTN_W, ATTN_W), const),
            pl.BlockSpec((SUBLANES, CONV_W), const),
            pl.BlockSpec((1, CONV_W), const),
        ],
        out_specs=[pl.BlockSpec((TM, ATTN_W), row)] * 4,
        out_shape=[out] * 4,
        scratch_shapes=[pltpu.VMEM((SUBLANES, CONV_W), F32)],
        compiler_params=pltpu.CompilerParams(
            dimension_semantics=("arbitrary",), vmem_limit_bytes=VMEM_LIMIT),
        name="inproj",
    )(x2, g_mix, wqkv, wconv, b_in, gq, gk, hmat, cw, cb)


def _lane_fold(parts, op):
    acc = None
    for a in parts:
        for c in range(0, a.shape[1], LANES):
            piece = a[:, c:c + LANES]
            acc = piece if acc is None else op(acc, piece)
    return acc


def _attn_kernel(q_ref, k0_ref, k1_ref, k2_ref, v0_ref, v1_ref, v2_ref, bias_ref, o_ref):
    k_refs = (k0_ref, k1_ref, k2_ref)
    v_refs = (v0_ref, v1_ref, v2_ref)
    pair_w = 2 * HEAD_DIM
    lane = lax.broadcasted_iota(jnp.int32, (TQ, pair_w), 1)
    low = lane < HEAD_DIM

    def scores(h, pens):
        ps = slice((h // 2) * pair_w, (h // 2 + 1) * pair_w)
        q_pair = q_ref[0, :, ps]
        own = low if h % 2 == 0 else jnp.logical_not(low)
        qh = jnp.where(own, q_pair, jnp.zeros_like(q_pair))
        s = [lax.dot_general(qh, k_refs[j][0, :, ps], (((1,), (1,)), ((), ())),
                             preferred_element_type=F32) + bias_ref[h, :, j * TQ:(j + 1) * TQ]
             for j in range(KV_SLABS)]
        return s if pens is None else [sj + pens[j] for j, sj in enumerate(s)]

    def weighted(h, s):
        ps = slice((h // 2) * pair_w, (h // 2 + 1) * pair_w)
        m = _lane_fold(s, jnp.maximum).max(axis=-1, keepdims=True)
        e = [jnp.exp2(sj - m) for sj in s]
        l = _lane_fold(e, jnp.add).sum(axis=-1, keepdims=True)
        acc = None
        for j in range(KV_SLABS):
            oj = _dot(e[j].astype(BF16), v_refs[j][0, :, ps])
            acc = oj if acc is None else acc + oj
        return acc * (1.0 / l)

    def all_heads(pens):
        pending = [scores(h, pens) for h in range(SCORE_AHEAD)]
        o_even = None
        for h in range(N_HEADS):
            if h + SCORE_AHEAD < N_HEADS:
                pending.append(scores(h + SCORE_AHEAD, pens))
            o = weighted(h, pending.pop(0))
            if h % 2 == 0:
                o_even = o
            else:
                ps = slice((h // 2) * pair_w, (h // 2 + 1) * pair_w)
                o_ref[0, :, ps] = jnp.where(low, o_even, o).astype(BF16)

    i = pl.program_id(1)

    @pl.when(i >= KV_SLABS - 1)
    def _():
        all_heads(None)

    @pl.when(i < KV_SLABS - 1)
    def _():
        all_heads([jnp.where(i >= KV_SLABS - 1 - j, 0.0, NEG).astype(F32) for j in range(KV_SLABS)])


def _attention(q, k, v, bias):
    b, s, _ = q.shape
    blk = (1, TQ, ATTN_W)

    def kv_map(j):
        back = KV_SLABS - 1 - j
        return lambda bi, i: (bi, jnp.maximum(i - back, 0), 0)

    kv_specs = [pl.BlockSpec(blk, kv_map(j)) for j in range(KV_SLABS)]
    return pl.pallas_call(
        _attn_kernel,
        grid=(b, s // TQ),
        in_specs=[pl.BlockSpec(blk, lambda bi, i: (bi, i, 0))] + kv_specs + kv_specs + [
            pl.BlockSpec((N_HEADS, TQ, KV_SLABS * TQ), lambda bi, i: (0, 0, 0))],
        out_specs=pl.BlockSpec(blk, lambda bi, i: (bi, i, 0)),
        out_shape=jax.ShapeDtypeStruct((b, s, ATTN_W), BF16),
        compiler_params=pltpu.CompilerParams(
            dimension_semantics=("arbitrary", "arbitrary"), vmem_limit_bytes=VMEM_LIMIT),
        name="attn",
    )(q, k, k, k, v, v, v, bias)


def _attn_bias(rel_bias):
    nk = KV_SLABS * TQ
    past = nk - TQ
    d = jnp.arange(TQ - 1 + past, -TQ, -1)
    idx = jnp.clip(d, -(CHUNK - 1), MAX_REL_PAST) + (CHUNK - 1)
    onehot = (idx[:, None] == jnp.arange(rel_bias.shape[1])[None, :]).astype(F32)
    per_dist = jnp.einsum("dn,hn->hd", onehot, rel_bias.astype(F32) * LOG2E,
                          precision=lax.Precision.HIGHEST)
    n_h, span = per_dist.shape
    padded = jnp.pad(per_dist, ((0, 0), (0, 2)))
    skew = jnp.tile(padded, (1, TQ))[:, :TQ * (span + 1)].reshape(n_h, TQ, span + 1)
    table = skew[:, :, TQ - 1:TQ - 1 + nk]
    r = jnp.arange(TQ)[:, None]
    c = jnp.arange(nk)[None, :]
    qc = r // CHUNK
    kc = c // CHUNK
    lead = past // CHUNK - LEFT_CHUNKS
    band = (kc >= qc + lead) & (kc <= qc + lead + LEFT_CHUNKS)
    return jnp.where(band[None], table, NEG)


def _merge_kernel(x_ref, ya_ref, yc_ref, g_ref, wg_ref, bgate_ref, wpa_ref, wpc_ref, wo_ref,
                  gffn_ref, wrt_ref, brt_ref, h_ref, stage_ref, route_ref, cnt_ref, n2_scr, logit_scr):
    @pl.when(pl.program_id(0) == 0)
    def _():
        n2_scr[...] = jnp.zeros_like(n2_scr)
        logit_scr[...] = jnp.zeros_like(logit_scr)

    n2 = n2_scr[...]
    lt = logit_scr[...].T
    row8 = lax.broadcasted_iota(jnp.int32, (SUBLANES, TM), 0).astype(F32)
    ninf = -jnp.inf

    def argmax_first(vals):
        mx = vals.max(axis=0, keepdims=True)
        idx = jnp.where(vals == mx, row8, float(SUBLANES)).min(axis=0, keepdims=True)
        return mx, idx

    gl = jnp.where(row8 < N_GROUPS, lt[N_EXPERTS:N_EXPERTS + SUBLANES, :], ninf)
    gmax, grp = argmax_first(gl)
    p_grp = 1.0 / jnp.exp(gl - gmax).sum(axis=0, keepdims=True)
    el = lt[0:EXPERTS_PER_GROUP, :]
    for g in range(1, N_GROUPS):
        el = jnp.where(grp == g, lt[g * EXPERTS_PER_GROUP:(g + 1) * EXPERTS_PER_GROUP, :], el)
    l1, i1 = argmax_first(el)
    l2, i2 = argmax_first(jnp.where(row8 == i1, ninf, el))
    e2 = jnp.exp(l2 - l1)
    den = 1.0 + e2
    w1 = p_grp * (1.0 / den)
    w2 = p_grp * (e2 / den)
    x1 = grp * EXPERTS_PER_GROUP + i1
    x2 = grp * EXPERTS_PER_GROUP + i2

    row_e = lax.broadcasted_iota(jnp.int32, (N_EXPERTS, TM), 0).astype(F32)
    oh1 = (row_e == x1).astype(F32)
    oh2 = (row_e == x2).astype(F32)
    oh = (oh1 + oh2).astype(BF16)
    r = lax.broadcasted_iota(jnp.int32, (TM, TM), 0)
    c = lax.broadcasted_iota(jnp.int32, (TM, TM), 1)
    earlier_tok = _dot(oh, (r < c).astype(BF16))
    er = lax.broadcasted_iota(jnp.int32, (N_EXPERTS, N_EXPERTS), 0)
    ec = lax.broadcasted_iota(jnp.int32, (N_EXPERTS, N_EXPERTS), 1)
    lower_exp = _dot((ec < er).astype(BF16), oh).sum(axis=1, keepdims=True)
    where = earlier_tok + lower_exp
    pos1 = (oh1 * where).sum(axis=0, keepdims=True)
    pos2 = (oh2 * where).sum(axis=0, keepdims=True)
    counts = lax.dot_general(jnp.ones((SUBLANES, TM), BF16), oh, (((1,), (1,)), ((), ())),
                             preferred_element_type=F32)
    cnt_ref[0] = jnp.concatenate([counts, jnp.zeros((SUBLANES, LANES - N_EXPERTS), F32)], axis=1)

    route_t = jnp.zeros((SUBLANES, TM), F32)
    for j, val in enumerate((x1, x2, w1, w2, pos1, pos2)):
        route_t = jnp.where(row8 == j, val, route_t)
    route_ref[...] = jnp.concatenate([route_t, jnp.zeros((LANES - SUBLANES, TM), F32)], axis=0).T

    x = x_ref[...]
    nb = _rms(x, g_ref[...]).astype(BF16)
    sga = _sigmoid(_dot(nb, wg_ref[:, 0:D_MODEL]) + bgate_ref[:, 0:D_MODEL])
    ma = sga * _dot(ya_ref[...], wpa_ref[...])
    sgc = _sigmoid(_dot(nb, wg_ref[:, D_MODEL:2 * D_MODEL]) + bgate_ref[:, D_MODEL:2 * D_MODEL])
    m = ma + sgc * _dot(yc_ref[...], wpc_ref[...])
    h = x + _dot(m.astype(BF16), wo_ref[...])
    h_ref[...] = h
    n2_new = _rms(h, gffn_ref[...]).astype(BF16)
    n2_scr[...] = n2_new
    logit_scr[...] = _dot(n2_new, wrt_ref[...]) + brt_ref[...]

    slot = lax.broadcasted_iota(jnp.int32, (TILE_ROWS, TM), 0).astype(F32)
    place = ((slot == pos1) | (slot == pos2)).astype(BF16)
    _pack_rows(stage_ref, _dot(place, n2), TILE_ROWS, is_bf16=True)


def _merge(x2, ya, yc, g_mix, wgate, bgate, wpa, wpc, wo, gffn, wrt, brt):
    t = x2.shape[0]
    n_tiles = t // TM
    const = lambda i: (0, 0)
    row = lambda i: (jnp.minimum(i, n_tiles - 1), 0)
    late = lambda i: (jnp.maximum(i - 1, 0), 0)
    return pl.pallas_call(
        _merge_kernel,
        grid=(n_tiles + 1,),
        in_specs=[
            pl.BlockSpec((TM, D_MODEL), row),
            pl.BlockSpec((TM, ATTN_W), row),
            pl.BlockSpec((TM, CONV_W), row),
            pl.BlockSpec((1, D_MODEL), const),
            pl.BlockSpec((D_MODEL, 2 * D_MODEL), const),
            pl.BlockSpec((1, 2 * D_MODEL), const),
            pl.BlockSpec((ATTN_W, D_MODEL), const),
            pl.BlockSpec((CONV_W, D_MODEL), const),
            pl.BlockSpec((D_MODEL, D_MODEL), const),
            pl.BlockSpec((1, D_MODEL), const),
            pl.BlockSpec((D_MODEL, LANES), const),
            pl.BlockSpec((1, LANES), const),
        ],
        out_specs=[
            pl.BlockSpec((TM, D_MODEL), row),
            pl.BlockSpec((TILE_ROWS * PACK_ROWS, LANES), late),
            pl.BlockSpec((TM, LANES), late),
            pl.BlockSpec((1, SUBLANES, LANES), lambda i: (jnp.maximum(i - 1, 0), 0, 0)),
        ],
        out_shape=[
            jax.ShapeDtypeStruct((t, D_MODEL), F32),
            jax.ShapeDtypeStruct((n_tiles * TILE_ROWS * PACK_ROWS, LANES), U32),
            jax.ShapeDtypeStruct((t, LANES), F32),
            jax.ShapeDtypeStruct((n_tiles, SUBLANES, LANES), F32),
        ],
        scratch_shapes=[pltpu.VMEM((TM, D_MODEL), BF16), pltpu.VMEM((TM, LANES), F32)],
        compiler_params=pltpu.CompilerParams(
            dimension_semantics=("arbitrary",), vmem_limit_bytes=VMEM_LIMIT),
        name="merge",
    )(x2, ya, yc, g_mix, wgate, bgate, wpa, wpc, wo, gffn, wrt, brt)


def _piece_counts(n):
    return [(n >> (size.bit_length() - 1)) & 1 for size in SEG_SIZES]


ALL_CLASSES = tuple(range(len(SEG_SIZES)))
BIG_CLASSES = tuple(c for c in ALL_CLASSES if SEG_SIZES[c] >= BIG_PIECE)
SMALL_CLASSES = tuple(c for c in ALL_CLASSES if SEG_SIZES[c] < BIG_PIECE)


def _segment_pieces(n, visit, classes=ALL_CLASSES):
    for cls in classes:
        size = SEG_SIZES[cls]

        @pl.when((n & size) != 0)
        def _(cls=cls, size=size):
            visit(cls, n & ~(2 * size - 1))


def _piece_copy(src_ref, dst_ref, sems, cls, src_row, dst_row):
    n = SEG_SIZES[cls] * PACK_ROWS
    return pltpu.make_async_copy(src_ref.at[pl.ds(src_row * PACK_ROWS, n), :],
                                 dst_ref.at[pl.ds(dst_row * PACK_ROWS, n), :], sems.at[cls])


def _drain(src_ref, dst_ref, sems, counts, has_big):
    unroll = 4

    def wait_classes(classes):
        for cls in classes:
            def wait_some(k, cls=cls):
                def body(t, carry):
                    for _ in range(k):
                        _piece_copy(src_ref, dst_ref, sems, cls, 0, 0).wait()
                    return carry
                return body

            n = counts[cls]
            lax.fori_loop(0, n >> 2, wait_some(unroll), 0)
            lax.fori_loop(0, n & (unroll - 1), wait_some(1), 0)

    @pl.when(has_big)
    def _():
        wait_classes(BIG_CLASSES)

    wait_classes(SMALL_CLASSES)


def _experts_kernel(be_ref, nb_ref, base_ref, jlo_ref, jhi_ref, nv_ref, big_ref, npiece_ref, cum_ref, end_ref,
                    src_ref, stage_hbm, w1_ref, w3_ref, w2_ref, ys_ref, *scratch, n_tiles, n_blocks):
    xbufs = scratch[:GATHER_AHEAD + 1]
    w1b_ref, w3b_ref, w2b_ref, sems = scratch[GATHER_AHEAD + 1:]
    s = pl.program_id(0)
    nb = nb_ref[0]
    n_cls = len(SEG_SIZES)

    def segment_copies(step, j, live, buf, sem, classes=ALL_CLASSES):
        base = base_ref[step]
        g = jnp.minimum(j, n_tiles - 1) * N_EXPERTS + be_ref[step]
        lo = jnp.maximum(cum_ref[g], base)
        hi = jnp.minimum(end_ref[g], base + ROW_BLK)
        src = src_ref[g] + lo
        dst = lo - base
        _segment_pieces(jnp.where(live, jnp.maximum(hi - lo, 0), 0),
                        lambda cls, o: _piece_copy(stage_hbm, buf, sem, cls, src + o, dst + o).start(), classes)

    def looped_copies(step, j0, j1, buf, sem, classes=ALL_CLASSES):
        def body(j, carry):
            segment_copies(step, j, True, buf, sem, classes)
            return carry

        lax.fori_loop(j0, j1, body, 0)

    @pl.when(s == 0)
    def _():
        for buf in xbufs:
            buf[...] = jnp.zeros_like(buf)
        for first in range(GATHER_AHEAD):
            blk = min(first, n_blocks - 1)
            looped_copies(blk, jlo_ref[blk], jnp.where(first < nb, jhi_ref[blk], jlo_ref[blk]),
                          xbufs[first], sems.at[first])

    def step(cur, cur_sem, nxt, nxt_sem):
        prev = be_ref[jnp.maximum(s - 1, 0)]

        @pl.when((s == 0) | (be_ref[s] != prev))
        def _():
            w1b_ref[...] = w1_ref[0].astype(BF16)
            w3b_ref[...] = w3_ref[0].astype(BF16)
            w2b_ref[...] = w2_ref[0].astype(BF16)

        _drain(stage_hbm, cur, cur_sem, [npiece_ref[s * n_cls + c] for c in range(n_cls)], big_ref[s] != 0)

        nxt_step = jnp.minimum(s + GATHER_AHEAD, n_blocks - 1)
        live = s + GATHER_AHEAD < nb
        j0 = jlo_ref[nxt_step]
        j1 = jnp.where(live, jhi_ref[nxt_step], j0)
        looped_copies(nxt_step, j0 + GATHER_UNROLL, j1, nxt, nxt_sem, SMALL_CLASSES)
        looped_copies(nxt_step, j0, jnp.where(big_ref[nxt_step] != 0, j1, j0), nxt, nxt_sem, BIG_CLASSES)
        groups = iter(_split(range(GATHER_UNROLL), 2 * PACK_ROWS))

        def start_group():
            for k in next(groups):
                segment_copies(nxt_step, j0 + k, j0 + k < j1, nxt, nxt_sem, SMALL_CLASSES)

        a = None
        g = None
        for blk in range(PACK_ROWS):
            start_group()
            xa = _unpack_block(cur, ROW_BLK, blk, n_valid=nv_ref[s])
            rows = slice(blk * PACK_W, (blk + 1) * PACK_W)
            da = _dot(xa, w1b_ref[rows, :])
            dg = _dot(xa, w3b_ref[rows, :])
            a = da if a is None else a + da
            g = dg if g is None else g + dg
        hdn = ((a * _sigmoid(a)) * g).astype(BF16)
        for blk in range(PACK_ROWS):
            start_group()
            _pack_block(ys_ref, _dot(hdn, w2b_ref[:, blk * PACK_W:(blk + 1) * PACK_W]), ROW_BLK, blk)

    n_buf = len(xbufs)
    for slot in range(n_buf):
        @pl.when((s < nb) & (s % n_buf == slot))
        def _(slot=slot):
            ahead = (slot + GATHER_AHEAD) % n_buf
            step(xbufs[slot], sems.at[slot], xbufs[ahead], sems.at[ahead])

    @pl.when(s >= nb)
    def _():
        ys_ref[...] = jnp.zeros_like(ys_ref)


def _experts(blk_e, nblk, base, jlo, jhi, nvalid, big, npiece, cum, end, src, stage, w1, w3, w2):
    n_blocks = blk_e.shape[0]
    n_tiles = cum.shape[0] // N_EXPERTS

    def wsel(s, be, nb, *_):
        return (be[jnp.minimum(s, nb[0] - 1)], 0, 0)

    grid_spec = pltpu.PrefetchScalarGridSpec(
        num_scalar_prefetch=11,
        grid=(n_blocks,),
        in_specs=[
            pl.BlockSpec(memory_space=pl.ANY),
            pl.BlockSpec((1, D_MODEL, D_EXPERT), wsel),
            pl.BlockSpec((1, D_MODEL, D_EXPERT), wsel),
            pl.BlockSpec((1, D_EXPERT, D_MODEL), wsel),
        ],
        out_specs=pl.BlockSpec((ROW_BLK * PACK_ROWS, LANES), lambda s, *_: (s, 0)),
        scratch_shapes=[pltpu.VMEM((ROW_BLK * PACK_ROWS, LANES), U32)] * (GATHER_AHEAD + 1) + [
            pltpu.VMEM((D_MODEL, D_EXPERT), BF16),
            pltpu.VMEM((D_MODEL, D_EXPERT), BF16),
            pltpu.VMEM((D_EXPERT, D_MODEL), BF16),
            pltpu.SemaphoreType.DMA((GATHER_AHEAD + 1, len(SEG_SIZES))),
        ],
    )
    return pl.pallas_call(
        functools.partial(_experts_kernel, n_tiles=n_tiles, n_blocks=n_blocks),
        grid_spec=grid_spec,
        out_shape=jax.ShapeDtypeStruct((n_blocks * ROW_BLK * PACK_ROWS, LANES), U32),
        compiler_params=pltpu.CompilerParams(
            dimension_semantics=("arbitrary",), vmem_limit_bytes=VMEM_LIMIT),
        name="experts",
    )(blk_e, nblk, base, jlo, jhi, nvalid, big, npiece, cum, end, src, stage, w1, w3, w2)


def _combine_kernel(cnt_ref, off_ref, dst_ref, big_ref, npiece_ref, h_ref, route_ref, p_ref, gple_ref, wpg_ref,
                    bpg_ref, wpp_ref, ys_hbm, o_ref, *scratch, n_steps):
    ybufs, sems = scratch[:-1], scratch[-1]
    i = pl.program_id(0)
    n_cls = len(SEG_SIZES)

    def segment_copies(step, e, live, buf, sem, classes=ALL_CLASSES):
        g = step * N_EXPERTS + e
        off, dst = off_ref[g], dst_ref[g]
        _segment_pieces(jnp.where(live, cnt_ref[g], 0),
                        lambda cls, o: _piece_copy(ys_hbm, buf, sem, cls, dst + o, off + o).start(), classes)

    def looped_copies(step, n_experts, buf, sem, classes=ALL_CLASSES):
        def body(e, carry):
            segment_copies(step, e, True, buf, sem, classes)
            return carry

        lax.fori_loop(0, n_experts, body, 0)

    @pl.when(i == 0)
    def _():
        for first in range(min(GATHER_AHEAD, n_steps)):
            looped_copies(first, N_EXPERTS, ybufs[first], sems.at[first])

    def step(cur, cur_sem, nxt, nxt_sem):
        _drain(ys_hbm, cur, cur_sem, [npiece_ref[i * n_cls + c] for c in range(n_cls)], big_ref[i] != 0)
        nxt_step = jnp.minimum(i + GATHER_AHEAD, n_steps - 1)
        live = i + GATHER_AHEAD < n_steps
        looped_copies(nxt_step, jnp.where(live & (big_ref[nxt_step] != 0), N_EXPERTS, 0), nxt, nxt_sem, BIG_CLASSES)
        groups = iter(_split(range(N_EXPERTS), PACK_ROWS + 2))

        def start_group():
            for e in next(groups):
                segment_copies(nxt_step, e, live, nxt, nxt_sem, SMALL_CLASSES)

        start_group()
        pp = _dot(p_ref[...].astype(BF16), wpp_ref[...])
        route = route_ref[...]
        place = lax.broadcasted_iota(jnp.int32, (TM, TILE_ROWS), 1).astype(F32)
        sel = [(place == route[:, 4 + kk:5 + kk]).astype(BF16) for kk in range(TOP_K)]
        moe = []
        for blk in range(PACK_ROWS):
            start_group()
            cols = _unpack_block(cur, TILE_ROWS, blk)
            moe.append(_dot(sel[0], cols) * route[:, 2:3] + _dot(sel[1], cols) * route[:, 3:4])
        start_group()
        h = h_ref[...] + jnp.concatenate(moe, axis=1)
        gate = _sigmoid(_dot(_rms(h, gple_ref[...]).astype(BF16), wpg_ref[...]) + bpg_ref[...])
        o_ref[...] = h + gate * pp

    n_buf = len(ybufs)
    for slot in range(n_buf):
        @pl.when(i % n_buf == slot)
        def _(slot=slot):
            ahead = (slot + GATHER_AHEAD) % n_buf
            step(ybufs[slot], sems.at[slot], ybufs[ahead], sems.at[ahead])


def _combine(cnt, off, dst, big, npiece, h1, route, p2, gple, wpg, bpg, wpp, ys):
    t = h1.shape[0]
    n_steps = t // TM
    const = lambda i, *_: (0, 0)
    row = lambda i, *_: (i, 0)
    grid_spec = pltpu.PrefetchScalarGridSpec(
        num_scalar_prefetch=5,
        grid=(n_steps,),
        in_specs=[
            pl.BlockSpec((TM, D_MODEL), row),
            pl.BlockSpec((TM, LANES), row),
            pl.BlockSpec((TM, PLE_DIM), row),
            pl.BlockSpec((1, D_MODEL), const),
            pl.BlockSpec((D_MODEL, D_MODEL), const),
            pl.BlockSpec((1, D_MODEL), const),
            pl.BlockSpec((PLE_DIM, D_MODEL), const),
            pl.BlockSpec(memory_space=pl.ANY),
        ],
        out_specs=pl.BlockSpec((TM, D_MODEL), row),
        scratch_shapes=[pltpu.VMEM((TILE_ROWS * PACK_ROWS, LANES), U32)] * (GATHER_AHEAD + 1) + [
            pltpu.SemaphoreType.DMA((GATHER_AHEAD + 1, len(SEG_SIZES))),
        ],
    )
    return pl.pallas_call(
        functools.partial(_combine_kernel, n_steps=n_steps),
        grid_spec=grid_spec,
        out_shape=jax.ShapeDtypeStruct((t, D_MODEL), F32),
        compiler_params=pltpu.CompilerParams(
            dimension_semantics=("arbitrary",), vmem_limit_bytes=VMEM_LIMIT),
        name="combine",
    )(cnt, off, dst, big, npiece, h1, route, p2, gple, wpg, bpg, wpp, ys)


def _layer(h, p_i, g_mix, w_in, b_in, g_q, g_k, rel_bias, conv_w, conv_b, w_pa, w_pc, w_o,
           g_ffn, w_group, b_group, w_router, b_router, w1, w3, w2,
           g_ple, w_ple_gate, b_ple_gate, w_ple_proj):
    b, s, d = h.shape
    t = b * s
    x2 = h.reshape(t, d)
    row2 = lambda a: a.reshape(1, -1).astype(F32)

    qkv_w = 3 * ATTN_W
    conv_end = qkv_w + 3 * CONV_W
    w_in_b = w_in.astype(BF16)
    gq = row2(jnp.tile(g_q.astype(F32) * (HEAD_DIM ** -0.5 * LOG2E), N_HEADS))
    gk = row2(jnp.tile(g_k.astype(F32), N_HEADS))
    head = jnp.arange(ATTN_W) // HEAD_DIM
    hmat = jnp.where(head[:, None] == head[None, :], 1.0 / HEAD_DIM, 0.0).astype(BF16)
    cw = jnp.concatenate([conv_w.astype(F32), jnp.zeros((SUBLANES - CONV_K, CONV_W), F32)], axis=0)

    q, k, v, yc = _inproj(x2, row2(g_mix), w_in_b[:, :qkv_w], w_in_b[:, qkv_w:conv_end],
                          row2(b_in[:conv_end]), gq, gk, hmat, cw, row2(conv_b), s)

    ya = _attention(q.reshape(b, s, ATTN_W), k.reshape(b, s, ATTN_W), v.reshape(b, s, ATTN_W),
                    _attn_bias(rel_bias)).reshape(t, ATTN_W)

    n_pad = LANES - N_GROUPS - N_EXPERTS
    wrt = jnp.concatenate([w_router, w_group, jnp.zeros((d, n_pad), w_group.dtype)], axis=1).astype(BF16)
    brt = row2(jnp.concatenate([b_router, b_group, jnp.zeros((n_pad,), b_group.dtype)]))
    h1, stage, route, cnt_f = _merge(x2, ya, yc, row2(g_mix), w_in_b[:, conv_end:], row2(b_in[conv_end:]),
                                     w_pa.astype(BF16), w_pc.astype(BF16), w_o.astype(BF16),
                                     row2(g_ffn), wrt, brt)

    n_tiles = t // TM
    cnt = cnt_f[:, 0, :N_EXPERTS].astype(jnp.int32)
    tile_off = jnp.cumsum(cnt, axis=1) - cnt
    tot = cnt.sum(axis=0)
    pcounts = (tot + ROW_BLK - 1) // ROW_BLK * ROW_BLK
    pends = jnp.cumsum(pcounts)
    pstarts = pends - pcounts
    cum = jnp.cumsum(cnt, axis=0) - cnt
    dst = pstarts[None, :] + cum
    n_blocks = (t * TOP_K) // ROW_BLK + N_EXPERTS
    blk_start = jnp.arange(n_blocks, dtype=jnp.int32) * ROW_BLK
    blk_e = jnp.minimum((pends[None, :] <= blk_start[:, None]).sum(axis=1), N_EXPERTS - 1).astype(jnp.int32)
    nblk = (pends[-1:] // ROW_BLK).astype(jnp.int32)
    sel = (jnp.arange(N_EXPERTS, dtype=jnp.int32)[:, None] == blk_e[None, :]).astype(jnp.int32)
    of_block = lambda a: (a[..., None] * sel).sum(axis=-2)
    base = blk_start - of_block(pstarts)
    nvalid = jnp.clip(of_block(tot) - base, 0, ROW_BLK)
    cum_e = of_block(cum)
    cnt_e = of_block(cnt)
    jlo = (cum_e + cnt_e <= base[None, :]).sum(axis=0)
    jhi = (cum_e < base[None, :] + ROW_BLK).sum(axis=0)
    part = jnp.clip(jnp.minimum(cum_e + cnt_e, base[None, :] + ROW_BLK) - jnp.maximum(cum_e, base[None, :]),
                    0, ROW_BLK)
    blk_pieces = jnp.stack(_piece_counts(part), axis=-1).sum(axis=0)
    tile_pieces = jnp.stack(_piece_counts(cnt), axis=-1).sum(axis=1)
    src = jnp.arange(n_tiles, dtype=jnp.int32)[:, None] * TILE_ROWS + tile_off - cum
    flat = lambda a: a.reshape(-1).astype(jnp.int32)

    blk_big = (part >= BIG_PIECE).any(axis=0)
    tile_big = (cnt >= BIG_PIECE).any(axis=1)

    ys = _experts(blk_e, nblk, flat(base), flat(jlo), flat(jhi), flat(nvalid), flat(blk_big), flat(blk_pieces),
                  flat(cum), flat(cum + cnt), flat(src), stage, w1, w3, w2)
    out = _combine(flat(cnt), flat(tile_off), flat(dst), flat(tile_big), flat(tile_pieces), h1, route,
                   p_i.reshape(t, PLE_DIM), row2(g_ple), w_ple_gate.astype(BF16), row2(b_ple_gate),
                   w_ple_proj.astype(BF16), ys)
    return out.reshape(b, s, d)


def kernel(x, p, g_mix, w_in, b_in, g_q, g_k, rel_bias, conv_w, conv_b, w_pa, w_pc, w_o, g_ffn, w_group, b_group, w_router, b_router, w1, w3, w2, g_ple, w_ple_gate, b_ple_gate, w_ple_proj):
    h = x
    for i in range(p.shape[0]):
        h = _layer(h, p[i], g_mix[i], w_in[i], b_in[i], g_q[i], g_k[i], rel_bias[i], conv_w[i], conv_b[i],
                   w_pa[i], w_pc[i], w_o[i], g_ffn[i], w_group[i], b_group[i], w_router[i], b_router[i],
                   w1[i], w3[i], w2[i], g_ple[i], w_ple_gate[i], b_ple_gate[i], w_ple_proj[i])
    return h
```

```python
import functools

import jax
import jax.numpy as jnp
from jax import lax
from jax.experimental import pallas as pl
from jax.experimental.pallas import tpu as pltpu

D_MODEL = 1024
CHUNK = 64
LEFT_CHUNKS = 8
N_HEADS = 8
HEAD_DIM = 64
ATTN_W = N_HEADS * HEAD_DIM
CONV_W = D_MODEL // 2
CONV_K = 3
MAX_REL_PAST = 256
PLE_DIM = 256
N_GROUPS = 4
EXPERTS_PER_GROUP = 8
N_EXPERTS = N_GROUPS * EXPERTS_PER_GROUP
TOP_K = 2
D_EXPERT = 512
EPS = 1e-6
NEG = -1e30
LOG2E = 1.4426950408889634

LANES = 128
SUBLANES = 8
TM = 256
TI = 512
TQ = 256
KV_SLABS = 1 + (LEFT_CHUNKS * CHUNK) // TQ
ROW_BLK = 512
TILE_ROWS = TOP_K * TM
PACK_ROWS = D_MODEL // (2 * LANES)
PACK_W = 2 * LANES
SEG_SIZES = tuple(TM >> k for k in range(TM.bit_length()))
BIG_PIECE = 32
ATTN_BATCH = 2
SCORE_AHEAD = 2
GATHER_AHEAD = 2
GATHER_UNROLL = 36
VMEM_LIMIT = 56 * 1024 * 1024

F32 = jnp.float32
BF16 = jnp.bfloat16
U32 = jnp.uint32


def _dot(a, b):
    return jnp.dot(a, b, preferred_element_type=F32)


def _rms(x, g):
    ms = jnp.mean(x * x, axis=-1, keepdims=True)
    return (x * lax.rsqrt(ms + EPS)) * g


def _sigmoid(x):
    return 1.0 / (1.0 + jnp.exp(-x))


def _pack_block(ref, vals, n_rows, a, is_bf16=False):
    lo = vals[:, 0:LANES]
    hi = vals[:, LANES:PACK_W]
    if not is_bf16:
        lo = lo.astype(BF16).astype(F32)
        hi = hi.astype(BF16).astype(F32)
    ref[pl.ds(a, n_rows, stride=PACK_ROWS), :] = (
        lax.bitcast_convert_type(hi, U32) | (lax.bitcast_convert_type(lo, U32) >> 16))


def _pack_rows(ref, vals, n_rows, is_bf16=False):
    for a in range(PACK_ROWS):
        _pack_block(ref, vals[:, a * PACK_W:(a + 1) * PACK_W], n_rows, a, is_bf16)


def _unpack_block(ref, n_rows, a, n_valid=None):
    word = ref[pl.ds(a, n_rows, stride=PACK_ROWS), :]
    if n_valid is not None:
        word = jnp.where(lax.broadcasted_iota(jnp.int32, (n_rows, LANES), 0) < n_valid, word, U32(0))
    lo = lax.bitcast_convert_type(word << 16, F32).astype(BF16)
    hi = lax.bitcast_convert_type(word & U32(0xFFFF0000), F32).astype(BF16)
    return jnp.concatenate([lo, hi], axis=1)


def _split(items, n_groups):
    items = list(items)
    return [items[len(items) * g // n_groups:len(items) * (g + 1) // n_groups] for g in range(n_groups)]


def _inproj_kernel(x_ref, g_ref, wqkv_ref, wconv_ref, b_ref, gq_ref, gk_ref, hm_ref,
                   cw_ref, cb_ref, q_ref, k_ref, v_ref, yc_ref, carry_ref, *, tiles_per_seq):
    i = pl.program_id(0)
    nb = _rms(x_ref[...], g_ref[...]).astype(BF16)

    zq = _dot(nb, wqkv_ref[...]) + b_ref[:, 0:3 * ATTN_W]
    hm = hm_ref[...]

    def head_rms(t, g):
        ms = _dot((t * t).astype(BF16), hm)
        return (t * lax.rsqrt(ms + EPS)) * g

    q_ref[...] = head_rms(zq[:, 0:ATTN_W], gq_ref[...]).astype(BF16)
    k_ref[...] = head_rms(zq[:, ATTN_W:2 * ATTN_W], gk_ref[...]).astype(BF16)
    v_ref[...] = zq[:, 2 * ATTN_W:3 * ATTN_W].astype(BF16)

    zc = _dot(nb, wconv_ref[...]) + b_ref[:, 3 * ATTN_W:3 * ATTN_W + 3 * CONV_W]
    u = zc[:, 0:CONV_W]
    bg = zc[:, CONV_W:2 * CONV_W]
    cg = zc[:, 2 * CONV_W:3 * CONV_W]
    cu = cg * u

    @pl.when((i % tiles_per_seq) == 0)
    def _():
        carry_ref[...] = jnp.zeros_like(carry_ref)

    prev = carry_ref[...]
    carry_ref[...] = cu[TI - SUBLANES:TI, :]
    row = lax.broadcasted_iota(jnp.int32, (SUBLANES, CONV_W), 0)

    def shifted(s):
        r = pltpu.roll(cu, s, 0)
        p = pltpu.roll(prev, s, 0)
        top = jnp.where(row < s, p, r[0:SUBLANES, :])
        return jnp.concatenate([top, r[SUBLANES:, :]], axis=0)

    y = cb_ref[...] + cw_ref[0:1, :] * shifted(2)
    y = y + cw_ref[1:2, :] * shifted(1)
    y = y + cw_ref[2:3, :] * cu
    yc_ref[...] = (bg * y).astype(BF16)


def _inproj(x2, g_mix, wqkv, wconv, b_in, gq, gk, hmat, cw, cb, seq):
    t = x2.shape[0]
    const = lambda i: (0, 0)
    row = lambda i: (i, 0)
    out = jax.ShapeDtypeStruct((t, ATTN_W), BF16)
    return pl.pallas_call(
        functools.partial(_inproj_kernel, tiles_per_seq=seq // TI),
        grid=(t // TI,),
        in_specs=[
            pl.BlockSpec((TI, D_MODEL), row),
            pl.BlockSpec((1, D_MODEL), const),
            pl.BlockSpec((D_MODEL, 3 * ATTN_W), const),
            pl.BlockSpec((D_MODEL, 3 * CONV_W), const),
            pl.BlockSpec((1, 3 * ATTN_W + 3 * CONV_W), const),
            pl.BlockSpec((1, ATTN_W), const),
            pl.BlockSpec((1, ATTN_W), const),
            pl.BlockSpec((ATTN_W, ATTN_W), const),
            pl.BlockSpec((SUBLANES, CONV_W), const),
            pl.BlockSpec((1, CONV_W), const),
        ],
        out_specs=[pl.BlockSpec((TI, ATTN_W), row)] * 4,
        out_shape=[out] * 4,
        scratch_shapes=[pltpu.VMEM((SUBLANES, CONV_W), F32)],
        compiler_params=pltpu.CompilerParams(
            dimension_semantics=("arbitrary",), vmem_limit_bytes=VMEM_LIMIT),
        name="inproj",
    )(x2, g_mix, wqkv, wconv, b_in, gq, gk, hmat, cw, cb)


def _lane_fold(parts, op):
    acc = None
    for a in parts:
        for c in range(0, a.shape[1], LANES):
            piece = a[:, c:c + LANES]
            acc = piece if acc is None else op(acc, piece)
    return acc


def _attn_kernel(q_ref, k0_ref, k1_ref, k2_ref, v0_ref, v1_ref, v2_ref, bias_ref, o_ref):
    k_refs = (k0_ref, k1_ref, k2_ref)
    v_refs = (v0_ref, v1_ref, v2_ref)
    pair_w = 2 * HEAD_DIM
    lane = lax.broadcasted_iota(jnp.int32, (TQ, pair_w), 1)
    low = lane < HEAD_DIM

    def scores(bb, h, pens):
        ps = slice((h // 2) * pair_w, (h // 2 + 1) * pair_w)
        q_pair = q_ref[bb, :, ps]
        own = low if h % 2 == 0 else jnp.logical_not(low)
        qh = jnp.where(own, q_pair, jnp.zeros_like(q_pair))
        s = [lax.dot_general(qh, k_refs[j][bb, :, ps], (((1,), (1,)), ((), ())),
                             preferred_element_type=F32) + bias_ref[h, :, j * TQ:(j + 1) * TQ]
             for j in range(KV_SLABS)]
        return s if pens is None else [sj + pens[j] for j, sj in enumerate(s)]

    def weighted(bb, h, s):
        ps = slice((h // 2) * pair_w, (h // 2 + 1) * pair_w)
        m = _lane_fold(s, jnp.maximum).max(axis=-1, keepdims=True)
        e = [jnp.exp2(sj - m) for sj in s]
        l = _lane_fold(e, jnp.add).sum(axis=-1, keepdims=True)
        acc = None
        for j in range(KV_SLABS):
            oj = _dot(e[j].astype(BF16), v_refs[j][bb, :, ps])
            acc = oj if acc is None else acc + oj
        return acc * (1.0 / l)

    def all_heads(pens):
        items = [(bb, h) for h in range(N_HEADS) for bb in range(ATTN_BATCH)]
        pending = [scores(bb, h, pens) for bb, h in items[:SCORE_AHEAD]]
        o_even = {}
        for n, (bb, h) in enumerate(items):
            if n + SCORE_AHEAD < len(items):
                pending.append(scores(*items[n + SCORE_AHEAD], pens))
            o = weighted(bb, h, pending.pop(0))
            if h % 2 == 0:
                o_even[bb] = o
            else:
                ps = slice((h // 2) * pair_w, (h // 2 + 1) * pair_w)
                o_ref[bb, :, ps] = jnp.where(low, o_even[bb], o).astype(BF16)

    i = pl.program_id(1)

    @pl.when(i >= KV_SLABS - 1)
    def _():
        all_heads(None)

    @pl.when(i < KV_SLABS - 1)
    def _():
        all_heads([jnp.where(i >= KV_SLABS - 1 - j, 0.0, NEG).astype(F32) for j in range(KV_SLABS)])


def _attention(q, k, v, bias):
    b, s, _ = q.shape
    blk = (ATTN_BATCH, TQ, ATTN_W)

    def kv_map(j):
        back = KV_SLABS - 1 - j
        return lambda bi, i: (bi, jnp.maximum(i - back, 0), 0)

    kv_specs = [pl.BlockSpec(blk, kv_map(j)) for j in range(KV_SLABS)]
    return pl.pallas_call(
        _attn_kernel,
        grid=(b // ATTN_BATCH, s // TQ),
        in_specs=[pl.BlockSpec(blk, lambda bi, i: (bi, i, 0))] + kv_specs + kv_specs + [
            pl.BlockSpec((N_HEADS, TQ, KV_SLABS * TQ), lambda bi, i: (0, 0, 0))],
        out_specs=pl.BlockSpec(blk, lambda bi, i: (bi, i, 0)),
        out_shape=jax.ShapeDtypeStruct((b, s, ATTN_W), BF16),
        compiler_params=pltpu.CompilerParams(
            dimension_semantics=("arbitrary", "arbitrary"), vmem_limit_bytes=VMEM_LIMIT),
        name="attn",
    )(q, k, k, k, v, v, v, bias)


def _attn_bias(rel_bias):
    nk = KV_SLABS * TQ
    past = nk - TQ
    d = jnp.arange(TQ - 1 + past, -TQ, -1)
    idx = jnp.clip(d, -(CHUNK - 1), MAX_REL_PAST) + (CHUNK - 1)
    onehot = (idx[:, None] == jnp.arange(rel_bias.shape[1])[None, :]).astype(F32)
    per_dist = jnp.einsum("dn,hn->hd", onehot, rel_bias.astype(F32) * LOG2E,
                          precision=lax.Precision.HIGHEST)
    n_h, span = per_dist.shape
    padded = jnp.pad(per_dist, ((0, 0), (0, 2)))
    skew = jnp.tile(padded, (1, TQ))[:, :TQ * (span + 1)].reshape(n_h, TQ, span + 1)
    table = skew[:, :, TQ - 1:TQ - 1 + nk]
    r = jnp.arange(TQ)[:, None]
    c = jnp.arange(nk)[None, :]
    qc = r // CHUNK
    kc = c // CHUNK
    lead = past // CHUNK - LEFT_CHUNKS
    band = (kc >= qc + lead) & (kc <= qc + lead + LEFT_CHUNKS)
    return jnp.where(band[None], table, NEG)


def _merge_kernel(x_ref, ya_ref, yc_ref, g_ref, wg_ref, bgate_ref, wpa_ref, wpc_ref, wo_ref,
                  gffn_ref, wrt_ref, brt_ref, h_ref, stage_ref, route_ref, cnt_ref, n2_scr, logit_scr):
    @pl.when(pl.program_id(0) == 0)
    def _():
        n2_scr[...] = jnp.zeros_like(n2_scr)
        logit_scr[...] = jnp.zeros_like(logit_scr)

    n2 = n2_scr[...]
    lt = logit_scr[...].T
    row8 = lax.broadcasted_iota(jnp.int32, (SUBLANES, TM), 0).astype(F32)
    ninf = -jnp.inf

    def argmax_first(vals):
        mx = vals.max(axis=0, keepdims=True)
        idx = jnp.where(vals == mx, row8, float(SUBLANES)).min(axis=0, keepdims=True)
        return mx, idx

    gl = jnp.where(row8 < N_GROUPS, lt[N_EXPERTS:N_EXPERTS + SUBLANES, :], ninf)
    gmax, grp = argmax_first(gl)
    p_grp = 1.0 / jnp.exp(gl - gmax).sum(axis=0, keepdims=True)
    el = lt[0:EXPERTS_PER_GROUP, :]
    for g in range(1, N_GROUPS):
        el = jnp.where(grp == g, lt[g * EXPERTS_PER_GROUP:(g + 1) * EXPERTS_PER_GROUP, :], el)
    l1, i1 = argmax_first(el)
    l2, i2 = argmax_first(jnp.where(row8 == i1, ninf, el))
    e2 = jnp.exp(l2 - l1)
    den = 1.0 + e2
    w1 = p_grp * (1.0 / den)
    w2 = p_grp * (e2 / den)
    x1 = grp * EXPERTS_PER_GROUP + i1
    x2 = grp * EXPERTS_PER_GROUP + i2

    row_e = lax.broadcasted_iota(jnp.int32, (N_EXPERTS, TM), 0).astype(F32)
    oh1 = (row_e == x1).astype(F32)
    oh2 = (row_e == x2).astype(F32)
    oh = (oh1 + oh2).astype(BF16)
    r = lax.broadcasted_iota(jnp.int32, (TM, TM), 0)
    c = lax.broadcasted_iota(jnp.int32, (TM, TM), 1)
    earlier_tok = _dot(oh, (r < c).astype(BF16))
    er = lax.broadcasted_iota(jnp.int32, (N_EXPERTS, N_EXPERTS), 0)
    ec = lax.broadcasted_iota(jnp.int32, (N_EXPERTS, N_EXPERTS), 1)
    lower_exp = _dot((ec < er).astype(BF16), oh).sum(axis=1, keepdims=True)
    where = earlier_tok + lower_exp
    pos1 = (oh1 * where).sum(axis=0, keepdims=True)
    pos2 = (oh2 * where).sum(axis=0, keepdims=True)
    counts = lax.dot_general(jnp.ones((SUBLANES, TM), BF16), oh, (((1,), (1,)), ((), ())),
                             preferred_element_type=F32)
    cnt_ref[0] = jnp.concatenate([counts, jnp.zeros((SUBLANES, LANES - N_EXPERTS), F32)], axis=1)

    route_t = jnp.zeros((SUBLANES, TM), F32)
    for j, val in enumerate((x1, x2, w1, w2, pos1, pos2)):
        route_t = jnp.where(row8 == j, val, route_t)
    route_ref[...] = jnp.concatenate([route_t, jnp.zeros((LANES - SUBLANES, TM), F32)], axis=0).T

    x = x_ref[...]
    nb = _rms(x, g_ref[...]).astype(BF16)
    sga = _sigmoid(_dot(nb, wg_ref[:, 0:D_MODEL]) + bgate_ref[:, 0:D_MODEL])
    ma = sga * _dot(ya_ref[...], wpa_ref[...])
    sgc = _sigmoid(_dot(nb, wg_ref[:, D_MODEL:2 * D_MODEL]) + bgate_ref[:, D_MODEL:2 * D_MODEL])
    m = ma + sgc * _dot(yc_ref[...], wpc_ref[...])
    h = x + _dot(m.astype(BF16), wo_ref[...])
    h_ref[...] = h
    n2_new = _rms(h, gffn_ref[...]).astype(BF16)
    n2_scr[...] = n2_new
    logit_scr[...] = _dot(n2_new, wrt_ref[...]) + brt_ref[...]

    slot = lax.broadcasted_iota(jnp.int32, (TILE_ROWS, TM), 0).astype(F32)
    place = ((slot == pos1) | (slot == pos2)).astype(BF16)
    _pack_rows(stage_ref, _dot(place, n2), TILE_ROWS, is_bf16=True)


def _merge(x2, ya, yc, g_mix, wgate, bgate, wpa, wpc, wo, gffn, wrt, brt):
    t = x2.shape[0]
    n_tiles = t // TM
    const = lambda i: (0, 0)
    row = lambda i: (jnp.minimum(i, n_tiles - 1), 0)
    late = lambda i: (jnp.maximum(i - 1, 0), 0)
    return pl.pallas_call(
        _merge_kernel,
        grid=(n_tiles + 1,),
        in_specs=[
            pl.BlockSpec((TM, D_MODEL), row),
            pl.BlockSpec((TM, ATTN_W), row),
            pl.BlockSpec((TM, CONV_W), row),
            pl.BlockSpec((1, D_MODEL), const),
            pl.BlockSpec((D_MODEL, 2 * D_MODEL), const),
            pl.BlockSpec((1, 2 * D_MODEL), const),
            pl.BlockSpec((ATTN_W, D_MODEL), const),
            pl.BlockSpec((CONV_W, D_MODEL), const),
            pl.BlockSpec((D_MODEL, D_MODEL), const),
            pl.BlockSpec((1, D_MODEL), const),
            pl.BlockSpec((D_MODEL, LANES), const),
            pl.BlockSpec((1, LANES), const),
        ],
        out_specs=[
            pl.BlockSpec((TM, D_MODEL), row),
            pl.BlockSpec((TILE_ROWS * PACK_ROWS, LANES), late),
            pl.BlockSpec((TM, LANES), late),
            pl.BlockSpec((1, SUBLANES, LANES), lambda i: (jnp.maximum(i - 1, 0), 0, 0)),
        ],
        out_shape=[
            jax.ShapeDtypeStruct((t, D_MODEL), F32),
            jax.ShapeDtypeStruct((n_tiles * TILE_ROWS * PACK_ROWS, LANES), U32),
            jax.ShapeDtypeStruct((t, LANES), F32),
            jax.ShapeDtypeStruct((n_tiles, SUBLANES, LANES), F32),
        ],
        scratch_shapes=[pltpu.VMEM((TM, D_MODEL), BF16), pltpu.VMEM((TM, LANES), F32)],
        compiler_params=pltpu.CompilerParams(
            dimension_semantics=("arbitrary",), vmem_limit_bytes=VMEM_LIMIT),
        name="merge",
    )(x2, ya, yc, g_mix, wgate, bgate, wpa, wpc, wo, gffn, wrt, brt)


def _piece_counts(n):
    return [(n >> (size.bit_length() - 1)) & 1 for size in SEG_SIZES]


ALL_CLASSES = tuple(range(len(SEG_SIZES)))
BIG_CLASSES = tuple(c for c in ALL_CLASSES if SEG_SIZES[c] >= BIG_PIECE)
SMALL_CLASSES = tuple(c for c in ALL_CLASSES if SEG_SIZES[c] < BIG_PIECE)


def _segment_pieces(n, visit, classes=ALL_CLASSES):
    for cls in classes:
        size = SEG_SIZES[cls]

        @pl.when((n & size) != 0)
        def _(cls=cls, size=size):
            visit(cls, n & ~(2 * size - 1))


def _piece_copy(src_ref, dst_ref, sems, cls, src_row, dst_row):
    n = SEG_SIZES[cls] * PACK_ROWS
    return pltpu.make_async_copy(src_ref.at[pl.ds(src_row * PACK_ROWS, n), :],
                                 dst_ref.at[pl.ds(dst_row * PACK_ROWS, n), :], sems.at[cls])


def _drain(src_ref, dst_ref, sems, counts, has_big):
    unroll = 4

    def wait_classes(classes):
        for cls in classes:
            def wait_some(k, cls=cls):
                def body(t, carry):
                    for _ in range(k):
                        _piece_copy(src_ref, dst_ref, sems, cls, 0, 0).wait()
                    return carry
                return body

            n = counts[cls]
            lax.fori_loop(0, n >> 2, wait_some(unroll), 0)
            lax.fori_loop(0, n & (unroll - 1), wait_some(1), 0)

    @pl.when(has_big)
    def _():
        wait_classes(BIG_CLASSES)

    wait_classes(SMALL_CLASSES)


def _experts_kernel(be_ref, nb_ref, base_ref, jlo_ref, jhi_ref, nv_ref, big_ref, npiece_ref, cum_ref, end_ref,
                    src_ref, stage_hbm, w1_ref, w3_ref, w2_ref, ys_ref, *scratch, n_tiles, n_blocks):
    xbufs = scratch[:GATHER_AHEAD + 1]
    w1b_ref, w3b_ref, w2b_ref, sems = scratch[GATHER_AHEAD + 1:]
    s = pl.program_id(0)
    nb = nb_ref[0]
    n_cls = len(SEG_SIZES)

    def segment_copies(step, j, live, buf, sem, classes=ALL_CLASSES):
        base = base_ref[step]
        g = jnp.minimum(j, n_tiles - 1) * N_EXPERTS + be_ref[step]
        lo = jnp.maximum(cum_ref[g], base)
        hi = jnp.minimum(end_ref[g], base + ROW_BLK)
        src = src_ref[g] + lo
        dst = lo - base
        _segment_pieces(jnp.where(live, jnp.maximum(hi - lo, 0), 0),
                        lambda cls, o: _piece_copy(stage_hbm, buf, sem, cls, src + o, dst + o).start(), classes)

    def looped_copies(step, j0, j1, buf, sem, classes=ALL_CLASSES):
        def body(j, carry):
            segment_copies(step, j, True, buf, sem, classes)
            return carry

        lax.fori_loop(j0, j1, body, 0)

    @pl.when(s == 0)
    def _():
        for buf in xbufs:
            buf[...] = jnp.zeros_like(buf)
        for first in range(GATHER_AHEAD):
            blk = min(first, n_blocks - 1)
            looped_copies(blk, jlo_ref[blk], jnp.where(first < nb, jhi_ref[blk], jlo_ref[blk]),
                          xbufs[first], sems.at[first])

    def step(cur, cur_sem, nxt, nxt_sem):
        prev = be_ref[jnp.maximum(s - 1, 0)]

        @pl.when((s == 0) | (be_ref[s] != prev))
        def _():
            w1b_ref[...] = w1_ref[0].astype(BF16)
            w3b_ref[...] = w3_ref[0].astype(BF16)
            w2b_ref[...] = w2_ref[0].astype(BF16)

        _drain(stage_hbm, cur, cur_sem, [npiece_ref[s * n_cls + c] for c in range(n_cls)], big_ref[s] != 0)

        nxt_step = jnp.minimum(s + GATHER_AHEAD, n_blocks - 1)
        live = s + GATHER_AHEAD < nb
        j0 = jlo_ref[nxt_step]
        j1 = jnp.where(live, jhi_ref[nxt_step], j0)
        looped_copies(nxt_step, j0 + GATHER_UNROLL, j1, nxt, nxt_sem, SMALL_CLASSES)
        looped_copies(nxt_step, j0, jnp.where(big_ref[nxt_step] != 0, j1, j0), nxt, nxt_sem, BIG_CLASSES)
        groups = iter(_split(range(GATHER_UNROLL), 2 * PACK_ROWS))

        def start_group():
            for k in next(groups):
                segment_copies(nxt_step, j0 + k, j0 + k < j1, nxt, nxt_sem, SMALL_CLASSES)

        a = None
        g = None
        for blk in range(PACK_ROWS):
            start_group()
            xa = _unpack_block(cur, ROW_BLK, blk, n_valid=nv_ref[s])
            rows = slice(blk * PACK_W, (blk + 1) * PACK_W)
            da = _dot(xa, w1b_ref[rows, :])
            dg = _dot(xa, w3b_ref[rows, :])
            a = da if a is None else a + da
            g = dg if g is None else g + dg
        hdn = ((a * _sigmoid(a)) * g).astype(BF16)
        for blk in range(PACK_ROWS):
            start_group()
            _pack_block(ys_ref, _dot(hdn, w2b_ref[:, blk * PACK_W:(blk + 1) * PACK_W]), ROW_BLK, blk)

    n_buf = len(xbufs)
    for slot in range(n_buf):
        @pl.when((s < nb) & (s % n_buf == slot))
        def _(slot=slot):
            ahead = (slot + GATHER_AHEAD) % n_buf
            step(xbufs[slot], sems.at[slot], xbufs[ahead], sems.at[ahead])

    @pl.when(s >= nb)
    def _():
        ys_ref[...] = jnp.zeros_like(ys_ref)


def _experts(blk_e, nblk, base, jlo, jhi, nvalid, big, npiece, cum, end, src, stage, w1, w3, w2):
    n_blocks = blk_e.shape[0]
    n_tiles = cum.shape[0] // N_EXPERTS

    def wsel(s, be, nb, *_):
        return (be[jnp.minimum(s, nb[0] - 1)], 0, 0)

    grid_spec = pltpu.PrefetchScalarGridSpec(
        num_scalar_prefetch=11,
        grid=(n_blocks,),
        in_specs=[
            pl.BlockSpec(memory_space=pl.ANY),
            pl.BlockSpec((1, D_MODEL, D_EXPERT), wsel),
            pl.BlockSpec((1, D_MODEL, D_EXPERT), wsel),
            pl.BlockSpec((1, D_EXPERT, D_MODEL), wsel),
        ],
        out_specs=pl.BlockSpec((ROW_BLK * PACK_ROWS, LANES), lambda s, *_: (s, 0)),
        scratch_shapes=[pltpu.VMEM((ROW_BLK * PACK_ROWS, LANES), U32)] * (GATHER_AHEAD + 1) + [
            pltpu.VMEM((D_MODEL, D_EXPERT), BF16),
            pltpu.VMEM((D_MODEL, D_EXPERT), BF16),
            pltpu.VMEM((D_EXPERT, D_MODEL), BF16),
            pltpu.SemaphoreType.DMA((GATHER_AHEAD + 1, len(SEG_SIZES))),
        ],
    )
    return pl.pallas_call(
        functools.partial(_experts_kernel, n_tiles=n_tiles, n_blocks=n_blocks),
        grid_spec=grid_spec,
        out_shape=jax.ShapeDtypeStruct((n_blocks * ROW_BLK * PACK_ROWS, LANES), U32),
        compiler_params=pltpu.CompilerParams(
            dimension_semantics=("arbitrary",), vmem_limit_bytes=VMEM_LIMIT),
        name="experts",
    )(blk_e, nblk, base, jlo, jhi, nvalid, big, npiece, cum, end, src, stage, w1, w3, w2)


def _combine_kernel(cnt_ref, off_ref, dst_ref, big_ref, npiece_ref, h_ref, route_ref, p_ref, gple_ref, wpg_ref,
                    bpg_ref, wpp_ref, ys_hbm, o_ref, *scratch, n_steps):
    ybufs, sems = scratch[:-1], scratch[-1]
    i = pl.program_id(0)
    n_cls = len(SEG_SIZES)

    def segment_copies(step, e, live, buf, sem, classes=ALL_CLASSES):
        g = step * N_EXPERTS + e
        off, dst = off_ref[g], dst_ref[g]
        _segment_pieces(jnp.where(live, cnt_ref[g], 0),
                        lambda cls, o: _piece_copy(ys_hbm, buf, sem, cls, dst + o, off + o).start(), classes)

    def looped_copies(step, n_experts, buf, sem, classes=ALL_CLASSES):
        def body(e, carry):
            segment_copies(step, e, True, buf, sem, classes)
            return carry

        lax.fori_loop(0, n_experts, body, 0)

    @pl.when(i == 0)
    def _():
        for first in range(min(GATHER_AHEAD, n_steps)):
            looped_copies(first, N_EXPERTS, ybufs[first], sems.at[first])

    def step(cur, cur_sem, nxt, nxt_sem):
        _drain(ys_hbm, cur, cur_sem, [npiece_ref[i * n_cls + c] for c in range(n_cls)], big_ref[i] != 0)
        nxt_step = jnp.minimum(i + GATHER_AHEAD, n_steps - 1)
        live = i + GATHER_AHEAD < n_steps
        looped_copies(nxt_step, jnp.where(live & (big_ref[nxt_step] != 0), N_EXPERTS, 0), nxt, nxt_sem, BIG_CLASSES)
        groups = iter(_split(range(N_EXPERTS), PACK_ROWS + 2))

        def start_group():
            for e in next(groups):
                segment_copies(nxt_step, e, live, nxt, nxt_sem, SMALL_CLASSES)

        start_group()
        pp = _dot(p_ref[...].astype(BF16), wpp_ref[...])
        route = route_ref[...]
        place = lax.broadcasted_iota(jnp.int32, (TM, TILE_ROWS), 1).astype(F32)
        sel = [(place == route[:, 4 + kk:5 + kk]).astype(BF16) for kk in range(TOP_K)]
        moe = []
        for blk in range(PACK_ROWS):
            start_group()
            cols = _unpack_block(cur, TILE_ROWS, blk)
            moe.append(_dot(sel[0], cols) * route[:, 2:3] + _dot(sel[1], cols) * route[:, 3:4])
        start_group()
        h = h_ref[...] + jnp.concatenate(moe, axis=1)
        gate = _sigmoid(_dot(_rms(h, gple_ref[...]).astype(BF16), wpg_ref[...]) + bpg_ref[...])
        o_ref[...] = h + gate * pp

    n_buf = len(ybufs)
    for slot in range(n_buf):
        @pl.when(i % n_buf == slot)
        def _(slot=slot):
            ahead = (slot + GATHER_AHEAD) % n_buf
            step(ybufs[slot], sems.at[slot], ybufs[ahead], sems.at[ahead])


def _combine(cnt, off, dst, big, npiece, h1, route, p2, gple, wpg, bpg, wpp, ys):
    t = h1.shape[0]
    n_steps = t // TM
    const = lambda i, *_: (0, 0)
    row = lambda i, *_: (i, 0)
    grid_spec = pltpu.PrefetchScalarGridSpec(
        num_scalar_prefetch=5,
        grid=(n_steps,),
        in_specs=[
            pl.BlockSpec((TM, D_MODEL), row),
            pl.BlockSpec((TM, LANES), row),
            pl.BlockSpec((TM, PLE_DIM), row),
            pl.BlockSpec((1, D_MODEL), const),
            pl.BlockSpec((D_MODEL, D_MODEL), const),
            pl.BlockSpec((1, D_MODEL), const),
            pl.BlockSpec((PLE_DIM, D_MODEL), const),
            pl.BlockSpec(memory_space=pl.ANY),
        ],
        out_specs=pl.BlockSpec((TM, D_MODEL), row),
        scratch_shapes=[pltpu.VMEM((TILE_ROWS * PACK_ROWS, LANES), U32)] * (GATHER_AHEAD + 1) + [
            pltpu.SemaphoreType.DMA((GATHER_AHEAD + 1, len(SEG_SIZES))),
        ],
    )
    return pl.pallas_call(
        functools.partial(_combine_kernel, n_steps=n_steps),
        grid_spec=grid_spec,
        out_shape=jax.ShapeDtypeStruct((t, D_MODEL), F32),
        compiler_params=pltpu.CompilerParams(
            dimension_semantics=("arbitrary",), vmem_limit_bytes=VMEM_LIMIT),
        name="combine",
    )(cnt, off, dst, big, npiece, h1, route, p2, gple, wpg, bpg, wpp, ys)


def _layer(h, p_i, g_mix, w_in, b_in, g_q, g_k, rel_bias, conv_w, conv_b, w_pa, w_pc, w_o,
           g_ffn, w_group, b_group, w_router, b_router, w1, w3, w2,
           g_ple, w_ple_gate, b_ple_gate, w_ple_proj):
    b, s, d = h.shape
    t = b * s
    x2 = h.reshape(t, d)
    row2 = lambda a: a.reshape(1, -1).astype(F32)

    qkv_w = 3 * ATTN_W
    conv_end = qkv_w + 3 * CONV_W
    w_in_b = w_in.astype(BF16)
    gq = row2(jnp.tile(g_q.astype(F32) * (HEAD_DIM ** -0.5 * LOG2E), N_HEADS))
    gk = row2(jnp.tile(g_k.astype(F32), N_HEADS))
    head = jnp.arange(ATTN_W) // HEAD_DIM
    hmat = jnp.where(head[:, None] == head[None, :], 1.0 / HEAD_DIM, 0.0).astype(BF16)
    cw = jnp.concatenate([conv_w.astype(F32), jnp.zeros((SUBLANES - CONV_K, CONV_W), F32)], axis=0)

    q, k, v, yc = _inproj(x2, row2(g_mix), w_in_b[:, :qkv_w], w_in_b[:, qkv_w:conv_end],
                          row2(b_in[:conv_end]), gq, gk, hmat, cw, row2(conv_b), s)

    ya = _attention(q.reshape(b, s, ATTN_W), k.reshape(b, s, ATTN_W), v.reshape(b, s, ATTN_W),
                    _attn_bias(rel_bias)).reshape(t, ATTN_W)

    n_pad = LANES - N_GROUPS - N_EXPERTS
    wrt = jnp.concatenate([w_router, w_group, jnp.zeros((d, n_pad), w_group.dtype)], axis=1).astype(BF16)
    brt = row2(jnp.concatenate([b_router, b_group, jnp.zeros((n_pad,), b_group.dtype)]))
    h1, stage, route, cnt_f = _merge(x2, ya, yc, row2(g_mix), w_in_b[:, conv_end:], row2(b_in[conv_end:]),
                                     w_pa.astype(BF16), w_pc.astype(BF16), w_o.astype(BF16),
                                     row2(g_ffn), wrt, brt)

    n_tiles = t // TM
    cnt = cnt_f[:, 0, :N_EXPERTS].astype(jnp.int32)
    tile_off = jnp.cumsum(cnt, axis=1) - cnt
    tot = cnt.sum(axis=0)
    pcounts = (tot + ROW_BLK - 1) // ROW_BLK * ROW_BLK
    pends = jnp.cumsum(pcounts)
    pstarts = pends - pcounts
    cum = jnp.cumsum(cnt, axis=0) - cnt
    dst = pstarts[None, :] + cum
    n_blocks = (t * TOP_K) // ROW_BLK + N_EXPERTS
    blk_start = jnp.arange(n_blocks, dtype=jnp.int32) * ROW_BLK
    blk_e = jnp.minimum((pends[None, :] <= blk_start[:, None]).sum(axis=1), N_EXPERTS - 1).astype(jnp.int32)
    nblk = (pends[-1:] // ROW_BLK).astype(jnp.int32)
    sel = (jnp.arange(N_EXPERTS, dtype=jnp.int32)[:, None] == blk_e[None, :]).astype(jnp.int32)
    of_block = lambda a: (a[..., None] * sel).sum(axis=-2)
    base = blk_start - of_block(pstarts)
    nvalid = jnp.clip(of_block(tot) - base, 0, ROW_BLK)
    cum_e = of_block(cum)
    cnt_e = of_block(cnt)
    jlo = (cum_e + cnt_e <= base[None, :]).sum(axis=0)
    jhi = (cum_e < base[None, :] + ROW_BLK).sum(axis=0)
    part = jnp.clip(jnp.minimum(cum_e + cnt_e, base[None, :] + ROW_BLK) - jnp.maximum(cum_e, base[None, :]),
                    0, ROW_BLK)
    blk_pieces = jnp.stack(_piece_counts(part), axis=-1).sum(axis=0)
    tile_pieces = jnp.stack(_piece_counts(cnt), axis=-1).sum(axis=1)
    src = jnp.arange(n_tiles, dtype=jnp.int32)[:, None] * TILE_ROWS + tile_off - cum
    flat = lambda a: a.reshape(-1).astype(jnp.int32)

    blk_big = (part >= BIG_PIECE).any(axis=0)
    tile_big = (cnt >= BIG_PIECE).any(axis=1)

    ys = _experts(blk_e, nblk, flat(base), flat(jlo), flat(jhi), flat(nvalid), flat(blk_big), flat(blk_pieces),
                  flat(cum), flat(cum + cnt), flat(src), stage, w1, w3, w2)
    out = _combine(flat(cnt), flat(tile_off), flat(dst), flat(tile_big), flat(tile_pieces), h1, route,
                   p_i.reshape(t, PLE_DIM), row2(g_ple), w_ple_gate.astype(BF16), row2(b_ple_gate),
                   w_ple_proj.astype(BF16), ys)
    return out.reshape(b, s, d)


def kernel(x, p, g_mix, w_in, b_in, g_q, g_k, rel_bias, conv_w, conv_b, w_pa, w_pc, w_o, g_ffn, w_group, b_group, w_router, b_router, w1, w3, w2, g_ple, w_ple_gate, b_ple_gate, w_ple_proj):
    h = x
    for i in range(p.shape[0]):
        h = _layer(h, p[i], g_mix[i], w_in[i], b_in[i], g_q[i], g_k[i], rel_bias[i], conv_w[i], conv_b[i],
                   w_pa[i], w_pc[i], w_o[i], g_ffn[i], w_group[i], b_group[i], w_router[i], b_router[i],
                   w1[i], w3[i], w2[i], g_ple[i], w_ple_gate[i], b_ple_gate[i], w_ple_proj[i])
    return h
```

```python
import functools

import jax
import jax.numpy as jnp
from jax import lax
from jax.experimental import pallas as pl
from jax.experimental.pallas import tpu as pltpu

D_MODEL = 1024
CHUNK = 64
LEFT_CHUNKS = 8
N_HEADS = 8
HEAD_DIM = 64
ATTN_W = N_HEADS * HEAD_DIM
CONV_W = D_MODEL // 2
CONV_K = 3
MAX_REL_PAST = 256
PLE_DIM = 256
N_GROUPS = 4
EXPERTS_PER_GROUP = 8
N_EXPERTS = N_GROUPS * EXPERTS_PER_GROUP
TOP_K = 2
D_EXPERT = 512
EPS = 1e-6
NEG = -1e30
LOG2E = 1.4426950408889634

LANES = 128
SUBLANES = 8
TM = 256
TI = 512
MERGE_TILES = 2
TQ = 256
KV_SLABS = 1 + (LEFT_CHUNKS * CHUNK) // TQ
ROW_BLK = 512
TILE_ROWS = TOP_K * TM
PACK_ROWS = D_MODEL // (2 * LANES)
PACK_W = 2 * LANES
SEG_SIZES = tuple(TM >> k for k in range(TM.bit_length()))
BIG_PIECE = 32
ATTN_BATCH = 2
SCORE_AHEAD = 2
GATHER_AHEAD = 2
GATHER_UNROLL = 36
VMEM_LIMIT = 56 * 1024 * 1024

F32 = jnp.float32
BF16 = jnp.bfloat16
U32 = jnp.uint32


def _dot(a, b):
    return jnp.dot(a, b, preferred_element_type=F32)


def _rms(x, g):
    ms = jnp.mean(x * x, axis=-1, keepdims=True)
    return (x * lax.rsqrt(ms + EPS)) * g


def _sigmoid(x):
    return 1.0 / (1.0 + jnp.exp(-x))


def _pack_block(ref, vals, n_rows, a, is_bf16=False, first_row=0):
    lo = vals[:, 0:LANES]
    hi = vals[:, LANES:PACK_W]
    if not is_bf16:
        lo = lo.astype(BF16).astype(F32)
        hi = hi.astype(BF16).astype(F32)
    ref[pl.ds(first_row * PACK_ROWS + a, n_rows, stride=PACK_ROWS), :] = (
        lax.bitcast_convert_type(hi, U32) | (lax.bitcast_convert_type(lo, U32) >> 16))


def _pack_rows(ref, vals, n_rows, is_bf16=False, first_row=0):
    for a in range(PACK_ROWS):
        _pack_block(ref, vals[:, a * PACK_W:(a + 1) * PACK_W], n_rows, a, is_bf16, first_row)


def _unpack_block(ref, n_rows, a, n_valid=None):
    word = ref[pl.ds(a, n_rows, stride=PACK_ROWS), :]
    if n_valid is not None:
        word = jnp.where(lax.broadcasted_iota(jnp.int32, (n_rows, LANES), 0) < n_valid, word, U32(0))
    lo = lax.bitcast_convert_type(word << 16, F32).astype(BF16)
    hi = lax.bitcast_convert_type(word & U32(0xFFFF0000), F32).astype(BF16)
    return jnp.concatenate([lo, hi], axis=1)


def _split(items, n_groups):
    items = list(items)
    return [items[len(items) * g // n_groups:len(items) * (g + 1) // n_groups] for g in range(n_groups)]


def _inproj_kernel(x_ref, g_ref, wqkv_ref, wconv_ref, b_ref, gq_ref, gk_ref, hm_ref,
                   cw_ref, cb_ref, q_ref, k_ref, v_ref, yc_ref, carry_ref, *, tiles_per_seq):
    i = pl.program_id(0)
    nb = _rms(x_ref[...], g_ref[...]).astype(BF16)

    zq = _dot(nb, wqkv_ref[...]) + b_ref[:, 0:3 * ATTN_W]
    hm = hm_ref[...]

    def head_rms(t, g):
        ms = _dot((t * t).astype(BF16), hm)
        return (t * lax.rsqrt(ms + EPS)) * g

    q_ref[...] = head_rms(zq[:, 0:ATTN_W], gq_ref[...]).astype(BF16)
    k_ref[...] = head_rms(zq[:, ATTN_W:2 * ATTN_W], gk_ref[...]).astype(BF16)
    v_ref[...] = zq[:, 2 * ATTN_W:3 * ATTN_W].astype(BF16)

    zc = _dot(nb, wconv_ref[...]) + b_ref[:, 3 * ATTN_W:3 * ATTN_W + 3 * CONV_W]
    u = zc[:, 0:CONV_W]
    bg = zc[:, CONV_W:2 * CONV_W]
    cg = zc[:, 2 * CONV_W:3 * CONV_W]
    cu = cg * u

    @pl.when((i % tiles_per_seq) == 0)
    def _():
        carry_ref[...] = jnp.zeros_like(carry_ref)

    prev = carry_ref[...]
    carry_ref[...] = cu[TI - SUBLANES:TI, :]
    row = lax.broadcasted_iota(jnp.int32, (SUBLANES, CONV_W), 0)

    def shifted(s):
        r = pltpu.roll(cu, s, 0)
        p = pltpu.roll(prev, s, 0)
        top = jnp.where(row < s, p, r[0:SUBLANES, :])
        return jnp.concatenate([top, r[SUBLANES:, :]], axis=0)

    y = cb_ref[...] + cw_ref[0:1, :] * shifted(2)
    y = y + cw_ref[1:2, :] * shifted(1)
    y = y + cw_ref[2:3, :] * cu
    yc_ref[...] = (bg * y).astype(BF16)


def _inproj(x2, g_mix, wqkv, wconv, b_in, gq, gk, hmat, cw, cb, seq):
    t = x2.shape[0]
    const = lambda i: (0, 0)
    row = lambda i: (i, 0)
    out = jax.ShapeDtypeStruct((t, ATTN_W), BF16)
    return pl.pallas_call(
        functools.partial(_inproj_kernel, tiles_per_seq=seq // TI),
        grid=(t // TI,),
        in_specs=[
            pl.BlockSpec((TI, D_MODEL), row),
            pl.BlockSpec((1, D_MODEL), const),
            pl.BlockSpec((D_MODEL, 3 * ATTN_W), const),
            pl.BlockSpec((D_MODEL, 3 * CONV_W), const),
            pl.BlockSpec((1, 3 * ATTN_W + 3 * CONV_W), const),
            pl.BlockSpec((1, ATTN_W), const),
            pl.BlockSpec((1, ATTN_W), const),
            pl.BlockSpec((ATTN_W, ATTN_W), const),
            pl.BlockSpec((SUBLANES, CONV_W), const),
            pl.BlockSpec((1, CONV_W), const),
        ],
        out_specs=[pl.BlockSpec((TI, ATTN_W), row)] * 4,
        out_shape=[out] * 4,
        scratch_shapes=[pltpu.VMEM((SUBLANES, CONV_W), F32)],
        compiler_params=pltpu.CompilerParams(
            dimension_semantics=("arbitrary",), vmem_limit_bytes=VMEM_LIMIT),
        name="inproj",
    )(x2, g_mix, wqkv, wconv, b_in, gq, gk, hmat, cw, cb)


def _lane_fold(parts, op):
    acc = None
    for a in parts:
        for c in range(0, a.shape[1], LANES):
            piece = a[:, c:c + LANES]
            acc = piece if acc is None else op(acc, piece)
    return acc


def _attn_kernel(q_ref, k0_ref, k1_ref, k2_ref, v0_ref, v1_ref, v2_ref, bias_ref, o_ref):
    k_refs = (k0_ref, k1_ref, k2_ref)
    v_refs = (v0_ref, v1_ref, v2_ref)
    pair_w = 2 * HEAD_DIM
    lane = lax.broadcasted_iota(jnp.int32, (TQ, pair_w), 1)
    low = lane < HEAD_DIM

    def scores(bb, h, pens):
        ps = slice((h // 2) * pair_w, (h // 2 + 1) * pair_w)
        q_pair = q_ref[bb, :, ps]
        own = low if h % 2 == 0 else jnp.logical_not(low)
        qh = jnp.where(own, q_pair, jnp.zeros_like(q_pair))
        s = [lax.dot_general(qh, k_refs[j][bb, :, ps], (((1,), (1,)), ((), ())),
                             preferred_element_type=F32) + bias_ref[h, :, j * TQ:(j + 1) * TQ]
             for j in range(KV_SLABS)]
        return s if pens is None else [sj + pens[j] for j, sj in enumerate(s)]

    def weighted(bb, h, s):
        ps = slice((h // 2) * pair_w, (h // 2 + 1) * pair_w)
        m = _lane_fold(s, jnp.maximum).max(axis=-1, keepdims=True)
        e = [jnp.exp2(sj - m) for sj in s]
        l = _lane_fold(e, jnp.add).sum(axis=-1, keepdims=True)
        acc = None
        for j in range(KV_SLABS):
            oj = _dot(e[j].astype(BF16), v_refs[j][bb, :, ps])
            acc = oj if acc is None else acc + oj
        return acc * (1.0 / l)

    def all_heads(pens):
        items = [(bb, h) for h in range(N_HEADS) for bb in range(ATTN_BATCH)]
        pending = [scores(bb, h, pens) for bb, h in items[:SCORE_AHEAD]]
        o_even = {}
        for n, (bb, h) in enumerate(items):
            if n + SCORE_AHEAD < len(items):
                pending.append(scores(*items[n + SCORE_AHEAD], pens))
            o = weighted(bb, h, pending.pop(0))
            if h % 2 == 0:
                o_even[bb] = o
            else:
                ps = slice((h // 2) * pair_w, (h // 2 + 1) * pair_w)
                o_ref[bb, :, ps] = jnp.where(low, o_even[bb], o).astype(BF16)

    i = pl.program_id(1)

    @pl.when(i >= KV_SLABS - 1)
    def _():
        all_heads(None)

    @pl.when(i < KV_SLABS - 1)
    def _():
        all_heads([jnp.where(i >= KV_SLABS - 1 - j, 0.0, NEG).astype(F32) for j in range(KV_SLABS)])


def _attention(q, k, v, bias):
    b, s, _ = q.shape
    blk = (ATTN_BATCH, TQ, ATTN_W)

    def kv_map(j):
        back = KV_SLABS - 1 - j
        return lambda bi, i: (bi, jnp.maximum(i - back, 0), 0)

    kv_specs = [pl.BlockSpec(blk, kv_map(j)) for j in range(KV_SLABS)]
    return pl.pallas_call(
        _attn_kernel,
        grid=(b // ATTN_BATCH, s // TQ),
        in_specs=[pl.BlockSpec(blk, lambda bi, i: (bi, i, 0))] + kv_specs + kv_specs + [
            pl.BlockSpec((N_HEADS, TQ, KV_SLABS * TQ), lambda bi, i: (0, 0, 0))],
        out_specs=pl.BlockSpec(blk, lambda bi, i: (bi, i, 0)),
        out_shape=jax.ShapeDtypeStruct((b, s, ATTN_W), BF16),
        compiler_params=pltpu.CompilerParams(
            dimension_semantics=("arbitrary", "arbitrary"), vmem_limit_bytes=VMEM_LIMIT),
        name="attn",
    )(q, k, k, k, v, v, v, bias)


def _attn_bias(rel_bias):
    nk = KV_SLABS * TQ
    past = nk - TQ
    d = jnp.arange(TQ - 1 + past, -TQ, -1)
    idx = jnp.clip(d, -(CHUNK - 1), MAX_REL_PAST) + (CHUNK - 1)
    onehot = (idx[:, None] == jnp.arange(rel_bias.shape[1])[None, :]).astype(F32)
    per_dist = jnp.einsum("dn,hn->hd", onehot, rel_bias.astype(F32) * LOG2E,
                          precision=lax.Precision.HIGHEST)
    n_h, span = per_dist.shape
    padded = jnp.pad(per_dist, ((0, 0), (0, 2)))
    skew = jnp.tile(padded, (1, TQ))[:, :TQ * (span + 1)].reshape(n_h, TQ, span + 1)
    table = skew[:, :, TQ - 1:TQ - 1 + nk]
    r = jnp.arange(TQ)[:, None]
    c = jnp.arange(nk)[None, :]
    qc = r // CHUNK
    kc = c // CHUNK
    lead = past // CHUNK - LEFT_CHUNKS
    band = (kc >= qc + lead) & (kc <= qc + lead + LEFT_CHUNKS)
    return jnp.where(band[None], table, NEG)


def _merge_kernel(x_ref, ya_ref, yc_ref, g_ref, wg_ref, bgate_ref, wpa_ref, wpc_ref, wo_ref,
                  gffn_ref, wrt_ref, brt_ref, h_ref, stage_ref, route_ref, cnt_ref, n2_scr, logit_scr):
    @pl.when(pl.program_id(0) == 0)
    def _():
        n2_scr[...] = jnp.zeros_like(n2_scr)
        logit_scr[...] = jnp.zeros_like(logit_scr)

    row8 = lax.broadcasted_iota(jnp.int32, (SUBLANES, TM), 0).astype(F32)
    ninf = -jnp.inf

    def argmax_first(vals):
        mx = vals.max(axis=0, keepdims=True)
        idx = jnp.where(vals == mx, row8, float(SUBLANES)).min(axis=0, keepdims=True)
        return mx, idx

    def route_tile(sub):
        rows = slice(sub * TM, (sub + 1) * TM)
        lt = logit_scr[rows, :].T
        gl = jnp.where(row8 < N_GROUPS, lt[N_EXPERTS:N_EXPERTS + SUBLANES, :], ninf)
        gmax, grp = argmax_first(gl)
        p_grp = 1.0 / jnp.exp(gl - gmax).sum(axis=0, keepdims=True)
        el = lt[0:EXPERTS_PER_GROUP, :]
        for g in range(1, N_GROUPS):
            el = jnp.where(grp == g, lt[g * EXPERTS_PER_GROUP:(g + 1) * EXPERTS_PER_GROUP, :], el)
        l1, i1 = argmax_first(el)
        l2, i2 = argmax_first(jnp.where(row8 == i1, ninf, el))
        e2 = jnp.exp(l2 - l1)
        den = 1.0 + e2
        w1 = p_grp * (1.0 / den)
        w2 = p_grp * (e2 / den)
        x1 = grp * EXPERTS_PER_GROUP + i1
        x2 = grp * EXPERTS_PER_GROUP + i2

        row_e = lax.broadcasted_iota(jnp.int32, (N_EXPERTS, TM), 0).astype(F32)
        oh1 = (row_e == x1).astype(F32)
        oh2 = (row_e == x2).astype(F32)
        oh = (oh1 + oh2).astype(BF16)
        r = lax.broadcasted_iota(jnp.int32, (TM, TM), 0)
        c = lax.broadcasted_iota(jnp.int32, (TM, TM), 1)
        earlier_tok = _dot(oh, (r < c).astype(BF16))
        er = lax.broadcasted_iota(jnp.int32, (N_EXPERTS, N_EXPERTS), 0)
        ec = lax.broadcasted_iota(jnp.int32, (N_EXPERTS, N_EXPERTS), 1)
        lower_exp = _dot((ec < er).astype(BF16), oh).sum(axis=1, keepdims=True)
        where = earlier_tok + lower_exp
        pos1 = (oh1 * where).sum(axis=0, keepdims=True)
        pos2 = (oh2 * where).sum(axis=0, keepdims=True)
        counts = lax.dot_general(jnp.ones((SUBLANES, TM), BF16), oh, (((1,), (1,)), ((), ())),
                                 preferred_element_type=F32)
        cnt_ref[sub] = jnp.concatenate([counts, jnp.zeros((SUBLANES, LANES - N_EXPERTS), F32)], axis=1)

        route_t = jnp.zeros((SUBLANES, TM), F32)
        for j, val in enumerate((x1, x2, w1, w2, pos1, pos2)):
            route_t = jnp.where(row8 == j, val, route_t)
        route_ref[rows, :] = jnp.concatenate([route_t, jnp.zeros((LANES - SUBLANES, TM), F32)], axis=0).T
        return pos1, pos2

    n2_old = [n2_scr[sub * TM:(sub + 1) * TM, :] for sub in range(MERGE_TILES)]
    positions = [route_tile(sub) for sub in range(MERGE_TILES)]

    x = x_ref[...]
    nb = _rms(x, g_ref[...]).astype(BF16)
    sga = _sigmoid(_dot(nb, wg_ref[:, 0:D_MODEL]) + bgate_ref[:, 0:D_MODEL])
    ma = sga * _dot(ya_ref[...], wpa_ref[...])
    sgc = _sigmoid(_dot(nb, wg_ref[:, D_MODEL:2 * D_MODEL]) + bgate_ref[:, D_MODEL:2 * D_MODEL])
    m = ma + sgc * _dot(yc_ref[...], wpc_ref[...])
    h = x + _dot(m.astype(BF16), wo_ref[...])
    h_ref[...] = h
    n2_new = _rms(h, gffn_ref[...]).astype(BF16)
    n2_scr[...] = n2_new
    logit_scr[...] = _dot(n2_new, wrt_ref[...]) + brt_ref[...]

    slot = lax.broadcasted_iota(jnp.int32, (TILE_ROWS, TM), 0).astype(F32)
    for sub, (pos1, pos2) in enumerate(positions):
        place = ((slot == pos1) | (slot == pos2)).astype(BF16)
        _pack_rows(stage_ref, _dot(place, n2_old[sub]), TILE_ROWS, is_bf16=True, first_row=sub * TILE_ROWS)


def _merge(x2, ya, yc, g_mix, wgate, bgate, wpa, wpc, wo, gffn, wrt, brt):
    t = x2.shape[0]
    n_tiles = t // TM
    n_steps = n_tiles // MERGE_TILES
    tmm = MERGE_TILES * TM
    const = lambda i: (0, 0)
    row = lambda i: (jnp.minimum(i, n_steps - 1), 0)
    late = lambda i: (jnp.maximum(i - 1, 0), 0)
    return pl.pallas_call(
        _merge_kernel,
        grid=(n_steps + 1,),
        in_specs=[
            pl.BlockSpec((tmm, D_MODEL), row),
            pl.BlockSpec((tmm, ATTN_W), row),
            pl.BlockSpec((tmm, CONV_W), row),
            pl.BlockSpec((1, D_MODEL), const),
            pl.BlockSpec((D_MODEL, 2 * D_MODEL), const),
            pl.BlockSpec((1, 2 * D_MODEL), const),
            pl.BlockSpec((ATTN_W, D_MODEL), const),
            pl.BlockSpec((CONV_W, D_MODEL), const),
            pl.BlockSpec((D_MODEL, D_MODEL), const),
            pl.BlockSpec((1, D_MODEL), const),
            pl.BlockSpec((D_MODEL, LANES), const),
            pl.BlockSpec((1, LANES), const),
        ],
        out_specs=[
            pl.BlockSpec((tmm, D_MODEL), row),
            pl.BlockSpec((MERGE_TILES * TILE_ROWS * PACK_ROWS, LANES), late),
            pl.BlockSpec((tmm, LANES), late),
            pl.BlockSpec((MERGE_TILES, SUBLANES, LANES), lambda i: (jnp.maximum(i - 1, 0), 0, 0)),
        ],
        out_shape=[
            jax.ShapeDtypeStruct((t, D_MODEL), F32),
            jax.ShapeDtypeStruct((n_tiles * TILE_ROWS * PACK_ROWS, LANES), U32),
            jax.ShapeDtypeStruct((t, LANES), F32),
            jax.ShapeDtypeStruct((n_tiles, SUBLANES, LANES), F32),
        ],
        scratch_shapes=[pltpu.VMEM((tmm, D_MODEL), BF16), pltpu.VMEM((tmm, LANES), F32)],
        compiler_params=pltpu.CompilerParams(
            dimension_semantics=("arbitrary",), vmem_limit_bytes=VMEM_LIMIT),
        name="merge",
    )(x2, ya, yc, g_mix, wgate, bgate, wpa, wpc, wo, gffn, wrt, brt)


def _piece_counts(n):
    return [(n >> (size.bit_length() - 1)) & 1 for size in SEG_SIZES]


ALL_CLASSES = tuple(range(len(SEG_SIZES)))
BIG_CLASSES = tuple(c for c in ALL_CLASSES if SEG_SIZES[c] >= BIG_PIECE)
SMALL_CLASSES = tuple(c for c in ALL_CLASSES if SEG_SIZES[c] < BIG_PIECE)


def _segment_pieces(n, visit, classes=ALL_CLASSES):
    for cls in classes:
        size = SEG_SIZES[cls]

        @pl.when((n & size) != 0)
        def _(cls=cls, size=size):
            visit(cls, n & ~(2 * size - 1))


def _piece_copy(src_ref, dst_ref, sems, cls, src_row, dst_row):
    n = SEG_SIZES[cls] * PACK_ROWS
    return pltpu.make_async_copy(src_ref.at[pl.ds(src_row * PACK_ROWS, n), :],
                                 dst_ref.at[pl.ds(dst_row * PACK_ROWS, n), :], sems.at[cls])


def _drain(src_ref, dst_ref, sems, counts, has_big):
    unroll = 4

    def wait_classes(classes):
        for cls in classes:
            def wait_some(k, cls=cls):
                def body(t, carry):
                    for _ in range(k):
                        _piece_copy(src_ref, dst_ref, sems, cls, 0, 0).wait()
                    return carry
                return body

            n = counts[cls]
            lax.fori_loop(0, n >> 2, wait_some(unroll), 0)
            lax.fori_loop(0, n & (unroll - 1), wait_some(1), 0)

    @pl.when(has_big)
    def _():
        wait_classes(BIG_CLASSES)

    wait_classes(SMALL_CLASSES)


def _experts_kernel(be_ref, nb_ref, base_ref, jlo_ref, jhi_ref, nv_ref, big_ref, npiece_ref, cum_ref, end_ref,
                    src_ref, stage_hbm, w1_ref, w3_ref, w2_ref, ys_ref, *scratch, n_tiles, n_blocks):
    xbufs = scratch[:GATHER_AHEAD + 1]
    w1b_ref, w3b_ref, w2b_ref, sems = scratch[GATHER_AHEAD + 1:]
    s = pl.program_id(0)
    nb = nb_ref[0]
    n_cls = len(SEG_SIZES)

    def segment_copies(step, j, live, buf, sem, classes=ALL_CLASSES):
        base = base_ref[step]
        g = jnp.minimum(j, n_tiles - 1) * N_EXPERTS + be_ref[step]
        lo = jnp.maximum(cum_ref[g], base)
        hi = jnp.minimum(end_ref[g], base + ROW_BLK)
        src = src_ref[g] + lo
        dst = lo - base
        _segment_pieces(jnp.where(live, jnp.maximum(hi - lo, 0), 0),
                        lambda cls, o: _piece_copy(stage_hbm, buf, sem, cls, src + o, dst + o).start(), classes)

    def looped_copies(step, j0, j1, buf, sem, classes=ALL_CLASSES):
        def body(j, carry):
            segment_copies(step, j, True, buf, sem, classes)
            return carry

        lax.fori_loop(j0, j1, body, 0)

    @pl.when(s == 0)
    def _():
        for buf in xbufs:
            buf[...] = jnp.zeros_like(buf)
        for first in range(GATHER_AHEAD):
            blk = min(first, n_blocks - 1)
            looped_copies(blk, jlo_ref[blk], jnp.where(first < nb, jhi_ref[blk], jlo_ref[blk]),
                          xbufs[first], sems.at[first])

    def step(cur, cur_sem, nxt, nxt_sem):
        prev = be_ref[jnp.maximum(s - 1, 0)]

        @pl.when((s == 0) | (be_ref[s] != prev))
        def _():
            w1b_ref[...] = w1_ref[0].astype(BF16)
            w3b_ref[...] = w3_ref[0].astype(BF16)
            w2b_ref[...] = w2_ref[0].astype(BF16)

        _drain(stage_hbm, cur, cur_sem, [npiece_ref[s * n_cls + c] for c in range(n_cls)], big_ref[s] != 0)

        nxt_step = jnp.minimum(s + GATHER_AHEAD, n_blocks - 1)
        live = s + GATHER_AHEAD < nb
        j0 = jlo_ref[nxt_step]
        j1 = jnp.where(live, jhi_ref[nxt_step], j0)
        looped_copies(nxt_step, j0 + GATHER_UNROLL, j1, nxt, nxt_sem, SMALL_CLASSES)
        looped_copies(nxt_step, j0, jnp.where(big_ref[nxt_step] != 0, j1, j0), nxt, nxt_sem, BIG_CLASSES)
        groups = iter(_split(range(GATHER_UNROLL), 2 * PACK_ROWS))

        def start_group():
            for k in next(groups):
                segment_copies(nxt_step, j0 + k, j0 + k < j1, nxt, nxt_sem, SMALL_CLASSES)

        a = None
        g = None
        for blk in range(PACK_ROWS):
            start_group()
            xa = _unpack_block(cur, ROW_BLK, blk, n_valid=nv_ref[s])
            rows = slice(blk * PACK_W, (blk + 1) * PACK_W)
            da = _dot(xa, w1b_ref[rows, :])
            dg = _dot(xa, w3b_ref[rows, :])
            a = da if a is None else a + da
            g = dg if g is None else g + dg
        hdn = ((a * _sigmoid(a)) * g).astype(BF16)
        for blk in range(PACK_ROWS):
            start_group()
            _pack_block(ys_ref, _dot(hdn, w2b_ref[:, blk * PACK_W:(blk + 1) * PACK_W]), ROW_BLK, blk)

    n_buf = len(xbufs)
    for slot in range(n_buf):
        @pl.when((s < nb) & (s % n_buf == slot))
        def _(slot=slot):
            ahead = (slot + GATHER_AHEAD) % n_buf
            step(xbufs[slot], sems.at[slot], xbufs[ahead], sems.at[ahead])

    @pl.when(s >= nb)
    def _():
        ys_ref[...] = jnp.zeros_like(ys_ref)


def _experts(blk_e, nblk, base, jlo, jhi, nvalid, big, npiece, cum, end, src, stage, w1, w3, w2):
    n_blocks = blk_e.shape[0]
    n_tiles = cum.shape[0] // N_EXPERTS

    def wsel(s, be, nb, *_):
        return (be[jnp.minimum(s, nb[0] - 1)], 0, 0)

    grid_spec = pltpu.PrefetchScalarGridSpec(
        num_scalar_prefetch=11,
        grid=(n_blocks,),
        in_specs=[
            pl.BlockSpec(memory_space=pl.ANY),
            pl.BlockSpec((1, D_MODEL, D_EXPERT), wsel),
            pl.BlockSpec((1, D_MODEL, D_EXPERT), wsel),
            pl.BlockSpec((1, D_EXPERT, D_MODEL), wsel),
        ],
        out_specs=pl.BlockSpec((ROW_BLK * PACK_ROWS, LANES), lambda s, *_: (s, 0)),
        scratch_shapes=[pltpu.VMEM((ROW_BLK * PACK_ROWS, LANES), U32)] * (GATHER_AHEAD + 1) + [
            pltpu.VMEM((D_MODEL, D_EXPERT), BF16),
            pltpu.VMEM((D_MODEL, D_EXPERT), BF16),
            pltpu.VMEM((D_EXPERT, D_MODEL), BF16),
            pltpu.SemaphoreType.DMA((GATHER_AHEAD + 1, len(SEG_SIZES))),
        ],
    )
    return pl.pallas_call(
        functools.partial(_experts_kernel, n_tiles=n_tiles, n_blocks=n_blocks),
        grid_spec=grid_spec,
        out_shape=jax.ShapeDtypeStruct((n_blocks * ROW_BLK * PACK_ROWS, LANES), U32),
        compiler_params=pltpu.CompilerParams(
            dimension_semantics=("arbitrary",), vmem_limit_bytes=VMEM_LIMIT),
        name="experts",
    )(blk_e, nblk, base, jlo, jhi, nvalid, big, npiece, cum, end, src, stage, w1, w3, w2)


def _combine_kernel(cnt_ref, off_ref, dst_ref, big_ref, npiece_ref, h_ref, route_ref, p_ref, gple_ref, wpg_ref,
                    bpg_ref, wpp_ref, ys_hbm, o_ref, *scratch, n_steps):
    ybufs, sems = scratch[:-1], scratch[-1]
    i = pl.program_id(0)
    n_cls = len(SEG_SIZES)

    def segment_copies(step, e, live, buf, sem, classes=ALL_CLASSES):
        g = step * N_EXPERTS + e
        off, dst = off_ref[g], dst_ref[g]
        _segment_pieces(jnp.where(live, cnt_ref[g], 0),
                        lambda cls, o: _piece_copy(ys_hbm, buf, sem, cls, dst + o, off + o).start(), classes)

    def looped_copies(step, n_experts, buf, sem, classes=ALL_CLASSES):
        def body(e, carry):
            segment_copies(step, e, True, buf, sem, classes)
            return carry

        lax.fori_loop(0, n_experts, body, 0)

    @pl.when(i == 0)
    def _():
        for first in range(min(GATHER_AHEAD, n_steps)):
            looped_copies(first, N_EXPERTS, ybufs[first], sems.at[first])

    def step(cur, cur_sem, nxt, nxt_sem):
        _drain(ys_hbm, cur, cur_sem, [npiece_ref[i * n_cls + c] for c in range(n_cls)], big_ref[i] != 0)
        nxt_step = jnp.minimum(i + GATHER_AHEAD, n_steps - 1)
        live = i + GATHER_AHEAD < n_steps
        looped_copies(nxt_step, jnp.where(live & (big_ref[nxt_step] != 0), N_EXPERTS, 0), nxt, nxt_sem, BIG_CLASSES)
        groups = iter(_split(range(N_EXPERTS), PACK_ROWS + 2))

        def start_group():
            for e in next(groups):
                segment_copies(nxt_step, e, live, nxt, nxt_sem, SMALL_CLASSES)

        start_group()
        pp = _dot(p_ref[...].astype(BF16), wpp_ref[...])
        route = route_ref[...]
        place = lax.broadcasted_iota(jnp.int32, (TM, TILE_ROWS), 1).astype(F32)
        sel = [(place == route[:, 4 + kk:5 + kk]).astype(BF16) for kk in range(TOP_K)]
        moe = []
        for blk in range(PACK_ROWS):
            start_group()
            cols = _unpack_block(cur, TILE_ROWS, blk)
            moe.append(_dot(sel[0], cols) * route[:, 2:3] + _dot(sel[1], cols) * route[:, 3:4])
        start_group()
        h = h_ref[...] + jnp.concatenate(moe, axis=1)
        gate = _sigmoid(_dot(_rms(h, gple_ref[...]).astype(BF16), wpg_ref[...]) + bpg_ref[...])
        o_ref[...] = h + gate * pp

    n_buf = len(ybufs)
    for slot in range(n_buf):
        @pl.when(i % n_buf == slot)
        def _(slot=slot):
            ahead = (slot + GATHER_AHEAD) % n_buf
            step(ybufs[slot], sems.at[slot], ybufs[ahead], sems.at[ahead])


def _combine(cnt, off, dst, big, npiece, h1, route, p2, gple, wpg, bpg, wpp, ys):
    t = h1.shape[0]
    n_steps = t // TM
    const = lambda i, *_: (0, 0)
    row = lambda i, *_: (i, 0)
    grid_spec = pltpu.PrefetchScalarGridSpec(
        num_scalar_prefetch=5,
        grid=(n_steps,),
        in_specs=[
            pl.BlockSpec((TM, D_MODEL), row),
            pl.BlockSpec((TM, LANES), row),
            pl.BlockSpec((TM, PLE_DIM), row),
            pl.BlockSpec((1, D_MODEL), const),
            pl.BlockSpec((D_MODEL, D_MODEL), const),
            pl.BlockSpec((1, D_MODEL), const),
            pl.BlockSpec((PLE_DIM, D_MODEL), const),
            pl.BlockSpec(memory_space=pl.ANY),
        ],
        out_specs=pl.BlockSpec((TM, D_MODEL), row),
        scratch_shapes=[pltpu.VMEM((TILE_ROWS * PACK_ROWS, LANES), U32)] * (GATHER_AHEAD + 1) + [
            pltpu.SemaphoreType.DMA((GATHER_AHEAD + 1, len(SEG_SIZES))),
        ],
    )
    return pl.pallas_call(
        functools.partial(_combine_kernel, n_steps=n_steps),
        grid_spec=grid_spec,
        out_shape=jax.ShapeDtypeStruct((t, D_MODEL), F32),
        compiler_params=pltpu.CompilerParams(
            dimension_semantics=("arbitrary",), vmem_limit_bytes=VMEM_LIMIT),
        name="combine",
    )(cnt, off, dst, big, npiece, h1, route, p2, gple, wpg, bpg, wpp, ys)


def _layer(h, p_i, g_mix, w_in, b_in, g_q, g_k, rel_bias, conv_w, conv_b, w_pa, w_pc, w_o,
           g_ffn, w_group, b_group, w_router, b_router, w1, w3, w2,
           g_ple, w_ple_gate, b_ple_gate, w_ple_proj):
    b, s, d = h.shape
    t = b * s
    x2 = h.reshape(t, d)
    row2 = lambda a: a.reshape(1, -1).astype(F32)

    qkv_w = 3 * ATTN_W
    conv_end = qkv_w + 3 * CONV_W
    w_in_b = w_in.astype(BF16)
    gq = row2(jnp.tile(g_q.astype(F32) * (HEAD_DIM ** -0.5 * LOG2E), N_HEADS))
    gk = row2(jnp.tile(g_k.astype(F32), N_HEADS))
    head = jnp.arange(ATTN_W) // HEAD_DIM
    hmat = jnp.where(head[:, None] == head[None, :], 1.0 / HEAD_DIM, 0.0).astype(BF16)
    cw = jnp.concatenate([conv_w.astype(F32), jnp.zeros((SUBLANES - CONV_K, CONV_W), F32)], axis=0)

    q, k, v, yc = _inproj(x2, row2(g_mix), w_in_b[:, :qkv_w], w_in_b[:, qkv_w:conv_end],
                          row2(b_in[:conv_end]), gq, gk, hmat, cw, row2(conv_b), s)

    ya = _attention(q.reshape(b, s, ATTN_W), k.reshape(b, s, ATTN_W), v.reshape(b, s, ATTN_W),
                    _attn_bias(rel_bias)).reshape(t, ATTN_W)

    n_pad = LANES - N_GROUPS - N_EXPERTS
    wrt = jnp.concatenate([w_router, w_group, jnp.zeros((d, n_pad), w_group.dtype)], axis=1).astype(BF16)
    brt = row2(jnp.concatenate([b_router, b_group, jnp.zeros((n_pad,), b_group.dtype)]))
    h1, stage, route, cnt_f = _merge(x2, ya, yc, row2(g_mix), w_in_b[:, conv_end:], row2(b_in[conv_end:]),
                                     w_pa.astype(BF16), w_pc.astype(BF16), w_o.astype(BF16),
                                     row2(g_ffn), wrt, brt)

    n_tiles = t // TM
    cnt = cnt_f[:, 0, :N_EXPERTS].astype(jnp.int32)
    tile_off = jnp.cumsum(cnt, axis=1) - cnt
    tot = cnt.sum(axis=0)
    pcounts = (tot + ROW_BLK - 1) // ROW_BLK * ROW_BLK
    pends = jnp.cumsum(pcounts)
    pstarts = pends - pcounts
    cum = jnp.cumsum(cnt, axis=0) - cnt
    dst = pstarts[None, :] + cum
    n_blocks = (t * TOP_K) // ROW_BLK + N_EXPERTS
    blk_start = jnp.arange(n_blocks, dtype=jnp.int32) * ROW_BLK
    blk_e = jnp.minimum((pends[None, :] <= blk_start[:, None]).sum(axis=1), N_EXPERTS - 1).astype(jnp.int32)
    nblk = (pends[-1:] // ROW_BLK).astype(jnp.int32)
    sel = (jnp.arange(N_EXPERTS, dtype=jnp.int32)[:, None] == blk_e[None, :]).astype(jnp.int32)
    of_block = lambda a: (a[..., None] * sel).sum(axis=-2)
    base = blk_start - of_block(pstarts)
    nvalid = jnp.clip(of_block(tot) - base, 0, ROW_BLK)
    cum_e = of_block(cum)
    cnt_e = of_block(cnt)
    jlo = (cum_e + cnt_e <= base[None, :]).sum(axis=0)
    jhi = (cum_e < base[None, :] + ROW_BLK).sum(axis=0)
    part = jnp.clip(jnp.minimum(cum_e + cnt_e, base[None, :] + ROW_BLK) - jnp.maximum(cum_e, base[None, :]),
                    0, ROW_BLK)
    blk_pieces = jnp.stack(_piece_counts(part), axis=-1).sum(axis=0)
    tile_pieces = jnp.stack(_piece_counts(cnt), axis=-1).sum(axis=1)
    src = jnp.arange(n_tiles, dtype=jnp.int32)[:, None] * TILE_ROWS + tile_off - cum
    flat = lambda a: a.reshape(-1).astype(jnp.int32)

    blk_big = (part >= BIG_PIECE).any(axis=0)
    tile_big = (cnt >= BIG_PIECE).any(axis=1)

    ys = _experts(blk_e, nblk, flat(base), flat(jlo), flat(jhi), flat(nvalid), flat(blk_big), flat(blk_pieces),
                  flat(cum), flat(cum + cnt), flat(src), stage, w1, w3, w2)
    out = _combine(flat(cnt), flat(tile_off), flat(dst), flat(tile_big), flat(tile_pieces), h1, route,
                   p_i.reshape(t, PLE_DIM), row2(g_ple), w_ple_gate.astype(BF16), row2(b_ple_gate),
                   w_ple_proj.astype(BF16), ys)
    return out.reshape(b, s, d)


def kernel(x, p, g_mix, w_in, b_in, g_q, g_k, rel_bias, conv_w, conv_b, w_pa, w_pc, w_o, g_ffn, w_group, b_group, w_router, b_router, w1, w3, w2, g_ple, w_ple_gate, b_ple_gate, w_ple_proj):
    h = x
    for i in range(p.shape[0]):
        h = _layer(h, p[i], g_mix[i], w_in[i], b_in[i], g_q[i], g_k[i], rel_bias[i], conv_w[i], conv_b[i],
                   w_pa[i], w_pc[i], w_o[i], g_ffn[i], w_group[i], b_group[i], w_router[i], b_router[i],
                   w1[i], w3[i], w2[i], g_ple[i], w_ple_gate[i], b_ple_gate[i], w_ple_proj[i])
    return h
```

```python
import functools

import jax
import jax.numpy as jnp
from jax import lax
from jax.experimental import pallas as pl
from jax.experimental.pallas import tpu as pltpu

D_MODEL = 1024
CHUNK = 64
LEFT_CHUNKS = 8
N_HEADS = 8
HEAD_DIM = 64
ATTN_W = N_HEADS * HEAD_DIM
CONV_W = D_MODEL // 2
CONV_K = 3
MAX_REL_PAST = 256
PLE_DIM = 256
N_GROUPS = 4
EXPERTS_PER_GROUP = 8
N_EXPERTS = N_GROUPS * EXPERTS_PER_GROUP
TOP_K = 2
D_EXPERT = 512
EPS = 1e-6
NEG = -1e30
LOG2E = 1.4426950408889634

LANES = 128
SUBLANES = 8
TM = 256
TI = 1024
MERGE_TILES = 4
TQ = 256
KV_SLABS = 1 + (LEFT_CHUNKS * CHUNK) // TQ
ROW_BLK = 512
TILE_ROWS = TOP_K * TM
PACK_ROWS = D_MODEL // (2 * LANES)
PACK_W = 2 * LANES
SEG_SIZES = tuple(TM >> k for k in range(TM.bit_length()))
BIG_PIECE = 32
ATTN_BATCH = 2
SCORE_AHEAD = 2
GATHER_AHEAD = 2
GATHER_UNROLL = 36
VMEM_LIMIT = 56 * 1024 * 1024

F32 = jnp.float32
BF16 = jnp.bfloat16
U32 = jnp.uint32


def _dot(a, b):
    return jnp.dot(a, b, preferred_element_type=F32)


def _rms(x, g):
    ms = jnp.mean(x * x, axis=-1, keepdims=True)
    return (x * lax.rsqrt(ms + EPS)) * g


def _sigmoid(x):
    return 1.0 / (1.0 + jnp.exp(-x))


def _pack_block(ref, vals, n_rows, a, is_bf16=False, first_row=0):
    lo = vals[:, 0:LANES]
    hi = vals[:, LANES:PACK_W]
    if not is_bf16:
        lo = lo.astype(BF16).astype(F32)
        hi = hi.astype(BF16).astype(F32)
    ref[pl.ds(first_row * PACK_ROWS + a, n_rows, stride=PACK_ROWS), :] = (
        lax.bitcast_convert_type(hi, U32) | (lax.bitcast_convert_type(lo, U32) >> 16))


def _pack_rows(ref, vals, n_rows, is_bf16=False, first_row=0):
    for a in range(PACK_ROWS):
        _pack_block(ref, vals[:, a * PACK_W:(a + 1) * PACK_W], n_rows, a, is_bf16, first_row)


def _unpack_block(ref, n_rows, a, n_valid=None):
    word = ref[pl.ds(a, n_rows, stride=PACK_ROWS), :]
    if n_valid is not None:
        word = jnp.where(lax.broadcasted_iota(jnp.int32, (n_rows, LANES), 0) < n_valid, word, U32(0))
    lo = lax.bitcast_convert_type(word << 16, F32).astype(BF16)
    hi = lax.bitcast_convert_type(word & U32(0xFFFF0000), F32).astype(BF16)
    return jnp.concatenate([lo, hi], axis=1)


def _split(items, n_groups):
    items = list(items)
    return [items[len(items) * g // n_groups:len(items) * (g + 1) // n_groups] for g in range(n_groups)]


def _inproj_kernel(x_ref, g_ref, wqkv_ref, wconv_ref, b_ref, gq_ref, gk_ref, hm_ref,
                   cw_ref, cb_ref, q_ref, k_ref, v_ref, yc_ref, carry_ref, *, tiles_per_seq):
    i = pl.program_id(0)
    nb = _rms(x_ref[...], g_ref[...]).astype(BF16)

    zq = _dot(nb, wqkv_ref[...]) + b_ref[:, 0:3 * ATTN_W]
    hm = hm_ref[...]

    def head_rms(t, g):
        ms = _dot((t * t).astype(BF16), hm)
        return (t * lax.rsqrt(ms + EPS)) * g

    q_ref[...] = head_rms(zq[:, 0:ATTN_W], gq_ref[...]).astype(BF16)
    k_ref[...] = head_rms(zq[:, ATTN_W:2 * ATTN_W], gk_ref[...]).astype(BF16)
    v_ref[...] = zq[:, 2 * ATTN_W:3 * ATTN_W].astype(BF16)

    zc = _dot(nb, wconv_ref[...]) + b_ref[:, 3 * ATTN_W:3 * ATTN_W + 3 * CONV_W]
    u = zc[:, 0:CONV_W]
    bg = zc[:, CONV_W:2 * CONV_W]
    cg = zc[:, 2 * CONV_W:3 * CONV_W]
    cu = cg * u

    @pl.when((i % tiles_per_seq) == 0)
    def _():
        carry_ref[...] = jnp.zeros_like(carry_ref)

    prev = carry_ref[...]
    carry_ref[...] = cu[TI - SUBLANES:TI, :]
    row = lax.broadcasted_iota(jnp.int32, (SUBLANES, CONV_W), 0)

    def shifted(s):
        r = pltpu.roll(cu, s, 0)
        p = pltpu.roll(prev, s, 0)
        top = jnp.where(row < s, p, r[0:SUBLANES, :])
        return jnp.concatenate([top, r[SUBLANES:, :]], axis=0)

    y = cb_ref[...] + cw_ref[0:1, :] * shifted(2)
    y = y + cw_ref[1:2, :] * shifted(1)
    y = y + cw_ref[2:3, :] * cu
    yc_ref[...] = (bg * y).astype(BF16)


def _inproj(x2, g_mix, wqkv, wconv, b_in, gq, gk, hmat, cw, cb, seq):
    t = x2.shape[0]
    const = lambda i: (0, 0)
    row = lambda i: (i, 0)
    out = jax.ShapeDtypeStruct((t, ATTN_W), BF16)
    return pl.pallas_call(
        functools.partial(_inproj_kernel, tiles_per_seq=seq // TI),
        grid=(t // TI,),
        in_specs=[
            pl.BlockSpec((TI, D_MODEL), row),
            pl.BlockSpec((1, D_MODEL), const),
            pl.BlockSpec((D_MODEL, 3 * ATTN_W), const),
            pl.BlockSpec((D_MODEL, 3 * CONV_W), const),
            pl.BlockSpec((1, 3 * ATTN_W + 3 * CONV_W), const),
            pl.BlockSpec((1, ATTN_W), const),
            pl.BlockSpec((1, ATTN_W), const),
            pl.BlockSpec((ATTN_W, ATTN_W), const),
            pl.BlockSpec((SUBLANES, CONV_W), const),
            pl.BlockSpec((1, CONV_W), const),
        ],
        out_specs=[pl.BlockSpec((TI, ATTN_W), row)] * 4,
        out_shape=[out] * 4,
        scratch_shapes=[pltpu.VMEM((SUBLANES, CONV_W), F32)],
        compiler_params=pltpu.CompilerParams(
            dimension_semantics=("arbitrary",), vmem_limit_bytes=VMEM_LIMIT),
        name="inproj",
    )(x2, g_mix, wqkv, wconv, b_in, gq, gk, hmat, cw, cb)


def _lane_fold(parts, op):
    acc = None
    for a in parts:
        for c in range(0, a.shape[1], LANES):
            piece = a[:, c:c + LANES]
            acc = piece if acc is None else op(acc, piece)
    return acc


def _attn_kernel(q_ref, k0_ref, k1_ref, k2_ref, v0_ref, v1_ref, v2_ref, bias_ref, o_ref):
    k_refs = (k0_ref, k1_ref, k2_ref)
    v_refs = (v0_ref, v1_ref, v2_ref)
    pair_w = 2 * HEAD_DIM
    lane = lax.broadcasted_iota(jnp.int32, (TQ, pair_w), 1)
    low = lane < HEAD_DIM

    def scores(bb, h, pens):
        ps = slice((h // 2) * pair_w, (h // 2 + 1) * pair_w)
        q_pair = q_ref[bb, :, ps]
        own = low if h % 2 == 0 else jnp.logical_not(low)
        qh = jnp.where(own, q_pair, jnp.zeros_like(q_pair))
        s = [lax.dot_general(qh, k_refs[j][bb, :, ps], (((1,), (1,)), ((), ())),
                             preferred_element_type=F32) + bias_ref[h, :, j * TQ:(j + 1) * TQ]
             for j in range(KV_SLABS)]
        return s if pens is None else [sj + pens[j] for j, sj in enumerate(s)]

    def weighted(bb, h, s):
        ps = slice((h // 2) * pair_w, (h // 2 + 1) * pair_w)
        m = _lane_fold(s, jnp.maximum).max(axis=-1, keepdims=True)
        e = [jnp.exp2(sj - m) for sj in s]
        l = _lane_fold(e, jnp.add).sum(axis=-1, keepdims=True)
        acc = None
        for j in range(KV_SLABS):
            oj = _dot(e[j].astype(BF16), v_refs[j][bb, :, ps])
            acc = oj if acc is None else acc + oj
        return acc * (1.0 / l)

    def all_heads(pens):
        items = [(bb, h) for h in range(N_HEADS) for bb in range(ATTN_BATCH)]
        pending = [scores(bb, h, pens) for bb, h in items[:SCORE_AHEAD]]
        o_even = {}
        for n, (bb, h) in enumerate(items):
            if n + SCORE_AHEAD < len(items):
                pending.append(scores(*items[n + SCORE_AHEAD], pens))
            o = weighted(bb, h, pending.pop(0))
            if h % 2 == 0:
                o_even[bb] = o
            else:
                ps = slice((h // 2) * pair_w, (h // 2 + 1) * pair_w)
                o_ref[bb, :, ps] = jnp.where(low, o_even[bb], o).astype(BF16)

    i = pl.program_id(1)

    @pl.when(i >= KV_SLABS - 1)
    def _():
        all_heads(None)

    @pl.when(i < KV_SLABS - 1)
    def _():
        all_heads([jnp.where(i >= KV_SLABS - 1 - j, 0.0, NEG).astype(F32) for j in range(KV_SLABS)])


def _attention(q, k, v, bias):
    b, s, _ = q.shape
    blk = (ATTN_BATCH, TQ, ATTN_W)

    def kv_map(j):
        back = KV_SLABS - 1 - j
        return lambda bi, i: (bi, jnp.maximum(i - back, 0), 0)

    kv_specs = [pl.BlockSpec(blk, kv_map(j)) for j in range(KV_SLABS)]
    return pl.pallas_call(
        _attn_kernel,
        grid=(b // ATTN_BATCH, s // TQ),
        in_specs=[pl.BlockSpec(blk, lambda bi, i: (bi, i, 0))] + kv_specs + kv_specs + [
            pl.BlockSpec((N_HEADS, TQ, KV_SLABS * TQ), lambda bi, i: (0, 0, 0))],
        out_specs=pl.BlockSpec(blk, lambda bi, i: (bi, i, 0)),
        out_shape=jax.ShapeDtypeStruct((b, s, ATTN_W), BF16),
        compiler_params=pltpu.CompilerParams(
            dimension_semantics=("arbitrary", "arbitrary"), vmem_limit_bytes=VMEM_LIMIT),
        name="attn",
    )(q, k, k, k, v, v, v, bias)


def _attn_bias(rel_bias):
    nk = KV_SLABS * TQ
    past = nk - TQ
    d = jnp.arange(TQ - 1 + past, -TQ, -1)
    idx = jnp.clip(d, -(CHUNK - 1), MAX_REL_PAST) + (CHUNK - 1)
    onehot = (idx[:, None] == jnp.arange(rel_bias.shape[1])[None, :]).astype(F32)
    per_dist = jnp.einsum("dn,hn->hd", onehot, rel_bias.astype(F32) * LOG2E,
                          precision=lax.Precision.HIGHEST)
    n_h, span = per_dist.shape
    padded = jnp.pad(per_dist, ((0, 0), (0, 2)))
    skew = jnp.tile(padded, (1, TQ))[:, :TQ * (span + 1)].reshape(n_h, TQ, span + 1)
    table = skew[:, :, TQ - 1:TQ - 1 + nk]
    r = jnp.arange(TQ)[:, None]
    c = jnp.arange(nk)[None, :]
    qc = r // CHUNK
    kc = c // CHUNK
    lead = past // CHUNK - LEFT_CHUNKS
    band = (kc >= qc + lead) & (kc <= qc + lead + LEFT_CHUNKS)
    return jnp.where(band[None], table, NEG)


def _merge_kernel(x_ref, ya_ref, yc_ref, g_ref, wg_ref, bgate_ref, wpa_ref, wpc_ref, wo_ref,
                  gffn_ref, wrt_ref, brt_ref, h_ref, stage_ref, route_ref, cnt_ref, n2_scr, logit_scr):
    @pl.when(pl.program_id(0) == 0)
    def _():
        n2_scr[...] = jnp.zeros_like(n2_scr)
        logit_scr[...] = jnp.zeros_like(logit_scr)

    row8 = lax.broadcasted_iota(jnp.int32, (SUBLANES, TM), 0).astype(F32)
    ninf = -jnp.inf

    def argmax_first(vals):
        mx = vals.max(axis=0, keepdims=True)
        idx = jnp.where(vals == mx, row8, float(SUBLANES)).min(axis=0, keepdims=True)
        return mx, idx

    def route_tile(sub):
        rows = slice(sub * TM, (sub + 1) * TM)
        lt = logit_scr[rows, :].T
        gl = jnp.where(row8 < N_GROUPS, lt[N_EXPERTS:N_EXPERTS + SUBLANES, :], ninf)
        gmax, grp = argmax_first(gl)
        p_grp = 1.0 / jnp.exp(gl - gmax).sum(axis=0, keepdims=True)
        el = lt[0:EXPERTS_PER_GROUP, :]
        for g in range(1, N_GROUPS):
            el = jnp.where(grp == g, lt[g * EXPERTS_PER_GROUP:(g + 1) * EXPERTS_PER_GROUP, :], el)
        l1, i1 = argmax_first(el)
        l2, i2 = argmax_first(jnp.where(row8 == i1, ninf, el))
        e2 = jnp.exp(l2 - l1)
        den = 1.0 + e2
        w1 = p_grp * (1.0 / den)
        w2 = p_grp * (e2 / den)
        x1 = grp * EXPERTS_PER_GROUP + i1
        x2 = grp * EXPERTS_PER_GROUP + i2

        row_e = lax.broadcasted_iota(jnp.int32, (N_EXPERTS, TM), 0).astype(F32)
        oh1 = (row_e == x1).astype(F32)
        oh2 = (row_e == x2).astype(F32)
        oh = (oh1 + oh2).astype(BF16)
        r = lax.broadcasted_iota(jnp.int32, (TM, TM), 0)
        c = lax.broadcasted_iota(jnp.int32, (TM, TM), 1)
        earlier_tok = _dot(oh, (r < c).astype(BF16))
        er = lax.broadcasted_iota(jnp.int32, (N_EXPERTS, N_EXPERTS), 0)
        ec = lax.broadcasted_iota(jnp.int32, (N_EXPERTS, N_EXPERTS), 1)
        lower_exp = _dot((ec < er).astype(BF16), oh).sum(axis=1, keepdims=True)
        where = earlier_tok + lower_exp
        pos1 = (oh1 * where).sum(axis=0, keepdims=True)
        pos2 = (oh2 * where).sum(axis=0, keepdims=True)
        counts = lax.dot_general(jnp.ones((SUBLANES, TM), BF16), oh, (((1,), (1,)), ((), ())),
                                 preferred_element_type=F32)
        cnt_ref[sub] = jnp.concatenate([counts, jnp.zeros((SUBLANES, LANES - N_EXPERTS), F32)], axis=1)

        route_t = jnp.zeros((SUBLANES, TM), F32)
        for j, val in enumerate((x1, x2, w1, w2, pos1, pos2)):
            route_t = jnp.where(row8 == j, val, route_t)
        route_ref[rows, :] = jnp.concatenate([route_t, jnp.zeros((LANES - SUBLANES, TM), F32)], axis=0).T
        return pos1, pos2

    n2_old = [n2_scr[sub * TM:(sub + 1) * TM, :] for sub in range(MERGE_TILES)]
    positions = [route_tile(sub) for sub in range(MERGE_TILES)]

    x = x_ref[...]
    nb = _rms(x, g_ref[...]).astype(BF16)
    sga = _sigmoid(_dot(nb, wg_ref[:, 0:D_MODEL]) + bgate_ref[:, 0:D_MODEL])
    ma = sga * _dot(ya_ref[...], wpa_ref[...])
    sgc = _sigmoid(_dot(nb, wg_ref[:, D_MODEL:2 * D_MODEL]) + bgate_ref[:, D_MODEL:2 * D_MODEL])
    m = ma + sgc * _dot(yc_ref[...], wpc_ref[...])
    h = x + _dot(m.astype(BF16), wo_ref[...])
    h_ref[...] = h
    n2_new = _rms(h, gffn_ref[...]).astype(BF16)
    n2_scr[...] = n2_new
    logit_scr[...] = _dot(n2_new, wrt_ref[...]) + brt_ref[...]

    slot = lax.broadcasted_iota(jnp.int32, (TILE_ROWS, TM), 0).astype(F32)
    for sub, (pos1, pos2) in enumerate(positions):
        place = ((slot == pos1) | (slot == pos2)).astype(BF16)
        _pack_rows(stage_ref, _dot(place, n2_old[sub]), TILE_ROWS, is_bf16=True, first_row=sub * TILE_ROWS)


def _merge(x2, ya, yc, g_mix, wgate, bgate, wpa, wpc, wo, gffn, wrt, brt):
    t = x2.shape[0]
    n_tiles = t // TM
    n_steps = n_tiles // MERGE_TILES
    tmm = MERGE_TILES * TM
    const = lambda i: (0, 0)
    row = lambda i: (jnp.minimum(i, n_steps - 1), 0)
    late = lambda i: (jnp.maximum(i - 1, 0), 0)
    return pl.pallas_call(
        _merge_kernel,
        grid=(n_steps + 1,),
        in_specs=[
            pl.BlockSpec((tmm, D_MODEL), row),
            pl.BlockSpec((tmm, ATTN_W), row),
            pl.BlockSpec((tmm, CONV_W), row),
            pl.BlockSpec((1, D_MODEL), const),
            pl.BlockSpec((D_MODEL, 2 * D_MODEL), const),
            pl.BlockSpec((1, 2 * D_MODEL), const),
            pl.BlockSpec((ATTN_W, D_MODEL), const),
            pl.BlockSpec((CONV_W, D_MODEL), const),
            pl.BlockSpec((D_MODEL, D_MODEL), const),
            pl.BlockSpec((1, D_MODEL), const),
            pl.BlockSpec((D_MODEL, LANES), const),
            pl.BlockSpec((1, LANES), const),
        ],
        out_specs=[
            pl.BlockSpec((tmm, D_MODEL), row),
            pl.BlockSpec((MERGE_TILES * TILE_ROWS * PACK_ROWS, LANES), late),
            pl.BlockSpec((tmm, LANES), late),
            pl.BlockSpec((MERGE_TILES, SUBLANES, LANES), lambda i: (jnp.maximum(i - 1, 0), 0, 0)),
        ],
        out_shape=[
            jax.ShapeDtypeStruct((t, D_MODEL), F32),
            jax.ShapeDtypeStruct((n_tiles * TILE_ROWS * PACK_ROWS, LANES), U32),
            jax.ShapeDtypeStruct((t, LANES), F32),
            jax.ShapeDtypeStruct((n_tiles, SUBLANES, LANES), F32),
        ],
        scratch_shapes=[pltpu.VMEM((tmm, D_MODEL), BF16), pltpu.VMEM((tmm, LANES), F32)],
        compiler_params=pltpu.CompilerParams(
            dimension_semantics=("arbitrary",), vmem_limit_bytes=VMEM_LIMIT),
        name="merge",
    )(x2, ya, yc, g_mix, wgate, bgate, wpa, wpc, wo, gffn, wrt, brt)


def _piece_counts(n):
    return [(n >> (size.bit_length() - 1)) & 1 for size in SEG_SIZES]


ALL_CLASSES = tuple(range(len(SEG_SIZES)))
BIG_CLASSES = tuple(c for c in ALL_CLASSES if SEG_SIZES[c] >= BIG_PIECE)
SMALL_CLASSES = tuple(c for c in ALL_CLASSES if SEG_SIZES[c] < BIG_PIECE)


def _segment_pieces(n, visit, classes=ALL_CLASSES):
    for cls in classes:
        size = SEG_SIZES[cls]

        @pl.when((n & size) != 0)
        def _(cls=cls, size=size):
            visit(cls, n & ~(2 * size - 1))


def _piece_copy(src_ref, dst_ref, sems, cls, src_row, dst_row):
    n = SEG_SIZES[cls] * PACK_ROWS
    return pltpu.make_async_copy(src_ref.at[pl.ds(src_row * PACK_ROWS, n), :],
                                 dst_ref.at[pl.ds(dst_row * PACK_ROWS, n), :], sems.at[cls])


def _drain(src_ref, dst_ref, sems, counts, has_big):
    unroll = 4

    def wait_classes(classes):
        for cls in classes:
            def wait_some(k, cls=cls):
                def body(t, carry):
                    for _ in range(k):
                        _piece_copy(src_ref, dst_ref, sems, cls, 0, 0).wait()
                    return carry
                return body

            n = counts[cls]
            lax.fori_loop(0, n >> 2, wait_some(unroll), 0)
            lax.fori_loop(0, n & (unroll - 1), wait_some(1), 0)

    @pl.when(has_big)
    def _():
        wait_classes(BIG_CLASSES)

    wait_classes(SMALL_CLASSES)


def _experts_kernel(be_ref, nb_ref, base_ref, jlo_ref, jhi_ref, nv_ref, big_ref, npiece_ref, cum_ref, end_ref,
                    src_ref, stage_hbm, w1_ref, w3_ref, w2_ref, ys_ref, *scratch, n_tiles, n_blocks):
    xbufs = scratch[:GATHER_AHEAD + 1]
    w1b_ref, w3b_ref, w2b_ref, sems = scratch[GATHER_AHEAD + 1:]
    s = pl.program_id(0)
    nb = nb_ref[0]
    n_cls = len(SEG_SIZES)

    def segment_copies(step, j, live, buf, sem, classes=ALL_CLASSES):
        base = base_ref[step]
        g = jnp.minimum(j, n_tiles - 1) * N_EXPERTS + be_ref[step]
        lo = jnp.maximum(cum_ref[g], base)
        hi = jnp.minimum(end_ref[g], base + ROW_BLK)
        src = src_ref[g] + lo
        dst = lo - base
        _segment_pieces(jnp.where(live, jnp.maximum(hi - lo, 0), 0),
                        lambda cls, o: _piece_copy(stage_hbm, buf, sem, cls, src + o, dst + o).start(), classes)

    def looped_copies(step, j0, j1, buf, sem, classes=ALL_CLASSES):
        def body(j, carry):
            segment_copies(step, j, True, buf, sem, classes)
            return carry

        lax.fori_loop(j0, j1, body, 0)

    @pl.when(s == 0)
    def _():
        for buf in xbufs:
            buf[...] = jnp.zeros_like(buf)
        for first in range(GATHER_AHEAD):
            blk = min(first, n_blocks - 1)
            looped_copies(blk, jlo_ref[blk], jnp.where(first < nb, jhi_ref[blk], jlo_ref[blk]),
                          xbufs[first], sems.at[first])

    def step(cur, cur_sem, nxt, nxt_sem):
        prev = be_ref[jnp.maximum(s - 1, 0)]

        @pl.when((s == 0) | (be_ref[s] != prev))
        def _():
            w1b_ref[...] = w1_ref[0].astype(BF16)
            w3b_ref[...] = w3_ref[0].astype(BF16)
            w2b_ref[...] = w2_ref[0].astype(BF16)

        _drain(stage_hbm, cur, cur_sem, [npiece_ref[s * n_cls + c] for c in range(n_cls)], big_ref[s] != 0)

        nxt_step = jnp.minimum(s + GATHER_AHEAD, n_blocks - 1)
        live = s + GATHER_AHEAD < nb
        j0 = jlo_ref[nxt_step]
        j1 = jnp.where(live, jhi_ref[nxt_step], j0)
        looped_copies(nxt_step, j0 + GATHER_UNROLL, j1, nxt, nxt_sem, SMALL_CLASSES)
        looped_copies(nxt_step, j0, jnp.where(big_ref[nxt_step] != 0, j1, j0), nxt, nxt_sem, BIG_CLASSES)
        groups = iter(_split(range(GATHER_UNROLL), 2 * PACK_ROWS))

        def start_group():
            for k in next(groups):
                segment_copies(nxt_step, j0 + k, j0 + k < j1, nxt, nxt_sem, SMALL_CLASSES)

        a = None
        g = None
        for blk in range(PACK_ROWS):
            start_group()
            xa = _unpack_block(cur, ROW_BLK, blk, n_valid=nv_ref[s])
            rows = slice(blk * PACK_W, (blk + 1) * PACK_W)
            da = _dot(xa, w1b_ref[rows, :])
            dg = _dot(xa, w3b_ref[rows, :])
            a = da if a is None else a + da
            g = dg if g is None else g + dg
        hdn = ((a * _sigmoid(a)) * g).astype(BF16)
        for blk in range(PACK_ROWS):
            start_group()
            _pack_block(ys_ref, _dot(hdn, w2b_ref[:, blk * PACK_W:(blk + 1) * PACK_W]), ROW_BLK, blk)

    n_buf = len(xbufs)
    for slot in range(n_buf):
        @pl.when((s < nb) & (s % n_buf == slot))
        def _(slot=slot):
            ahead = (slot + GATHER_AHEAD) % n_buf
            step(xbufs[slot], sems.at[slot], xbufs[ahead], sems.at[ahead])

    @pl.when(s >= nb)
    def _():
        ys_ref[...] = jnp.zeros_like(ys_ref)


def _experts(blk_e, nblk, base, jlo, jhi, nvalid, big, npiece, cum, end, src, stage, w1, w3, w2):
    n_blocks = blk_e.shape[0]
    n_tiles = cum.shape[0] // N_EXPERTS

    def wsel(s, be, nb, *_):
        return (be[jnp.minimum(s, nb[0] - 1)], 0, 0)

    grid_spec = pltpu.PrefetchScalarGridSpec(
        num_scalar_prefetch=11,
        grid=(n_blocks,),
        in_specs=[
            pl.BlockSpec(memory_space=pl.ANY),
            pl.BlockSpec((1, D_MODEL, D_EXPERT), wsel),
            pl.BlockSpec((1, D_MODEL, D_EXPERT), wsel),
            pl.BlockSpec((1, D_EXPERT, D_MODEL), wsel),
        ],
        out_specs=pl.BlockSpec((ROW_BLK * PACK_ROWS, LANES), lambda s, *_: (s, 0)),
        scratch_shapes=[pltpu.VMEM((ROW_BLK * PACK_ROWS, LANES), U32)] * (GATHER_AHEAD + 1) + [
            pltpu.VMEM((D_MODEL, D_EXPERT), BF16),
            pltpu.VMEM((D_MODEL, D_EXPERT), BF16),
            pltpu.VMEM((D_EXPERT, D_MODEL), BF16),
            pltpu.SemaphoreType.DMA((GATHER_AHEAD + 1, len(SEG_SIZES))),
        ],
    )
    return pl.pallas_call(
        functools.partial(_experts_kernel, n_tiles=n_tiles, n_blocks=n_blocks),
        grid_spec=grid_spec,
        out_shape=jax.ShapeDtypeStruct((n_blocks * ROW_BLK * PACK_ROWS, LANES), U32),
        compiler_params=pltpu.CompilerParams(
            dimension_semantics=("arbitrary",), vmem_limit_bytes=VMEM_LIMIT),
        name="experts",
    )(blk_e, nblk, base, jlo, jhi, nvalid, big, npiece, cum, end, src, stage, w1, w3, w2)


def _combine_kernel(cnt_ref, off_ref, dst_ref, big_ref, npiece_ref, h_ref, route_ref, p_ref, gple_ref, wpg_ref,
                    bpg_ref, wpp_ref, ys_hbm, o_ref, *scratch, n_steps):
    ybufs, sems = scratch[:-1], scratch[-1]
    i = pl.program_id(0)
    n_cls = len(SEG_SIZES)

    def segment_copies(step, e, live, buf, sem, classes=ALL_CLASSES):
        g = step * N_EXPERTS + e
        off, dst = off_ref[g], dst_ref[g]
        _segment_pieces(jnp.where(live, cnt_ref[g], 0),
                        lambda cls, o: _piece_copy(ys_hbm, buf, sem, cls, dst + o, off + o).start(), classes)

    def looped_copies(step, n_experts, buf, sem, classes=ALL_CLASSES):
        def body(e, carry):
            segment_copies(step, e, True, buf, sem, classes)
            return carry

        lax.fori_loop(0, n_experts, body, 0)

    @pl.when(i == 0)
    def _():
        for first in range(min(GATHER_AHEAD, n_steps)):
            looped_copies(first, N_EXPERTS, ybufs[first], sems.at[first])

    def step(cur, cur_sem, nxt, nxt_sem):
        _drain(ys_hbm, cur, cur_sem, [npiece_ref[i * n_cls + c] for c in range(n_cls)], big_ref[i] != 0)
        nxt_step = jnp.minimum(i + GATHER_AHEAD, n_steps - 1)
        live = i + GATHER_AHEAD < n_steps
        looped_copies(nxt_step, jnp.where(live & (big_ref[nxt_step] != 0), N_EXPERTS, 0), nxt, nxt_sem, BIG_CLASSES)
        groups = iter(_split(range(N_EXPERTS), PACK_ROWS + 2))

        def start_group():
            for e in next(groups):
                segment_copies(nxt_step, e, live, nxt, nxt_sem, SMALL_CLASSES)

        start_group()
        pp = _dot(p_ref[...].astype(BF16), wpp_ref[...])
        route = route_ref[...]
        place = lax.broadcasted_iota(jnp.int32, (TM, TILE_ROWS), 1).astype(F32)
        sel = [(place == route[:, 4 + kk:5 + kk]).astype(BF16) for kk in range(TOP_K)]
        moe = []
        for blk in range(PACK_ROWS):
            start_group()
            cols = _unpack_block(cur, TILE_ROWS, blk)
            moe.append(_dot(sel[0], cols) * route[:, 2:3] + _dot(sel[1], cols) * route[:, 3:4])
        start_group()
        h = h_ref[...] + jnp.concatenate(moe, axis=1)
        gate = _sigmoid(_dot(_rms(h, gple_ref[...]).astype(BF16), wpg_ref[...]) + bpg_ref[...])
        o_ref[...] = h + gate * pp

    n_buf = len(ybufs)
    for slot in range(n_buf):
        @pl.when(i % n_buf == slot)
        def _(slot=slot):
            ahead = (slot + GATHER_AHEAD) % n_buf
            step(ybufs[slot], sems.at[slot], ybufs[ahead], sems.at[ahead])


def _combine(cnt, off, dst, big, npiece, h1, route, p2, gple, wpg, bpg, wpp, ys):
    t = h1.shape[0]
    n_steps = t // TM
    const = lambda i, *_: (0, 0)
    row = lambda i, *_: (i, 0)
    grid_spec = pltpu.PrefetchScalarGridSpec(
        num_scalar_prefetch=5,
        grid=(n_steps,),
        in_specs=[
            pl.BlockSpec((TM, D_MODEL), row),
            pl.BlockSpec((TM, LANES), row),
            pl.BlockSpec((TM, PLE_DIM), row),
            pl.BlockSpec((1, D_MODEL), const),
            pl.BlockSpec((D_MODEL, D_MODEL), const),
            pl.BlockSpec((1, D_MODEL), const),
            pl.BlockSpec((PLE_DIM, D_MODEL), const),
            pl.BlockSpec(memory_space=pl.ANY),
        ],
        out_specs=pl.BlockSpec((TM, D_MODEL), row),
        scratch_shapes=[pltpu.VMEM((TILE_ROWS * PACK_ROWS, LANES), U32)] * (GATHER_AHEAD + 1) + [
            pltpu.SemaphoreType.DMA((GATHER_AHEAD + 1, len(SEG_SIZES))),
        ],
    )
    return pl.pallas_call(
        functools.partial(_combine_kernel, n_steps=n_steps),
        grid_spec=grid_spec,
        out_shape=jax.ShapeDtypeStruct((t, D_MODEL), F32),
        compiler_params=pltpu.CompilerParams(
            dimension_semantics=("arbitrary",), vmem_limit_bytes=VMEM_LIMIT),
        name="combine",
    )(cnt, off, dst, big, npiece, h1, route, p2, gple, wpg, bpg, wpp, ys)


def _layer(h, p_i, g_mix, w_in, b_in, g_q, g_k, rel_bias, conv_w, conv_b, w_pa, w_pc, w_o,
           g_ffn, w_group, b_group, w_router, b_router, w1, w3, w2,
           g_ple, w_ple_gate, b_ple_gate, w_ple_proj):
    b, s, d = h.shape
    t = b * s
    x2 = h.reshape(t, d)
    row2 = lambda a: a.reshape(1, -1).astype(F32)

    qkv_w = 3 * ATTN_W
    conv_end = qkv_w + 3 * CONV_W
    w_in_b = w_in.astype(BF16)
    gq = row2(jnp.tile(g_q.astype(F32) * (HEAD_DIM ** -0.5 * LOG2E), N_HEADS))
    gk = row2(jnp.tile(g_k.astype(F32), N_HEADS))
    head = jnp.arange(ATTN_W) // HEAD_DIM
    hmat = jnp.where(head[:, None] == head[None, :], 1.0 / HEAD_DIM, 0.0).astype(BF16)
    cw = jnp.concatenate([conv_w.astype(F32), jnp.zeros((SUBLANES - CONV_K, CONV_W), F32)], axis=0)

    q, k, v, yc = _inproj(x2, row2(g_mix), w_in_b[:, :qkv_w], w_in_b[:, qkv_w:conv_end],
                          row2(b_in[:conv_end]), gq, gk, hmat, cw, row2(conv_b), s)

    ya = _attention(q.reshape(b, s, ATTN_W), k.reshape(b, s, ATTN_W), v.reshape(b, s, ATTN_W),
                    _attn_bias(rel_bias)).reshape(t, ATTN_W)

    n_pad = LANES - N_GROUPS - N_EXPERTS
    wrt = jnp.concatenate([w_router, w_group, jnp.zeros((d, n_pad), w_group.dtype)], axis=1).astype(BF16)
    brt = row2(jnp.concatenate([b_router, b_group, jnp.zeros((n_pad,), b_group.dtype)]))
    h1, stage, route, cnt_f = _merge(x2, ya, yc, row2(g_mix), w_in_b[:, conv_end:], row2(b_in[conv_end:]),
                                     w_pa.astype(BF16), w_pc.astype(BF16), w_o.astype(BF16),
                                     row2(g_ffn), wrt, brt)

    n_tiles = t // TM
    cnt = cnt_f[:, 0, :N_EXPERTS].astype(jnp.int32)
    tile_off = jnp.cumsum(cnt, axis=1) - cnt
    tot = cnt.sum(axis=0)
    pcounts = (tot + ROW_BLK - 1) // ROW_BLK * ROW_BLK
    pends = jnp.cumsum(pcounts)
    pstarts = pends - pcounts
    cum = jnp.cumsum(cnt, axis=0) - cnt
    dst = pstarts[None, :] + cum
    n_blocks = (t * TOP_K) // ROW_BLK + N_EXPERTS
    blk_start = jnp.arange(n_blocks, dtype=jnp.int32) * ROW_BLK
    blk_e = jnp.minimum((pends[None, :] <= blk_start[:, None]).sum(axis=1), N_EXPERTS - 1).astype(jnp.int32)
    nblk = (pends[-1:] // ROW_BLK).astype(jnp.int32)
    sel = (jnp.arange(N_EXPERTS, dtype=jnp.int32)[:, None] == blk_e[None, :]).astype(jnp.int32)
    of_block = lambda a: (a[..., None] * sel).sum(axis=-2)
    base = blk_start - of_block(pstarts)
    nvalid = jnp.clip(of_block(tot) - base, 0, ROW_BLK)
    cum_e = of_block(cum)
    cnt_e = of_block(cnt)
    jlo = (cum_e + cnt_e <= base[None, :]).sum(axis=0)
    jhi = (cum_e < base[None, :] + ROW_BLK).sum(axis=0)
    part = jnp.clip(jnp.minimum(cum_e + cnt_e, base[None, :] + ROW_BLK) - jnp.maximum(cum_e, base[None, :]),
                    0, ROW_BLK)
    blk_pieces = jnp.stack(_piece_counts(part), axis=-1).sum(axis=0)
    tile_pieces = jnp.stack(_piece_counts(cnt), axis=-1).sum(axis=1)
    src = jnp.arange(n_tiles, dtype=jnp.int32)[:, None] * TILE_ROWS + tile_off - cum
    flat = lambda a: a.reshape(-1).astype(jnp.int32)

    blk_big = (part >= BIG_PIECE).any(axis=0)
    tile_big = (cnt >= BIG_PIECE).any(axis=1)

    ys = _experts(blk_e, nblk, flat(base), flat(jlo), flat(jhi), flat(nvalid), flat(blk_big), flat(blk_pieces),
                  flat(cum), flat(cum + cnt), flat(src), stage, w1, w3, w2)
    out = _combine(flat(cnt), flat(tile_off), flat(dst), flat(tile_big), flat(tile_pieces), h1, route,
                   p_i.reshape(t, PLE_DIM), row2(g_ple), w_ple_gate.astype(BF16), row2(b_ple_gate),
                   w_ple_proj.astype(BF16), ys)
    return out.reshape(b, s, d)


def kernel(x, p, g_mix, w_in, b_in, g_q, g_k, rel_bias, conv_w, conv_b, w_pa, w_pc, w_o, g_ffn, w_group, b_group, w_router, b_router, w1, w3, w2, g_ple, w_ple_gate, b_ple_gate, w_ple_proj):
    h = x
    for i in range(p.shape[0]):
        h = _layer(h, p[i], g_mix[i], w_in[i], b_in[i], g_q[i], g_k[i], rel_bias[i], conv_w[i], conv_b[i],
                   w_pa[i], w_pc[i], w_o[i], g_ffn[i], w_group[i], b_group[i], w_router[i], b_router[i],
                   w1[i], w3[i], w2[i], g_ple[i], w_ple_gate[i], b_ple_gate[i], w_ple_proj[i])
    return h
```

```python
import functools

import jax
import jax.numpy as jnp
from jax import lax
from jax.experimental import pallas as pl
from jax.experimental.pallas import tpu as pltpu

D_MODEL = 1024
CHUNK = 64
LEFT_CHUNKS = 8
N_HEADS = 8
HEAD_DIM = 64
ATTN_W = N_HEADS * HEAD_DIM
CONV_W = D_MODEL // 2
CONV_K = 3
MAX_REL_PAST = 256
PLE_DIM = 256
N_GROUPS = 4
EXPERTS_PER_GROUP = 8
N_EXPERTS = N_GROUPS * EXPERTS_PER_GROUP
TOP_K = 2
D_EXPERT = 512
EPS = 1e-6
NEG = -1e30
LOG2E = 1.4426950408889634

LANES = 128
SUBLANES = 8
TM = 256
TI = 1024
MERGE_TILES = 4
COMBINE_TILES = 2
TQ = 256
KV_SLABS = 1 + (LEFT_CHUNKS * CHUNK) // TQ
ROW_BLK = 512
TILE_ROWS = TOP_K * TM
PACK_ROWS = D_MODEL // (2 * LANES)
PACK_W = 2 * LANES
SEG_SIZES = tuple(TM >> k for k in range(TM.bit_length()))
BIG_PIECE = 32
ATTN_BATCH = 2
SCORE_AHEAD = 2
GATHER_AHEAD = 2
GATHER_UNROLL = 36
VMEM_LIMIT = 56 * 1024 * 1024

F32 = jnp.float32
BF16 = jnp.bfloat16
U32 = jnp.uint32


def _dot(a, b):
    return jnp.dot(a, b, preferred_element_type=F32)


def _rms(x, g):
    ms = jnp.mean(x * x, axis=-1, keepdims=True)
    return (x * lax.rsqrt(ms + EPS)) * g


def _sigmoid(x):
    return 1.0 / (1.0 + jnp.exp(-x))


def _pack_block(ref, vals, n_rows, a, is_bf16=False, first_row=0):
    lo = vals[:, 0:LANES]
    hi = vals[:, LANES:PACK_W]
    if not is_bf16:
        lo = lo.astype(BF16).astype(F32)
        hi = hi.astype(BF16).astype(F32)
    ref[pl.ds(first_row * PACK_ROWS + a, n_rows, stride=PACK_ROWS), :] = (
        lax.bitcast_convert_type(hi, U32) | (lax.bitcast_convert_type(lo, U32) >> 16))


def _pack_rows(ref, vals, n_rows, is_bf16=False, first_row=0):
    for a in range(PACK_ROWS):
        _pack_block(ref, vals[:, a * PACK_W:(a + 1) * PACK_W], n_rows, a, is_bf16, first_row)


def _unpack_block(ref, n_rows, a, n_valid=None):
    word = ref[pl.ds(a, n_rows, stride=PACK_ROWS), :]
    if n_valid is not None:
        word = jnp.where(lax.broadcasted_iota(jnp.int32, (n_rows, LANES), 0) < n_valid, word, U32(0))
    lo = lax.bitcast_convert_type(word << 16, F32).astype(BF16)
    hi = lax.bitcast_convert_type(word & U32(0xFFFF0000), F32).astype(BF16)
    return jnp.concatenate([lo, hi], axis=1)


def _split(items, n_groups):
    items = list(items)
    return [items[len(items) * g // n_groups:len(items) * (g + 1) // n_groups] for g in range(n_groups)]


def _inproj_kernel(x_ref, g_ref, wqkv_ref, wconv_ref, b_ref, gq_ref, gk_ref, hm_ref,
                   cw_ref, cb_ref, q_ref, k_ref, v_ref, yc_ref, carry_ref, *, tiles_per_seq):
    i = pl.program_id(0)
    nb = _rms(x_ref[...], g_ref[...]).astype(BF16)

    zq = _dot(nb, wqkv_ref[...]) + b_ref[:, 0:3 * ATTN_W]
    hm = hm_ref[...]

    def head_rms(t, g):
        ms = _dot((t * t).astype(BF16), hm)
        return (t * lax.rsqrt(ms + EPS)) * g

    q_ref[...] = head_rms(zq[:, 0:ATTN_W], gq_ref[...]).astype(BF16)
    k_ref[...] = head_rms(zq[:, ATTN_W:2 * ATTN_W], gk_ref[...]).astype(BF16)
    v_ref[...] = zq[:, 2 * ATTN_W:3 * ATTN_W].astype(BF16)

    zc = _dot(nb, wconv_ref[...]) + b_ref[:, 3 * ATTN_W:3 * ATTN_W + 3 * CONV_W]
    u = zc[:, 0:CONV_W]
    bg = zc[:, CONV_W:2 * CONV_W]
    cg = zc[:, 2 * CONV_W:3 * CONV_W]
    cu = cg * u

    @pl.when((i % tiles_per_seq) == 0)
    def _():
        carry_ref[...] = jnp.zeros_like(carry_ref)

    prev = carry_ref[...]
    carry_ref[...] = cu[TI - SUBLANES:TI, :]
    row = lax.broadcasted_iota(jnp.int32, (SUBLANES, CONV_W), 0)

    def shifted(s):
        r = pltpu.roll(cu, s, 0)
        p = pltpu.roll(prev, s, 0)
        top = jnp.where(row < s, p, r[0:SUBLANES, :])
        return jnp.concatenate([top, r[SUBLANES:, :]], axis=0)

    y = cb_ref[...] + cw_ref[0:1, :] * shifted(2)
    y = y + cw_ref[1:2, :] * shifted(1)
    y = y + cw_ref[2:3, :] * cu
    yc_ref[...] = (bg * y).astype(BF16)


def _inproj(x2, g_mix, wqkv, wconv, b_in, gq, gk, hmat, cw, cb, seq):
    t = x2.shape[0]
    const = lambda i: (0, 0)
    row = lambda i: (i, 0)
    out = jax.ShapeDtypeStruct((t, ATTN_W), BF16)
    return pl.pallas_call(
        functools.partial(_inproj_kernel, tiles_per_seq=seq // TI),
        grid=(t // TI,),
        in_specs=[
            pl.BlockSpec((TI, D_MODEL), row),
            pl.BlockSpec((1, D_MODEL), const),
            pl.BlockSpec((D_MODEL, 3 * ATTN_W), const),
            pl.BlockSpec((D_MODEL, 3 * CONV_W), const),
            pl.BlockSpec((1, 3 * ATTN_W + 3 * CONV_W), const),
            pl.BlockSpec((1, ATTN_W), const),
            pl.BlockSpec((1, ATTN_W), const),
            pl.BlockSpec((ATTN_W, ATTN_W), const),
            pl.BlockSpec((SUBLANES, CONV_W), const),
            pl.BlockSpec((1, CONV_W), const),
        ],
        out_specs=[pl.BlockSpec((TI, ATTN_W), row)] * 4,
        out_shape=[out] * 4,
        scratch_shapes=[pltpu.VMEM((SUBLANES, CONV_W), F32)],
        compiler_params=pltpu.CompilerParams(
            dimension_semantics=("arbitrary",), vmem_limit_bytes=VMEM_LIMIT),
        name="inproj",
    )(x2, g_mix, wqkv, wconv, b_in, gq, gk, hmat, cw, cb)


def _lane_fold(parts, op):
    acc = None
    for a in parts:
        for c in range(0, a.shape[1], LANES):
            piece = a[:, c:c + LANES]
            acc = piece if acc is None else op(acc, piece)
    return acc


def _attn_kernel(q_ref, k0_ref, k1_ref, k2_ref, v0_ref, v1_ref, v2_ref, bias_ref, o_ref):
    k_refs = (k0_ref, k1_ref, k2_ref)
    v_refs = (v0_ref, v1_ref, v2_ref)
    pair_w = 2 * HEAD_DIM
    lane = lax.broadcasted_iota(jnp.int32, (TQ, pair_w), 1)
    low = lane < HEAD_DIM

    def scores(bb, h, pens):
        ps = slice((h // 2) * pair_w, (h // 2 + 1) * pair_w)
        q_pair = q_ref[bb, :, ps]
        own = low if h % 2 == 0 else jnp.logical_not(low)
        qh = jnp.where(own, q_pair, jnp.zeros_like(q_pair))
        s = [lax.dot_general(qh, k_refs[j][bb, :, ps], (((1,), (1,)), ((), ())),
                             preferred_element_type=F32) + bias_ref[h, :, j * TQ:(j + 1) * TQ]
             for j in range(KV_SLABS)]
        return s if pens is None else [sj + pens[j] for j, sj in enumerate(s)]

    def weighted(bb, h, s):
        ps = slice((h // 2) * pair_w, (h // 2 + 1) * pair_w)
        m = _lane_fold(s, jnp.maximum).max(axis=-1, keepdims=True)
        e = [jnp.exp2(sj - m) for sj in s]
        l = _lane_fold(e, jnp.add).sum(axis=-1, keepdims=True)
        acc = None
        for j in range(KV_SLABS):
            oj = _dot(e[j].astype(BF16), v_refs[j][bb, :, ps])
            acc = oj if acc is None else acc + oj
        return acc * (1.0 / l)

    def all_heads(pens):
        items = [(bb, h) for h in range(N_HEADS) for bb in range(ATTN_BATCH)]
        pending = [scores(bb, h, pens) for bb, h in items[:SCORE_AHEAD]]
        o_even = {}
        for n, (bb, h) in enumerate(items):
            if n + SCORE_AHEAD < len(items):
                pending.append(scores(*items[n + SCORE_AHEAD], pens))
            o = weighted(bb, h, pending.pop(0))
            if h % 2 == 0:
                o_even[bb] = o
            else:
                ps = slice((h // 2) * pair_w, (h // 2 + 1) * pair_w)
                o_ref[bb, :, ps] = jnp.where(low, o_even[bb], o).astype(BF16)

    i = pl.program_id(1)

    @pl.when(i >= KV_SLABS - 1)
    def _():
        all_heads(None)

    @pl.when(i < KV_SLABS - 1)
    def _():
        all_heads([jnp.where(i >= KV_SLABS - 1 - j, 0.0, NEG).astype(F32) for j in range(KV_SLABS)])


def _attention(q, k, v, bias):
    b, s, _ = q.shape
    blk = (ATTN_BATCH, TQ, ATTN_W)

    def kv_map(j):
        back = KV_SLABS - 1 - j
        return lambda bi, i: (bi, jnp.maximum(i - back, 0), 0)

    kv_specs = [pl.BlockSpec(blk, kv_map(j)) for j in range(KV_SLABS)]
    return pl.pallas_call(
        _attn_kernel,
        grid=(b // ATTN_BATCH, s // TQ),
        in_specs=[pl.BlockSpec(blk, lambda bi, i: (bi, i, 0))] + kv_specs + kv_specs + [
            pl.BlockSpec((N_HEADS, TQ, KV_SLABS * TQ), lambda bi, i: (0, 0, 0))],
        out_specs=pl.BlockSpec(blk, lambda bi, i: (bi, i, 0)),
        out_shape=jax.ShapeDtypeStruct((b, s, ATTN_W), BF16),
        compiler_params=pltpu.CompilerParams(
            dimension_semantics=("arbitrary", "arbitrary"), vmem_limit_bytes=VMEM_LIMIT),
        name="attn",
    )(q, k, k, k, v, v, v, bias)


def _attn_bias(rel_bias):
    nk = KV_SLABS * TQ
    past = nk - TQ
    d = jnp.arange(TQ - 1 + past, -TQ, -1)
    idx = jnp.clip(d, -(CHUNK - 1), MAX_REL_PAST) + (CHUNK - 1)
    onehot = (idx[:, None] == jnp.arange(rel_bias.shape[1])[None, :]).astype(F32)
    per_dist = jnp.einsum("dn,hn->hd", onehot, rel_bias.astype(F32) * LOG2E,
                          precision=lax.Precision.HIGHEST)
    n_h, span = per_dist.shape
    padded = jnp.pad(per_dist, ((0, 0), (0, 2)))
    skew = jnp.tile(padded, (1, TQ))[:, :TQ * (span + 1)].reshape(n_h, TQ, span + 1)
    table = skew[:, :, TQ - 1:TQ - 1 + nk]
    r = jnp.arange(TQ)[:, None]
    c = jnp.arange(nk)[None, :]
    qc = r // CHUNK
    kc = c // CHUNK
    lead = past // CHUNK - LEFT_CHUNKS
    band = (kc >= qc + lead) & (kc <= qc + lead + LEFT_CHUNKS)
    return jnp.where(band[None], table, NEG)


def _merge_kernel(x_ref, ya_ref, yc_ref, g_ref, wg_ref, bgate_ref, wpa_ref, wpc_ref, wo_ref,
                  gffn_ref, wrt_ref, brt_ref, h_ref, stage_ref, route_ref, cnt_ref, n2_scr, logit_scr):
    @pl.when(pl.program_id(0) == 0)
    def _():
        n2_scr[...] = jnp.zeros_like(n2_scr)
        logit_scr[...] = jnp.zeros_like(logit_scr)

    row8 = lax.broadcasted_iota(jnp.int32, (SUBLANES, TM), 0).astype(F32)
    ninf = -jnp.inf

    def argmax_first(vals):
        mx = vals.max(axis=0, keepdims=True)
        idx = jnp.where(vals == mx, row8, float(SUBLANES)).min(axis=0, keepdims=True)
        return mx, idx

    def route_tile(sub):
        rows = slice(sub * TM, (sub + 1) * TM)
        lt = logit_scr[rows, :].T
        gl = jnp.where(row8 < N_GROUPS, lt[N_EXPERTS:N_EXPERTS + SUBLANES, :], ninf)
        gmax, grp = argmax_first(gl)
        p_grp = 1.0 / jnp.exp(gl - gmax).sum(axis=0, keepdims=True)
        el = lt[0:EXPERTS_PER_GROUP, :]
        for g in range(1, N_GROUPS):
            el = jnp.where(grp == g, lt[g * EXPERTS_PER_GROUP:(g + 1) * EXPERTS_PER_GROUP, :], el)
        l1, i1 = argmax_first(el)
        l2, i2 = argmax_first(jnp.where(row8 == i1, ninf, el))
        e2 = jnp.exp(l2 - l1)
        den = 1.0 + e2
        w1 = p_grp * (1.0 / den)
        w2 = p_grp * (e2 / den)
        x1 = grp * EXPERTS_PER_GROUP + i1
        x2 = grp * EXPERTS_PER_GROUP + i2

        row_e = lax.broadcasted_iota(jnp.int32, (N_EXPERTS, TM), 0).astype(F32)
        oh1 = (row_e == x1).astype(F32)
        oh2 = (row_e == x2).astype(F32)
        oh = (oh1 + oh2).astype(BF16)
        r = lax.broadcasted_iota(jnp.int32, (TM, TM), 0)
        c = lax.broadcasted_iota(jnp.int32, (TM, TM), 1)
        earlier_tok = _dot(oh, (r < c).astype(BF16))
        er = lax.broadcasted_iota(jnp.int32, (N_EXPERTS, N_EXPERTS), 0)
        ec = lax.broadcasted_iota(jnp.int32, (N_EXPERTS, N_EXPERTS), 1)
        lower_exp = _dot((ec < er).astype(BF16), oh).sum(axis=1, keepdims=True)
        where = earlier_tok + lower_exp
        pos1 = (oh1 * where).sum(axis=0, keepdims=True)
        pos2 = (oh2 * where).sum(axis=0, keepdims=True)
        counts = lax.dot_general(jnp.ones((SUBLANES, TM), BF16), oh, (((1,), (1,)), ((), ())),
                                 preferred_element_type=F32)
        cnt_ref[sub] = jnp.concatenate([counts, jnp.zeros((SUBLANES, LANES - N_EXPERTS), F32)], axis=1)

        route_t = jnp.zeros((SUBLANES, TM), F32)
        for j, val in enumerate((x1, x2, w1, w2, pos1, pos2)):
            route_t = jnp.where(row8 == j, val, route_t)
        route_ref[rows, :] = jnp.concatenate([route_t, jnp.zeros((LANES - SUBLANES, TM), F32)], axis=0).T
        return pos1, pos2

    n2_old = [n2_scr[sub * TM:(sub + 1) * TM, :] for sub in range(MERGE_TILES)]
    positions = [route_tile(sub) for sub in range(MERGE_TILES)]

    x = x_ref[...]
    nb = _rms(x, g_ref[...]).astype(BF16)
    sga = _sigmoid(_dot(nb, wg_ref[:, 0:D_MODEL]) + bgate_ref[:, 0:D_MODEL])
    ma = sga * _dot(ya_ref[...], wpa_ref[...])
    sgc = _sigmoid(_dot(nb, wg_ref[:, D_MODEL:2 * D_MODEL]) + bgate_ref[:, D_MODEL:2 * D_MODEL])
    m = ma + sgc * _dot(yc_ref[...], wpc_ref[...])
    h = x + _dot(m.astype(BF16), wo_ref[...])
    h_ref[...] = h
    n2_new = _rms(h, gffn_ref[...]).astype(BF16)
    n2_scr[...] = n2_new
    logit_scr[...] = _dot(n2_new, wrt_ref[...]) + brt_ref[...]

    slot = lax.broadcasted_iota(jnp.int32, (TILE_ROWS, TM), 0).astype(F32)
    for sub, (pos1, pos2) in enumerate(positions):
        place = ((slot == pos1) | (slot == pos2)).astype(BF16)
        _pack_rows(stage_ref, _dot(place, n2_old[sub]), TILE_ROWS, is_bf16=True, first_row=sub * TILE_ROWS)


def _merge(x2, ya, yc, g_mix, wgate, bgate, wpa, wpc, wo, gffn, wrt, brt):
    t = x2.shape[0]
    n_tiles = t // TM
    n_steps = n_tiles // MERGE_TILES
    tmm = MERGE_TILES * TM
    const = lambda i: (0, 0)
    row = lambda i: (jnp.minimum(i, n_steps - 1), 0)
    late = lambda i: (jnp.maximum(i - 1, 0), 0)
    return pl.pallas_call(
        _merge_kernel,
        grid=(n_steps + 1,),
        in_specs=[
            pl.BlockSpec((tmm, D_MODEL), row),
            pl.BlockSpec((tmm, ATTN_W), row),
            pl.BlockSpec((tmm, CONV_W), row),
            pl.BlockSpec((1, D_MODEL), const),
            pl.BlockSpec((D_MODEL, 2 * D_MODEL), const),
            pl.BlockSpec((1, 2 * D_MODEL), const),
            pl.BlockSpec((ATTN_W, D_MODEL), const),
            pl.BlockSpec((CONV_W, D_MODEL), const),
            pl.BlockSpec((D_MODEL, D_MODEL), const),
            pl.BlockSpec((1, D_MODEL), const),
            pl.BlockSpec((D_MODEL, LANES), const),
            pl.BlockSpec((1, LANES), const),
        ],
        out_specs=[
            pl.BlockSpec((tmm, D_MODEL), row),
            pl.BlockSpec((MERGE_TILES * TILE_ROWS * PACK_ROWS, LANES), late),
            pl.BlockSpec((tmm, LANES), late),
            pl.BlockSpec((MERGE_TILES, SUBLANES, LANES), lambda i: (jnp.maximum(i - 1, 0), 0, 0)),
        ],
        out_shape=[
            jax.ShapeDtypeStruct((t, D_MODEL), F32),
            jax.ShapeDtypeStruct((n_tiles * TILE_ROWS * PACK_ROWS, LANES), U32),
            jax.ShapeDtypeStruct((t, LANES), F32),
            jax.ShapeDtypeStruct((n_tiles, SUBLANES, LANES), F32),
        ],
        scratch_shapes=[pltpu.VMEM((tmm, D_MODEL), BF16), pltpu.VMEM((tmm, LANES), F32)],
        compiler_params=pltpu.CompilerParams(
            dimension_semantics=("arbitrary",), vmem_limit_bytes=VMEM_LIMIT),
        name="merge",
    )(x2, ya, yc, g_mix, wgate, bgate, wpa, wpc, wo, gffn, wrt, brt)


def _piece_counts(n):
    return [(n >> (size.bit_length() - 1)) & 1 for size in SEG_SIZES]


ALL_CLASSES = tuple(range(len(SEG_SIZES)))
BIG_CLASSES = tuple(c for c in ALL_CLASSES if SEG_SIZES[c] >= BIG_PIECE)
SMALL_CLASSES = tuple(c for c in ALL_CLASSES if SEG_SIZES[c] < BIG_PIECE)


def _segment_pieces(n, visit, classes=ALL_CLASSES):
    for cls in classes:
        size = SEG_SIZES[cls]

        @pl.when((n & size) != 0)
        def _(cls=cls, size=size):
            visit(cls, n & ~(2 * size - 1))


def _piece_copy(src_ref, dst_ref, sems, cls, src_row, dst_row):
    n = SEG_SIZES[cls] * PACK_ROWS
    return pltpu.make_async_copy(src_ref.at[pl.ds(src_row * PACK_ROWS, n), :],
                                 dst_ref.at[pl.ds(dst_row * PACK_ROWS, n), :], sems.at[cls])


def _drain(src_ref, dst_ref, sems, counts, has_big):
    unroll = 4

    def wait_classes(classes):
        for cls in classes:
            def wait_some(k, cls=cls):
                def body(t, carry):
                    for _ in range(k):
                        _piece_copy(src_ref, dst_ref, sems, cls, 0, 0).wait()
                    return carry
                return body

            n = counts[cls]
            lax.fori_loop(0, n >> (unroll.bit_length() - 1), wait_some(unroll), 0)
            lax.fori_loop(0, n & (unroll - 1), wait_some(1), 0)

    @pl.when(has_big)
    def _():
        wait_classes(BIG_CLASSES)

    wait_classes(SMALL_CLASSES)


def _experts_kernel(be_ref, nb_ref, base_ref, jlo_ref, jhi_ref, nv_ref, big_ref, npiece_ref, cum_ref, end_ref,
                    src_ref, stage_hbm, w1_ref, w3_ref, w2_ref, ys_ref, *scratch, n_tiles, n_blocks):
    xbufs = scratch[:GATHER_AHEAD + 1]
    w1b_ref, w3b_ref, w2b_ref, sems = scratch[GATHER_AHEAD + 1:]
    s = pl.program_id(0)
    nb = nb_ref[0]
    n_cls = len(SEG_SIZES)

    def segment_copies(step, j, live, buf, sem, classes=ALL_CLASSES):
        base = base_ref[step]
        g = jnp.minimum(j, n_tiles - 1) * N_EXPERTS + be_ref[step]
        lo = jnp.maximum(cum_ref[g], base)
        hi = jnp.minimum(end_ref[g], base + ROW_BLK)
        src = src_ref[g] + lo
        dst = lo - base
        _segment_pieces(jnp.where(live, jnp.maximum(hi - lo, 0), 0),
                        lambda cls, o: _piece_copy(stage_hbm, buf, sem, cls, src + o, dst + o).start(), classes)

    def looped_copies(step, j0, j1, buf, sem, classes=ALL_CLASSES):
        def body(j, carry):
            segment_copies(step, j, True, buf, sem, classes)
            return carry

        lax.fori_loop(j0, j1, body, 0)

    @pl.when(s == 0)
    def _():
        for buf in xbufs:
            buf[...] = jnp.zeros_like(buf)
        for first in range(GATHER_AHEAD):
            blk = min(first, n_blocks - 1)
            looped_copies(blk, jlo_ref[blk], jnp.where(first < nb, jhi_ref[blk], jlo_ref[blk]),
                          xbufs[first], sems.at[first])

    def step(cur, cur_sem, nxt, nxt_sem):
        prev = be_ref[jnp.maximum(s - 1, 0)]

        @pl.when((s == 0) | (be_ref[s] != prev))
        def _():
            w1b_ref[...] = w1_ref[0].astype(BF16)
            w3b_ref[...] = w3_ref[0].astype(BF16)
            w2b_ref[...] = w2_ref[0].astype(BF16)

        _drain(stage_hbm, cur, cur_sem, [npiece_ref[s * n_cls + c] for c in range(n_cls)], big_ref[s] != 0)

        nxt_step = jnp.minimum(s + GATHER_AHEAD, n_blocks - 1)
        live = s + GATHER_AHEAD < nb
        j0 = jlo_ref[nxt_step]
        j1 = jnp.where(live, jhi_ref[nxt_step], j0)
        looped_copies(nxt_step, j0 + GATHER_UNROLL, j1, nxt, nxt_sem, SMALL_CLASSES)
        looped_copies(nxt_step, j0, jnp.where(big_ref[nxt_step] != 0, j1, j0), nxt, nxt_sem, BIG_CLASSES)
        groups = iter(_split(range(GATHER_UNROLL), 2 * PACK_ROWS))

        def start_group():
            for k in next(groups):
                segment_copies(nxt_step, j0 + k, j0 + k < j1, nxt, nxt_sem, SMALL_CLASSES)

        a = None
        g = None
        for blk in range(PACK_ROWS):
            start_group()
            xa = _unpack_block(cur, ROW_BLK, blk, n_valid=nv_ref[s])
            rows = slice(blk * PACK_W, (blk + 1) * PACK_W)
            da = _dot(xa, w1b_ref[rows, :])
            dg = _dot(xa, w3b_ref[rows, :])
            a = da if a is None else a + da
            g = dg if g is None else g + dg
        hdn = ((a * _sigmoid(a)) * g).astype(BF16)
        for blk in range(PACK_ROWS):
            start_group()
            _pack_block(ys_ref, _dot(hdn, w2b_ref[:, blk * PACK_W:(blk + 1) * PACK_W]), ROW_BLK, blk)

    n_buf = len(xbufs)
    for slot in range(n_buf):
        @pl.when((s < nb) & (s % n_buf == slot))
        def _(slot=slot):
            ahead = (slot + GATHER_AHEAD) % n_buf
            step(xbufs[slot], sems.at[slot], xbufs[ahead], sems.at[ahead])

    @pl.when(s >= nb)
    def _():
        ys_ref[...] = jnp.zeros_like(ys_ref)


def _experts(blk_e, nblk, base, jlo, jhi, nvalid, big, npiece, cum, end, src, stage, w1, w3, w2):
    n_blocks = blk_e.shape[0]
    n_tiles = cum.shape[0] // N_EXPERTS

    def wsel(s, be, nb, *_):
        return (be[jnp.minimum(s, nb[0] - 1)], 0, 0)

    grid_spec = pltpu.PrefetchScalarGridSpec(
        num_scalar_prefetch=11,
        grid=(n_blocks,),
        in_specs=[
            pl.BlockSpec(memory_space=pl.ANY),
            pl.BlockSpec((1, D_MODEL, D_EXPERT), wsel),
            pl.BlockSpec((1, D_MODEL, D_EXPERT), wsel),
            pl.BlockSpec((1, D_EXPERT, D_MODEL), wsel),
        ],
        out_specs=pl.BlockSpec((ROW_BLK * PACK_ROWS, LANES), lambda s, *_: (s, 0)),
        scratch_shapes=[pltpu.VMEM((ROW_BLK * PACK_ROWS, LANES), U32)] * (GATHER_AHEAD + 1) + [
            pltpu.VMEM((D_MODEL, D_EXPERT), BF16),
            pltpu.VMEM((D_MODEL, D_EXPERT), BF16),
            pltpu.VMEM((D_EXPERT, D_MODEL), BF16),
            pltpu.SemaphoreType.DMA((GATHER_AHEAD + 1, len(SEG_SIZES))),
        ],
    )
    return pl.pallas_call(
        functools.partial(_experts_kernel, n_tiles=n_tiles, n_blocks=n_blocks),
        grid_spec=grid_spec,
        out_shape=jax.ShapeDtypeStruct((n_blocks * ROW_BLK * PACK_ROWS, LANES), U32),
        compiler_params=pltpu.CompilerParams(
            dimension_semantics=("arbitrary",), vmem_limit_bytes=VMEM_LIMIT),
        name="experts",
    )(blk_e, nblk, base, jlo, jhi, nvalid, big, npiece, cum, end, src, stage, w1, w3, w2)


def _combine_kernel(cnt_ref, off_ref, dst_ref, big_ref, npiece_ref, h_ref, route_ref, p_ref, gple_ref, wpg_ref,
                    bpg_ref, wpp_ref, ys_hbm, o_ref, *scratch, n_tiles):
    ybufs, sems = scratch[:-1], scratch[-1]
    i = pl.program_id(0)
    n_cls = len(SEG_SIZES)

    def segment_copies(tile, e, live, slot, classes=ALL_CLASSES):
        g = tile * N_EXPERTS + e
        off, dst = off_ref[g], dst_ref[g]
        _segment_pieces(jnp.where(live, cnt_ref[g], 0),
                        lambda cls, o: _piece_copy(ys_hbm, ybufs[slot], sems.at[slot], cls, dst + o, off + o).start(),
                        classes)

    def looped_copies(tile, n_experts, slot, classes=ALL_CLASSES):
        def body(e, carry):
            segment_copies(tile, e, True, slot, classes)
            return carry

        lax.fori_loop(0, n_experts, body, 0)

    @pl.when(i == 0)
    def _():
        for first in range(COMBINE_TILES):
            looped_copies(first, N_EXPERTS, first)

    def step(cur_slots, nxt_slots):
        tiles = [i * COMBINE_TILES + sub for sub in range(COMBINE_TILES)]
        for tile, slot in zip(tiles, cur_slots):
            _drain(ys_hbm, ybufs[slot], sems.at[slot], [npiece_ref[tile * n_cls + c] for c in range(n_cls)],
                   big_ref[tile] != 0)
        ahead = [(jnp.minimum(tile + COMBINE_TILES, n_tiles - 1), tile + COMBINE_TILES < n_tiles, slot)
                 for tile, slot in zip(tiles, nxt_slots)]
        for tile, live, slot in ahead:
            looped_copies(tile, jnp.where(live & (big_ref[tile] != 0), N_EXPERTS, 0), slot, BIG_CLASSES)
        groups = iter(_split([(a, e) for a in ahead for e in range(N_EXPERTS)], COMBINE_TILES * PACK_ROWS + 2))

        def start_group():
            for (tile, live, slot), e in next(groups):
                segment_copies(tile, e, live, slot, SMALL_CLASSES)

        start_group()
        pp = _dot(p_ref[...].astype(BF16), wpp_ref[...])
        place = lax.broadcasted_iota(jnp.int32, (TM, TILE_ROWS), 1).astype(F32)
        moe_rows = []
        for sub, slot in enumerate(cur_slots):
            route = route_ref[sub * TM:(sub + 1) * TM, :]
            sel = [(place == route[:, 4 + kk:5 + kk]).astype(BF16) for kk in range(TOP_K)]
            moe = []
            for blk in range(PACK_ROWS):
                start_group()
                cols = _unpack_block(ybufs[slot], TILE_ROWS, blk)
                moe.append(_dot(sel[0], cols) * route[:, 2:3] + _dot(sel[1], cols) * route[:, 3:4])
            moe_rows.append(jnp.concatenate(moe, axis=1))
        start_group()
        h = h_ref[...] + jnp.concatenate(moe_rows, axis=0)
        gate = _sigmoid(_dot(_rms(h, gple_ref[...]).astype(BF16), wpg_ref[...]) + bpg_ref[...])
        o_ref[...] = h + gate * pp

    for parity in range(2):
        @pl.when(i % 2 == parity)
        def _(parity=parity):
            half = [list(range(p * COMBINE_TILES, (p + 1) * COMBINE_TILES)) for p in range(2)]
            step(half[parity], half[1 - parity])


def _combine(cnt, off, dst, big, npiece, h1, route, p2, gple, wpg, bpg, wpp, ys):
    t = h1.shape[0]
    n_tiles = t // TM
    rows = COMBINE_TILES * TM
    const = lambda i, *_: (0, 0)
    row = lambda i, *_: (i, 0)
    grid_spec = pltpu.PrefetchScalarGridSpec(
        num_scalar_prefetch=5,
        grid=(n_tiles // COMBINE_TILES,),
        in_specs=[
            pl.BlockSpec((rows, D_MODEL), row),
            pl.BlockSpec((rows, LANES), row),
            pl.BlockSpec((rows, PLE_DIM), row),
            pl.BlockSpec((1, D_MODEL), const),
            pl.BlockSpec((D_MODEL, D_MODEL), const),
            pl.BlockSpec((1, D_MODEL), const),
            pl.BlockSpec((PLE_DIM, D_MODEL), const),
            pl.BlockSpec(memory_space=pl.ANY),
        ],
        out_specs=pl.BlockSpec((rows, D_MODEL), row),
        scratch_shapes=[pltpu.VMEM((TILE_ROWS * PACK_ROWS, LANES), U32)] * (2 * COMBINE_TILES) + [
            pltpu.SemaphoreType.DMA((2 * COMBINE_TILES, len(SEG_SIZES))),
        ],
    )
    return pl.pallas_call(
        functools.partial(_combine_kernel, n_tiles=n_tiles),
        grid_spec=grid_spec,
        out_shape=jax.ShapeDtypeStruct((t, D_MODEL), F32),
        compiler_params=pltpu.CompilerParams(
            dimension_semantics=("arbitrary",), vmem_limit_bytes=VMEM_LIMIT),
        name="combine",
    )(cnt, off, dst, big, npiece, h1, route, p2, gple, wpg, bpg, wpp, ys)


def _layer(h, p_i, g_mix, w_in, b_in, g_q, g_k, rel_bias, conv_w, conv_b, w_pa, w_pc, w_o,
           g_ffn, w_group, b_group, w_router, b_router, w1, w3, w2,
           g_ple, w_ple_gate, b_ple_gate, w_ple_proj):
    b, s, d = h.shape
    t = b * s
    x2 = h.reshape(t, d)
    row2 = lambda a: a.reshape(1, -1).astype(F32)

    qkv_w = 3 * ATTN_W
    conv_end = qkv_w + 3 * CONV_W
    w_in_b = w_in.astype(BF16)
    gq = row2(jnp.tile(g_q.astype(F32) * (HEAD_DIM ** -0.5 * LOG2E), N_HEADS))
    gk = row2(jnp.tile(g_k.astype(F32), N_HEADS))
    head = jnp.arange(ATTN_W) // HEAD_DIM
    hmat = jnp.where(head[:, None] == head[None, :], 1.0 / HEAD_DIM, 0.0).astype(BF16)
    cw = jnp.concatenate([conv_w.astype(F32), jnp.zeros((SUBLANES - CONV_K, CONV_W), F32)], axis=0)

    q, k, v, yc = _inproj(x2, row2(g_mix), w_in_b[:, :qkv_w], w_in_b[:, qkv_w:conv_end],
                          row2(b_in[:conv_end]), gq, gk, hmat, cw, row2(conv_b), s)

    ya = _attention(q.reshape(b, s, ATTN_W), k.reshape(b, s, ATTN_W), v.reshape(b, s, ATTN_W),
                    _attn_bias(rel_bias)).reshape(t, ATTN_W)

    n_pad = LANES - N_GROUPS - N_EXPERTS
    wrt = jnp.concatenate([w_router, w_group, jnp.zeros((d, n_pad), w_group.dtype)], axis=1).astype(BF16)
    brt = row2(jnp.concatenate([b_router, b_group, jnp.zeros((n_pad,), b_group.dtype)]))
    h1, stage, route, cnt_f = _merge(x2, ya, yc, row2(g_mix), w_in_b[:, conv_end:], row2(b_in[conv_end:]),
                                     w_pa.astype(BF16), w_pc.astype(BF16), w_o.astype(BF16),
                                     row2(g_ffn), wrt, brt)

    n_tiles = t // TM
    cnt = cnt_f[:, 0, :N_EXPERTS].astype(jnp.int32)
    tile_off = jnp.cumsum(cnt, axis=1) - cnt
    tot = cnt.sum(axis=0)
    pcounts = (tot + ROW_BLK - 1) // ROW_BLK * ROW_BLK
    pends = jnp.cumsum(pcounts)
    pstarts = pends - pcounts
    cum = jnp.cumsum(cnt, axis=0) - cnt
    dst = pstarts[None, :] + cum
    n_blocks = (t * TOP_K) // ROW_BLK + N_EXPERTS
    blk_start = jnp.arange(n_blocks, dtype=jnp.int32) * ROW_BLK
    blk_e = jnp.minimum((pends[None, :] <= blk_start[:, None]).sum(axis=1), N_EXPERTS - 1).astype(jnp.int32)
    nblk = (pends[-1:] // ROW_BLK).astype(jnp.int32)
    sel = (jnp.arange(N_EXPERTS, dtype=jnp.int32)[:, None] == blk_e[None, :]).astype(jnp.int32)
    of_block = lambda a: (a[..., None] * sel).sum(axis=-2)
    base = blk_start - of_block(pstarts)
    nvalid = jnp.clip(of_block(tot) - base, 0, ROW_BLK)
    cum_e = of_block(cum)
    cnt_e = of_block(cnt)
    jlo = (cum_e + cnt_e <= base[None, :]).sum(axis=0)
    jhi = (cum_e < base[None, :] + ROW_BLK).sum(axis=0)
    part = jnp.clip(jnp.minimum(cum_e + cnt_e, base[None, :] + ROW_BLK) - jnp.maximum(cum_e, base[None, :]),
                    0, ROW_BLK)
    blk_pieces = jnp.stack(_piece_counts(part), axis=-1).sum(axis=0)
    tile_pieces = jnp.stack(_piece_counts(cnt), axis=-1).sum(axis=1)
    src = jnp.arange(n_tiles, dtype=jnp.int32)[:, None] * TILE_ROWS + tile_off - cum
    flat = lambda a: a.reshape(-1).astype(jnp.int32)

    blk_big = (part >= BIG_PIECE).any(axis=0)
    tile_big = (cnt >= BIG_PIECE).any(axis=1)

    ys = _experts(blk_e, nblk, flat(base), flat(jlo), flat(jhi), flat(nvalid), flat(blk_big), flat(blk_pieces),
                  flat(cum), flat(cum + cnt), flat(src), stage, w1, w3, w2)
    out = _combine(flat(cnt), flat(tile_off), flat(dst), flat(tile_big), flat(tile_pieces), h1, route,
                   p_i.reshape(t, PLE_DIM), row2(g_ple), w_ple_gate.astype(BF16), row2(b_ple_gate),
                   w_ple_proj.astype(BF16), ys)
    return out.reshape(b, s, d)


def kernel(x, p, g_mix, w_in, b_in, g_q, g_k, rel_bias, conv_w, conv_b, w_pa, w_pc, w_o, g_ffn, w_group, b_group, w_router, b_router, w1, w3, w2, g_ple, w_ple_gate, b_ple_gate, w_ple_proj):
    h = x
    for i in range(p.shape[0]):
        h = _layer(h, p[i], g_mix[i], w_in[i], b_in[i], g_q[i], g_k[i], rel_bias[i], conv_w[i], conv_b[i],
                   w_pa[i], w_pc[i], w_o[i], g_ffn[i], w_group[i], b_group[i], w_router[i], b_router[i],
                   w1[i], w3[i], w2[i], g_ple[i], w_ple_gate[i], b_ple_gate[i], w_ple_proj[i])
    return h
```

```python
import functools

import jax
import jax.numpy as jnp
from jax import lax
from jax.experimental import pallas as pl
from jax.experimental.pallas import tpu as pltpu

D_MODEL = 1024
CHUNK = 64
LEFT_CHUNKS = 8
N_HEADS = 8
HEAD_DIM = 64
ATTN_W = N_HEADS * HEAD_DIM
CONV_W = D_MODEL // 2
CONV_K = 3
MAX_REL_PAST = 256
PLE_DIM = 256
N_GROUPS = 4
EXPERTS_PER_GROUP = 8
N_EXPERTS = N_GROUPS * EXPERTS_PER_GROUP
TOP_K = 2
D_EXPERT = 512
EPS = 1e-6
NEG = -1e30
LOG2E = 1.4426950408889634

LANES = 128
SUBLANES = 8
TM = 256
TI = 1024
MERGE_TILES = 4
COMBINE_TILES = 2
TQ = 256
KV_SLABS = 1 + (LEFT_CHUNKS * CHUNK) // TQ
ROW_BLK = 512
TILE_ROWS = TOP_K * TM
PACK_ROWS = D_MODEL // (2 * LANES)
PACK_W = 2 * LANES
SEG_SIZES = tuple(TM >> k for k in range(TM.bit_length()))
BIG_PIECE = 32
ATTN_BATCH = 4
SCORE_AHEAD = 2
GATHER_AHEAD = 2
GATHER_UNROLL = 36
VMEM_LIMIT = 56 * 1024 * 1024

F32 = jnp.float32
BF16 = jnp.bfloat16
U32 = jnp.uint32


def _dot(a, b):
    return jnp.dot(a, b, preferred_element_type=F32)


def _rms(x, g):
    ms = jnp.mean(x * x, axis=-1, keepdims=True)
    return (x * lax.rsqrt(ms + EPS)) * g


def _sigmoid(x):
    return 1.0 / (1.0 + jnp.exp(-x))


def _pack_block(ref, vals, n_rows, a, is_bf16=False, first_row=0):
    lo = vals[:, 0:LANES]
    hi = vals[:, LANES:PACK_W]
    if not is_bf16:
        lo = lo.astype(BF16).astype(F32)
        hi = hi.astype(BF16).astype(F32)
    ref[pl.ds(first_row * PACK_ROWS + a, n_rows, stride=PACK_ROWS), :] = (
        lax.bitcast_convert_type(hi, U32) | (lax.bitcast_convert_type(lo, U32) >> 16))


def _pack_rows(ref, vals, n_rows, is_bf16=False, first_row=0):
    for a in range(PACK_ROWS):
        _pack_block(ref, vals[:, a * PACK_W:(a + 1) * PACK_W], n_rows, a, is_bf16, first_row)


def _unpack_block(ref, n_rows, a, n_valid=None):
    word = ref[pl.ds(a, n_rows, stride=PACK_ROWS), :]
    if n_valid is not None:
        word = jnp.where(lax.broadcasted_iota(jnp.int32, (n_rows, LANES), 0) < n_valid, word, U32(0))
    lo = lax.bitcast_convert_type(word << 16, F32).astype(BF16)
    hi = lax.bitcast_convert_type(word & U32(0xFFFF0000), F32).astype(BF16)
    return jnp.concatenate([lo, hi], axis=1)


def _split(items, n_groups):
    items = list(items)
    return [items[len(items) * g // n_groups:len(items) * (g + 1) // n_groups] for g in range(n_groups)]


def _inproj_kernel(x_ref, g_ref, wqkv_ref, wconv_ref, b_ref, gq_ref, gk_ref, hm_ref,
                   cw_ref, cb_ref, q_ref, k_ref, v_ref, yc_ref, carry_ref, *, tiles_per_seq):
    i = pl.program_id(0)
    nb = _rms(x_ref[...], g_ref[...]).astype(BF16)

    zq = _dot(nb, wqkv_ref[...]) + b_ref[:, 0:3 * ATTN_W]
    hm = hm_ref[...]

    def head_rms(t, g):
        ms = _dot((t * t).astype(BF16), hm)
        return (t * lax.rsqrt(ms + EPS)) * g

    q_ref[...] = head_rms(zq[:, 0:ATTN_W], gq_ref[...]).astype(BF16)
    k_ref[...] = head_rms(zq[:, ATTN_W:2 * ATTN_W], gk_ref[...]).astype(BF16)
    v_ref[...] = zq[:, 2 * ATTN_W:3 * ATTN_W].astype(BF16)

    zc = _dot(nb, wconv_ref[...]) + b_ref[:, 3 * ATTN_W:3 * ATTN_W + 3 * CONV_W]
    u = zc[:, 0:CONV_W]
    bg = zc[:, CONV_W:2 * CONV_W]
    cg = zc[:, 2 * CONV_W:3 * CONV_W]
    cu = cg * u

    @pl.when((i % tiles_per_seq) == 0)
    def _():
        carry_ref[...] = jnp.zeros_like(carry_ref)

    prev = carry_ref[...]
    carry_ref[...] = cu[TI - SUBLANES:TI, :]
    row = lax.broadcasted_iota(jnp.int32, (SUBLANES, CONV_W), 0)

    def shifted(s):
        r = pltpu.roll(cu, s, 0)
        p = pltpu.roll(prev, s, 0)
        top = jnp.where(row < s, p, r[0:SUBLANES, :])
        return jnp.concatenate([top, r[SUBLANES:, :]], axis=0)

    y = cb_ref[...] + cw_ref[0:1, :] * shifted(2)
    y = y + cw_ref[1:2, :] * shifted(1)
    y = y + cw_ref[2:3, :] * cu
    yc_ref[...] = (bg * y).astype(BF16)


def _inproj(x2, g_mix, wqkv, wconv, b_in, gq, gk, hmat, cw, cb, seq):
    t = x2.shape[0]
    const = lambda i: (0, 0)
    row = lambda i: (i, 0)
    out = jax.ShapeDtypeStruct((t, ATTN_W), BF16)
    return pl.pallas_call(
        functools.partial(_inproj_kernel, tiles_per_seq=seq // TI),
        grid=(t // TI,),
        in_specs=[
            pl.BlockSpec((TI, D_MODEL), row),
            pl.BlockSpec((1, D_MODEL), const),
            pl.BlockSpec((D_MODEL, 3 * ATTN_W), const),
            pl.BlockSpec((D_MODEL, 3 * CONV_W), const),
            pl.BlockSpec((1, 3 * ATTN_W + 3 * CONV_W), const),
            pl.BlockSpec((1, ATTN_W), const),
            pl.BlockSpec((1, ATTN_W), const),
            pl.BlockSpec((ATTN_W, ATTN_W), const),
            pl.BlockSpec((SUBLANES, CONV_W), const),
            pl.BlockSpec((1, CONV_W), const),
        ],
        out_specs=[pl.BlockSpec((TI, ATTN_W), row)] * 4,
        out_shape=[out] * 4,
        scratch_shapes=[pltpu.VMEM((SUBLANES, CONV_W), F32)],
        compiler_params=pltpu.CompilerParams(
            dimension_semantics=("arbitrary",), vmem_limit_bytes=VMEM_LIMIT),
        name="inproj",
    )(x2, g_mix, wqkv, wconv, b_in, gq, gk, hmat, cw, cb)


def _lane_fold(parts, op):
    acc = None
    for a in parts:
        for c in range(0, a.shape[1], LANES):
            piece = a[:, c:c + LANES]
            acc = piece if acc is None else op(acc, piece)
    return acc


def _attn_kernel(q_ref, k0_ref, k1_ref, k2_ref, v0_ref, v1_ref, v2_ref, bias_ref, o_ref):
    k_refs = (k0_ref, k1_ref, k2_ref)
    v_refs = (v0_ref, v1_ref, v2_ref)
    pair_w = 2 * HEAD_DIM
    lane = lax.broadcasted_iota(jnp.int32, (TQ, pair_w), 1)
    low = lane < HEAD_DIM

    def scores(bb, h, pens):
        ps = slice((h // 2) * pair_w, (h // 2 + 1) * pair_w)
        q_pair = q_ref[bb, :, ps]
        own = low if h % 2 == 0 else jnp.logical_not(low)
        qh = jnp.where(own, q_pair, jnp.zeros_like(q_pair))
        s = [lax.dot_general(qh, k_refs[j][bb, :, ps], (((1,), (1,)), ((), ())),
                             preferred_element_type=F32) + bias_ref[h, :, j * TQ:(j + 1) * TQ]
             for j in range(KV_SLABS)]
        return s if pens is None else [sj + pens[j] for j, sj in enumerate(s)]

    def weighted(bb, h, s):
        ps = slice((h // 2) * pair_w, (h // 2 + 1) * pair_w)
        m = _lane_fold(s, jnp.maximum).max(axis=-1, keepdims=True)
        e = [jnp.exp2(sj - m) for sj in s]
        l = _lane_fold(e, jnp.add).sum(axis=-1, keepdims=True)
        acc = None
        for j in range(KV_SLABS):
            oj = _dot(e[j].astype(BF16), v_refs[j][bb, :, ps])
            acc = oj if acc is None else acc + oj
        return acc * (1.0 / l)

    def all_heads(pens):
        items = [(bb, h) for h in range(N_HEADS) for bb in range(ATTN_BATCH)]
        pending = [scores(bb, h, pens) for bb, h in items[:SCORE_AHEAD]]
        o_even = {}
        for n, (bb, h) in enumerate(items):
            if n + SCORE_AHEAD < len(items):
                pending.append(scores(*items[n + SCORE_AHEAD], pens))
            o = weighted(bb, h, pending.pop(0))
            if h % 2 == 0:
                o_even[bb] = o
            else:
                ps = slice((h // 2) * pair_w, (h // 2 + 1) * pair_w)
                o_ref[bb, :, ps] = jnp.where(low, o_even[bb], o).astype(BF16)

    i = pl.program_id(1)

    @pl.when(i >= KV_SLABS - 1)
    def _():
        all_heads(None)

    @pl.when(i < KV_SLABS - 1)
    def _():
        all_heads([jnp.where(i >= KV_SLABS - 1 - j, 0.0, NEG).astype(F32) for j in range(KV_SLABS)])


def _attention(q, k, v, bias):
    b, s, _ = q.shape
    blk = (ATTN_BATCH, TQ, ATTN_W)

    def kv_map(j):
        back = KV_SLABS - 1 - j
        return lambda bi, i: (bi, jnp.maximum(i - back, 0), 0)

    kv_specs = [pl.BlockSpec(blk, kv_map(j)) for j in range(KV_SLABS)]
    return pl.pallas_call(
        _attn_kernel,
        grid=(b // ATTN_BATCH, s // TQ),
        in_specs=[pl.BlockSpec(blk, lambda bi, i: (bi, i, 0))] + kv_specs + kv_specs + [
            pl.BlockSpec((N_HEADS, TQ, KV_SLABS * TQ), lambda bi, i: (0, 0, 0))],
        out_specs=pl.BlockSpec(blk, lambda bi, i: (bi, i, 0)),
        out_shape=jax.ShapeDtypeStruct((b, s, ATTN_W), BF16),
        compiler_params=pltpu.CompilerParams(
            dimension_semantics=("arbitrary", "arbitrary"), vmem_limit_bytes=VMEM_LIMIT),
        name="attn",
    )(q, k, k, k, v, v, v, bias)


def _attn_bias(rel_bias):
    nk = KV_SLABS * TQ
    past = nk - TQ
    d = jnp.arange(TQ - 1 + past, -TQ, -1)
    idx = jnp.clip(d, -(CHUNK - 1), MAX_REL_PAST) + (CHUNK - 1)
    onehot = (idx[:, None] == jnp.arange(rel_bias.shape[1])[None, :]).astype(F32)
    per_dist = jnp.einsum("dn,hn->hd", onehot, rel_bias.astype(F32) * LOG2E,
                          precision=lax.Precision.HIGHEST)
    n_h, span = per_dist.shape
    padded = jnp.pad(per_dist, ((0, 0), (0, 2)))
    skew = jnp.tile(padded, (1, TQ))[:, :TQ * (span + 1)].reshape(n_h, TQ, span + 1)
    table = skew[:, :, TQ - 1:TQ - 1 + nk]
    r = jnp.arange(TQ)[:, None]
    c = jnp.arange(nk)[None, :]
    qc = r // CHUNK
    kc = c // CHUNK
    lead = past // CHUNK - LEFT_CHUNKS
    band = (kc >= qc + lead) & (kc <= qc + lead + LEFT_CHUNKS)
    return jnp.where(band[None], table, NEG)


def _merge_kernel(x_ref, ya_ref, yc_ref, g_ref, wg_ref, bgate_ref, wpa_ref, wpc_ref, wo_ref,
                  gffn_ref, wrt_ref, brt_ref, h_ref, stage_ref, route_ref, cnt_ref, n2_scr, logit_scr):
    @pl.when(pl.program_id(0) == 0)
    def _():
        n2_scr[...] = jnp.zeros_like(n2_scr)
        logit_scr[...] = jnp.zeros_like(logit_scr)

    row8 = lax.broadcasted_iota(jnp.int32, (SUBLANES, TM), 0).astype(F32)
    ninf = -jnp.inf

    def argmax_first(vals):
        mx = vals.max(axis=0, keepdims=True)
        idx = jnp.where(vals == mx, row8, float(SUBLANES)).min(axis=0, keepdims=True)
        return mx, idx

    def route_tile(sub):
        rows = slice(sub * TM, (sub + 1) * TM)
        lt = logit_scr[rows, :].T
        gl = jnp.where(row8 < N_GROUPS, lt[N_EXPERTS:N_EXPERTS + SUBLANES, :], ninf)
        gmax, grp = argmax_first(gl)
        p_grp = 1.0 / jnp.exp(gl - gmax).sum(axis=0, keepdims=True)
        el = lt[0:EXPERTS_PER_GROUP, :]
        for g in range(1, N_GROUPS):
            el = jnp.where(grp == g, lt[g * EXPERTS_PER_GROUP:(g + 1) * EXPERTS_PER_GROUP, :], el)
        l1, i1 = argmax_first(el)
        l2, i2 = argmax_first(jnp.where(row8 == i1, ninf, el))
        e2 = jnp.exp(l2 - l1)
        den = 1.0 + e2
        w1 = p_grp * (1.0 / den)
        w2 = p_grp * (e2 / den)
        x1 = grp * EXPERTS_PER_GROUP + i1
        x2 = grp * EXPERTS_PER_GROUP + i2

        row_e = lax.broadcasted_iota(jnp.int32, (N_EXPERTS, TM), 0).astype(F32)
        oh1 = (row_e == x1).astype(F32)
        oh2 = (row_e == x2).astype(F32)
        oh = (oh1 + oh2).astype(BF16)
        r = lax.broadcasted_iota(jnp.int32, (TM, TM), 0)
        c = lax.broadcasted_iota(jnp.int32, (TM, TM), 1)
        earlier_tok = _dot(oh, (r < c).astype(BF16))
        er = lax.broadcasted_iota(jnp.int32, (N_EXPERTS, N_EXPERTS), 0)
        ec = lax.broadcasted_iota(jnp.int32, (N_EXPERTS, N_EXPERTS), 1)
        lower_exp = _dot((ec < er).astype(BF16), oh).sum(axis=1, keepdims=True)
        where = earlier_tok + lower_exp
        pos1 = (oh1 * where).sum(axis=0, keepdims=True)
        pos2 = (oh2 * where).sum(axis=0, keepdims=True)
        counts = lax.dot_general(jnp.ones((SUBLANES, TM), BF16), oh, (((1,), (1,)), ((), ())),
                                 preferred_element_type=F32)
        cnt_ref[sub] = jnp.concatenate([counts, jnp.zeros((SUBLANES, LANES - N_EXPERTS), F32)], axis=1)

        route_t = jnp.zeros((SUBLANES, TM), F32)
        for j, val in enumerate((x1, x2, w1, w2, pos1, pos2)):
            route_t = jnp.where(row8 == j, val, route_t)
        route_ref[rows, :] = jnp.concatenate([route_t, jnp.zeros((LANES - SUBLANES, TM), F32)], axis=0).T
        return pos1, pos2

    n2_old = [n2_scr[sub * TM:(sub + 1) * TM, :] for sub in range(MERGE_TILES)]
    positions = [route_tile(sub) for sub in range(MERGE_TILES)]

    x = x_ref[...]
    nb = _rms(x, g_ref[...]).astype(BF16)
    sga = _sigmoid(_dot(nb, wg_ref[:, 0:D_MODEL]) + bgate_ref[:, 0:D_MODEL])
    ma = sga * _dot(ya_ref[...], wpa_ref[...])
    sgc = _sigmoid(_dot(nb, wg_ref[:, D_MODEL:2 * D_MODEL]) + bgate_ref[:, D_MODEL:2 * D_MODEL])
    m = ma + sgc * _dot(yc_ref[...], wpc_ref[...])
    h = x + _dot(m.astype(BF16), wo_ref[...])
    h_ref[...] = h
    n2_new = _rms(h, gffn_ref[...]).astype(BF16)
    n2_scr[...] = n2_new
    logit_scr[...] = _dot(n2_new, wrt_ref[...]) + brt_ref[...]

    slot = lax.broadcasted_iota(jnp.int32, (TILE_ROWS, TM), 0).astype(F32)
    for sub, (pos1, pos2) in enumerate(positions):
        place = ((slot == pos1) | (slot == pos2)).astype(BF16)
        _pack_rows(stage_ref, _dot(place, n2_old[sub]), TILE_ROWS, is_bf16=True, first_row=sub * TILE_ROWS)


def _merge(x2, ya, yc, g_mix, wgate, bgate, wpa, wpc, wo, gffn, wrt, brt):
    t = x2.shape[0]
    n_tiles = t // TM
    n_steps = n_tiles // MERGE_TILES
    tmm = MERGE_TILES * TM
    const = lambda i: (0, 0)
    row = lambda i: (jnp.minimum(i, n_steps - 1), 0)
    late = lambda i: (jnp.maximum(i - 1, 0), 0)
    return pl.pallas_call(
        _merge_kernel,
        grid=(n_steps + 1,),
        in_specs=[
            pl.BlockSpec((tmm, D_MODEL), row),
            pl.BlockSpec((tmm, ATTN_W), row),
            pl.BlockSpec((tmm, CONV_W), row),
            pl.BlockSpec((1, D_MODEL), const),
            pl.BlockSpec((D_MODEL, 2 * D_MODEL), const),
            pl.BlockSpec((1, 2 * D_MODEL), const),
            pl.BlockSpec((ATTN_W, D_MODEL), const),
            pl.BlockSpec((CONV_W, D_MODEL), const),
            pl.BlockSpec((D_MODEL, D_MODEL), const),
            pl.BlockSpec((1, D_MODEL), const),
            pl.BlockSpec((D_MODEL, LANES), const),
            pl.BlockSpec((1, LANES), const),
        ],
        out_specs=[
            pl.BlockSpec((tmm, D_MODEL), row),
            pl.BlockSpec((MERGE_TILES * TILE_ROWS * PACK_ROWS, LANES), late),
            pl.BlockSpec((tmm, LANES), late),
            pl.BlockSpec((MERGE_TILES, SUBLANES, LANES), lambda i: (jnp.maximum(i - 1, 0), 0, 0)),
        ],
        out_shape=[
            jax.ShapeDtypeStruct((t, D_MODEL), F32),
            jax.ShapeDtypeStruct((n_tiles * TILE_ROWS * PACK_ROWS, LANES), U32),
            jax.ShapeDtypeStruct((t, LANES), F32),
            jax.ShapeDtypeStruct((n_tiles, SUBLANES, LANES), F32),
        ],
        scratch_shapes=[pltpu.VMEM((tmm, D_MODEL), BF16), pltpu.VMEM((tmm, LANES), F32)],
        compiler_params=pltpu.CompilerParams(
            dimension_semantics=("arbitrary",), vmem_limit_bytes=VMEM_LIMIT),
        name="merge",
    )(x2, ya, yc, g_mix, wgate, bgate, wpa, wpc, wo, gffn, wrt, brt)


def _piece_counts(n):
    return [(n >> (size.bit_length() - 1)) & 1 for size in SEG_SIZES]


ALL_CLASSES = tuple(range(len(SEG_SIZES)))
BIG_CLASSES = tuple(c for c in ALL_CLASSES if SEG_SIZES[c] >= BIG_PIECE)
SMALL_CLASSES = tuple(c for c in ALL_CLASSES if SEG_SIZES[c] < BIG_PIECE)


def _segment_pieces(n, visit, classes=ALL_CLASSES):
    for cls in classes:
        size = SEG_SIZES[cls]

        @pl.when((n & size) != 0)
        def _(cls=cls, size=size):
            visit(cls, n & ~(2 * size - 1))


def _piece_copy(src_ref, dst_ref, sems, cls, src_row, dst_row):
    n = SEG_SIZES[cls] * PACK_ROWS
    return pltpu.make_async_copy(src_ref.at[pl.ds(src_row * PACK_ROWS, n), :],
                                 dst_ref.at[pl.ds(dst_row * PACK_ROWS, n), :], sems.at[cls])


def _drain(src_ref, dst_ref, sems, counts, has_big):
    unroll = 4

    def wait_classes(classes):
        for cls in classes:
            def wait_some(k, cls=cls):
                def body(t, carry):
                    for _ in range(k):
                        _piece_copy(src_ref, dst_ref, sems, cls, 0, 0).wait()
                    return carry
                return body

            n = counts[cls]
            lax.fori_loop(0, n >> (unroll.bit_length() - 1), wait_some(unroll), 0)
            lax.fori_loop(0, n & (unroll - 1), wait_some(1), 0)

    @pl.when(has_big)
    def _():
        wait_classes(BIG_CLASSES)

    wait_classes(SMALL_CLASSES)


def _experts_kernel(be_ref, nb_ref, base_ref, jlo_ref, jhi_ref, nv_ref, big_ref, npiece_ref, cum_ref, end_ref,
                    src_ref, stage_hbm, w1_ref, w3_ref, w2_ref, ys_ref, *scratch, n_tiles, n_blocks):
    xbufs = scratch[:GATHER_AHEAD + 1]
    w1b_ref, w3b_ref, w2b_ref, sems = scratch[GATHER_AHEAD + 1:]
    s = pl.program_id(0)
    nb = nb_ref[0]
    n_cls = len(SEG_SIZES)

    def segment_copies(step, j, live, buf, sem, classes=ALL_CLASSES):
        base = base_ref[step]
        g = jnp.minimum(j, n_tiles - 1) * N_EXPERTS + be_ref[step]
        lo = jnp.maximum(cum_ref[g], base)
        hi = jnp.minimum(end_ref[g], base + ROW_BLK)
        src = src_ref[g] + lo
        dst = lo - base
        _segment_pieces(jnp.where(live, jnp.maximum(hi - lo, 0), 0),
                        lambda cls, o: _piece_copy(stage_hbm, buf, sem, cls, src + o, dst + o).start(), classes)

    def looped_copies(step, j0, j1, buf, sem, classes=ALL_CLASSES):
        def body(j, carry):
            segment_copies(step, j, True, buf, sem, classes)
            return carry

        lax.fori_loop(j0, j1, body, 0)

    @pl.when(s == 0)
    def _():
        for buf in xbufs:
            buf[...] = jnp.zeros_like(buf)
        for first in range(GATHER_AHEAD):
            blk = min(first, n_blocks - 1)
            looped_copies(blk, jlo_ref[blk], jnp.where(first < nb, jhi_ref[blk], jlo_ref[blk]),
                          xbufs[first], sems.at[first])

    def step(cur, cur_sem, nxt, nxt_sem):
        prev = be_ref[jnp.maximum(s - 1, 0)]

        @pl.when((s == 0) | (be_ref[s] != prev))
        def _():
            w1b_ref[...] = w1_ref[0].astype(BF16)
            w3b_ref[...] = w3_ref[0].astype(BF16)
            w2b_ref[...] = w2_ref[0].astype(BF16)

        _drain(stage_hbm, cur, cur_sem, [npiece_ref[s * n_cls + c] for c in range(n_cls)], big_ref[s] != 0)

        nxt_step = jnp.minimum(s + GATHER_AHEAD, n_blocks - 1)
        live = s + GATHER_AHEAD < nb
        j0 = jlo_ref[nxt_step]
        j1 = jnp.where(live, jhi_ref[nxt_step], j0)
        looped_copies(nxt_step, j0 + GATHER_UNROLL, j1, nxt, nxt_sem, SMALL_CLASSES)
        looped_copies(nxt_step, j0, jnp.where(big_ref[nxt_step] != 0, j1, j0), nxt, nxt_sem, BIG_CLASSES)
        groups = iter(_split(range(GATHER_UNROLL), 2 * PACK_ROWS))

        def start_group():
            for k in next(groups):
                segment_copies(nxt_step, j0 + k, j0 + k < j1, nxt, nxt_sem, SMALL_CLASSES)

        a = None
        g = None
        for blk in range(PACK_ROWS):
            start_group()
            xa = _unpack_block(cur, ROW_BLK, blk, n_valid=nv_ref[s])
            rows = slice(blk * PACK_W, (blk + 1) * PACK_W)
            da = _dot(xa, w1b_ref[rows, :])
            dg = _dot(xa, w3b_ref[rows, :])
            a = da if a is None else a + da
            g = dg if g is None else g + dg
        hdn = ((a * _sigmoid(a)) * g).astype(BF16)
        for blk in range(PACK_ROWS):
            start_group()
            _pack_block(ys_ref, _dot(hdn, w2b_ref[:, blk * PACK_W:(blk + 1) * PACK_W]), ROW_BLK, blk)

    n_buf = len(xbufs)
    for slot in range(n_buf):
        @pl.when((s < nb) & (s % n_buf == slot))
        def _(slot=slot):
            ahead = (slot + GATHER_AHEAD) % n_buf
            step(xbufs[slot], sems.at[slot], xbufs[ahead], sems.at[ahead])

    @pl.when(s >= nb)
    def _():
        ys_ref[...] = jnp.zeros_like(ys_ref)


def _experts(blk_e, nblk, base, jlo, jhi, nvalid, big, npiece, cum, end, src, stage, w1, w3, w2):
    n_blocks = blk_e.shape[0]
    n_tiles = cum.shape[0] // N_EXPERTS

    def wsel(s, be, nb, *_):
        return (be[jnp.minimum(s, nb[0] - 1)], 0, 0)

    grid_spec = pltpu.PrefetchScalarGridSpec(
        num_scalar_prefetch=11,
        grid=(n_blocks,),
        in_specs=[
            pl.BlockSpec(memory_space=pl.ANY),
            pl.BlockSpec((1, D_MODEL, D_EXPERT), wsel),
            pl.BlockSpec((1, D_MODEL, D_EXPERT), wsel),
            pl.BlockSpec((1, D_EXPERT, D_MODEL), wsel),
        ],
        out_specs=pl.BlockSpec((ROW_BLK * PACK_ROWS, LANES), lambda s, *_: (s, 0)),
        scratch_shapes=[pltpu.VMEM((ROW_BLK * PACK_ROWS, LANES), U32)] * (GATHER_AHEAD + 1) + [
            pltpu.VMEM((D_MODEL, D_EXPERT), BF16),
            pltpu.VMEM((D_MODEL, D_EXPERT), BF16),
            pltpu.VMEM((D_EXPERT, D_MODEL), BF16),
            pltpu.SemaphoreType.DMA((GATHER_AHEAD + 1, len(SEG_SIZES))),
        ],
    )
    return pl.pallas_call(
        functools.partial(_experts_kernel, n_tiles=n_tiles, n_blocks=n_blocks),
        grid_spec=grid_spec,
        out_shape=jax.ShapeDtypeStruct((n_blocks * ROW_BLK * PACK_ROWS, LANES), U32),
        compiler_params=pltpu.CompilerParams(
            dimension_semantics=("arbitrary",), vmem_limit_bytes=VMEM_LIMIT),
        name="experts",
    )(blk_e, nblk, base, jlo, jhi, nvalid, big, npiece, cum, end, src, stage, w1, w3, w2)


def _combine_kernel(cnt_ref, off_ref, dst_ref, big_ref, npiece_ref, h_ref, route_ref, p_ref, gple_ref, wpg_ref,
                    bpg_ref, wpp_ref, ys_hbm, o_ref, *scratch, n_tiles):
    ybufs, sems = scratch[:-1], scratch[-1]
    i = pl.program_id(0)
    n_cls = len(SEG_SIZES)

    def segment_copies(tile, e, live, slot, classes=ALL_CLASSES):
        g = tile * N_EXPERTS + e
        off, dst = off_ref[g], dst_ref[g]
        _segment_pieces(jnp.where(live, cnt_ref[g], 0),
                        lambda cls, o: _piece_copy(ys_hbm, ybufs[slot], sems.at[slot], cls, dst + o, off + o).start(),
                        classes)

    def looped_copies(tile, n_experts, slot, classes=ALL_CLASSES):
        def body(e, carry):
            segment_copies(tile, e, True, slot, classes)
            return carry

        lax.fori_loop(0, n_experts, body, 0)

    @pl.when(i == 0)
    def _():
        for first in range(COMBINE_TILES):
            looped_copies(first, N_EXPERTS, first)

    def step(cur_slots, nxt_slots):
        tiles = [i * COMBINE_TILES + sub for sub in range(COMBINE_TILES)]
        for tile, slot in zip(tiles, cur_slots):
            _drain(ys_hbm, ybufs[slot], sems.at[slot], [npiece_ref[tile * n_cls + c] for c in range(n_cls)],
                   big_ref[tile] != 0)
        ahead = [(jnp.minimum(tile + COMBINE_TILES, n_tiles - 1), tile + COMBINE_TILES < n_tiles, slot)
                 for tile, slot in zip(tiles, nxt_slots)]
        for tile, live, slot in ahead:
            looped_copies(tile, jnp.where(live & (big_ref[tile] != 0), N_EXPERTS, 0), slot, BIG_CLASSES)
        groups = iter(_split([(a, e) for a in ahead for e in range(N_EXPERTS)], COMBINE_TILES * PACK_ROWS + 2))

        def start_group():
            for (tile, live, slot), e in next(groups):
                segment_copies(tile, e, live, slot, SMALL_CLASSES)

        start_group()
        pp = _dot(p_ref[...].astype(BF16), wpp_ref[...])
        place = lax.broadcasted_iota(jnp.int32, (TM, TILE_ROWS), 1).astype(F32)
        moe_rows = []
        for sub, slot in enumerate(cur_slots):
            route = route_ref[sub * TM:(sub + 1) * TM, :]
            sel = [(place == route[:, 4 + kk:5 + kk]).astype(BF16) for kk in range(TOP_K)]
            moe = []
            for blk in range(PACK_ROWS):
                start_group()
                cols = _unpack_block(ybufs[slot], TILE_ROWS, blk)
                moe.append(_dot(sel[0], cols) * route[:, 2:3] + _dot(sel[1], cols) * route[:, 3:4])
            moe_rows.append(jnp.concatenate(moe, axis=1))
        start_group()
        h = h_ref[...] + jnp.concatenate(moe_rows, axis=0)
        gate = _sigmoid(_dot(_rms(h, gple_ref[...]).astype(BF16), wpg_ref[...]) + bpg_ref[...])
        o_ref[...] = h + gate * pp

    for parity in range(2):
        @pl.when(i % 2 == parity)
        def _(parity=parity):
            half = [list(range(p * COMBINE_TILES, (p + 1) * COMBINE_TILES)) for p in range(2)]
            step(half[parity], half[1 - parity])


def _combine(cnt, off, dst, big, npiece, h1, route, p2, gple, wpg, bpg, wpp, ys):
    t = h1.shape[0]
    n_tiles = t // TM
    rows = COMBINE_TILES * TM
    const = lambda i, *_: (0, 0)
    row = lambda i, *_: (i, 0)
    grid_spec = pltpu.PrefetchScalarGridSpec(
        num_scalar_prefetch=5,
        grid=(n_tiles // COMBINE_TILES,),
        in_specs=[
            pl.BlockSpec((rows, D_MODEL), row),
            pl.BlockSpec((rows, LANES), row),
            pl.BlockSpec((rows, PLE_DIM), row),
            pl.BlockSpec((1, D_MODEL), const),
            pl.BlockSpec((D_MODEL, D_MODEL), const),
            pl.BlockSpec((1, D_MODEL), const),
            pl.BlockSpec((PLE_DIM, D_MODEL), const),
            pl.BlockSpec(memory_space=pl.ANY),
        ],
        out_specs=pl.BlockSpec((rows, D_MODEL), row),
        scratch_shapes=[pltpu.VMEM((TILE_ROWS * PACK_ROWS, LANES), U32)] * (2 * COMBINE_TILES) + [
            pltpu.SemaphoreType.DMA((2 * COMBINE_TILES, len(SEG_SIZES))),
        ],
    )
    return pl.pallas_call(
        functools.partial(_combine_kernel, n_tiles=n_tiles),
        grid_spec=grid_spec,
        out_shape=jax.ShapeDtypeStruct((t, D_MODEL), F32),
        compiler_params=pltpu.CompilerParams(
            dimension_semantics=("arbitrary",), vmem_limit_bytes=VMEM_LIMIT),
        name="combine",
    )(cnt, off, dst, big, npiece, h1, route, p2, gple, wpg, bpg, wpp, ys)


def _layer(h, p_i, g_mix, w_in, b_in, g_q, g_k, rel_bias, conv_w, conv_b, w_pa, w_pc, w_o,
           g_ffn, w_group, b_group, w_router, b_router, w1, w3, w2,
           g_ple, w_ple_gate, b_ple_gate, w_ple_proj):
    b, s, d = h.shape
    t = b * s
    x2 = h.reshape(t, d)
    row2 = lambda a: a.reshape(1, -1).astype(F32)

    qkv_w = 3 * ATTN_W
    conv_end = qkv_w + 3 * CONV_W
    w_in_b = w_in.astype(BF16)
    gq = row2(jnp.tile(g_q.astype(F32) * (HEAD_DIM ** -0.5 * LOG2E), N_HEADS))
    gk = row2(jnp.tile(g_k.astype(F32), N_HEADS))
    head = jnp.arange(ATTN_W) // HEAD_DIM
    hmat = jnp.where(head[:, None] == head[None, :], 1.0 / HEAD_DIM, 0.0).astype(BF16)
    cw = jnp.concatenate([conv_w.astype(F32), jnp.zeros((SUBLANES - CONV_K, CONV_W), F32)], axis=0)

    q, k, v, yc = _inproj(x2, row2(g_mix), w_in_b[:, :qkv_w], w_in_b[:, qkv_w:conv_end],
                          row2(b_in[:conv_end]), gq, gk, hmat, cw, row2(conv_b), s)

    ya = _attention(q.reshape(b, s, ATTN_W), k.reshape(b, s, ATTN_W), v.reshape(b, s, ATTN_W),
                    _attn_bias(rel_bias)).reshape(t, ATTN_W)

    n_pad = LANES - N_GROUPS - N_EXPERTS
    wrt = jnp.concatenate([w_router, w_group, jnp.zeros((d, n_pad), w_group.dtype)], axis=1).astype(BF16)
    brt = row2(jnp.concatenate([b_router, b_group, jnp.zeros((n_pad,), b_group.dtype)]))
    h1, stage, route, cnt_f = _merge(x2, ya, yc, row2(g_mix), w_in_b[:, conv_end:], row2(b_in[conv_end:]),
                                     w_pa.astype(BF16), w_pc.astype(BF16), w_o.astype(BF16),
                                     row2(g_ffn), wrt, brt)

    n_tiles = t // TM
    cnt = cnt_f[:, 0, :N_EXPERTS].astype(jnp.int32)
    tile_off = jnp.cumsum(cnt, axis=1) - cnt
    tot = cnt.sum(axis=0)
    pcounts = (tot + ROW_BLK - 1) // ROW_BLK * ROW_BLK
    pends = jnp.cumsum(pcounts)
    pstarts = pends - pcounts
    cum = jnp.cumsum(cnt, axis=0) - cnt
    dst = pstarts[None, :] + cum
    n_blocks = (t * TOP_K) // ROW_BLK + N_EXPERTS
    blk_start = jnp.arange(n_blocks, dtype=jnp.int32) * ROW_BLK
    blk_e = jnp.minimum((pends[None, :] <= blk_start[:, None]).sum(axis=1), N_EXPERTS - 1).astype(jnp.int32)
    nblk = (pends[-1:] // ROW_BLK).astype(jnp.int32)
    sel = (jnp.arange(N_EXPERTS, dtype=jnp.int32)[:, None] == blk_e[None, :]).astype(jnp.int32)
    of_block = lambda a: (a[..., None] * sel).sum(axis=-2)
    base = blk_start - of_block(pstarts)
    nvalid = jnp.clip(of_block(tot) - base, 0, ROW_BLK)
    cum_e = of_block(cum)
    cnt_e = of_block(cnt)
    jlo = (cum_e + cnt_e <= base[None, :]).sum(axis=0)
    jhi = (cum_e < base[None, :] + ROW_BLK).sum(axis=0)
    part = jnp.clip(jnp.minimum(cum_e + cnt_e, base[None, :] + ROW_BLK) - jnp.maximum(cum_e, base[None, :]),
                    0, ROW_BLK)
    blk_pieces = jnp.stack(_piece_counts(part), axis=-1).sum(axis=0)
    tile_pieces = jnp.stack(_piece_counts(cnt), axis=-1).sum(axis=1)
    src = jnp.arange(n_tiles, dtype=jnp.int32)[:, None] * TILE_ROWS + tile_off - cum
    flat = lambda a: a.reshape(-1).astype(jnp.int32)

    blk_big = (part >= BIG_PIECE).any(axis=0)
    tile_big = (cnt >= BIG_PIECE).any(axis=1)

    ys = _experts(blk_e, nblk, flat(base), flat(jlo), flat(jhi), flat(nvalid), flat(blk_big), flat(blk_pieces),
                  flat(cum), flat(cum + cnt), flat(src), stage, w1, w3, w2)
    out = _combine(flat(cnt), flat(tile_off), flat(dst), flat(tile_big), flat(tile_pieces), h1, route,
                   p_i.reshape(t, PLE_DIM), row2(g_ple), w_ple_gate.astype(BF16), row2(b_ple_gate),
                   w_ple_proj.astype(BF16), ys)
    return out.reshape(b, s, d)


def kernel(x, p, g_mix, w_in, b_in, g_q, g_k, rel_bias, conv_w, conv_b, w_pa, w_pc, w_o, g_ffn, w_group, b_group, w_router, b_router, w1, w3, w2, g_ple, w_ple_gate, b_ple_gate, w_ple_proj):
    h = x
    for i in range(p.shape[0]):
        h = _layer(h, p[i], g_mix[i], w_in[i], b_in[i], g_q[i], g_k[i], rel_bias[i], conv_w[i], conv_b[i],
                   w_pa[i], w_pc[i], w_o[i], g_ffn[i], w_group[i], b_group[i], w_router[i], b_router[i],
                   w1[i], w3[i], w2[i], g_ple[i], w_ple_gate[i], b_ple_gate[i], w_ple_proj[i])
    return h
```

```python
import functools

import jax
import jax.numpy as jnp
from jax import lax
from jax.experimental import pallas as pl
from jax.experimental.pallas import tpu as pltpu

D_MODEL = 1024
CHUNK = 64
LEFT_CHUNKS = 8
N_HEADS = 8
HEAD_DIM = 64
ATTN_W = N_HEADS * HEAD_DIM
CONV_W = D_MODEL // 2
CONV_K = 3
MAX_REL_PAST = 256
PLE_DIM = 256
N_GROUPS = 4
EXPERTS_PER_GROUP = 8
N_EXPERTS = N_GROUPS * EXPERTS_PER_GROUP
TOP_K = 2
D_EXPERT = 512
EPS = 1e-6
NEG = -1e30
LOG2E = 1.4426950408889634

LANES = 128
SUBLANES = 8
TM = 256
TI = 1024
MERGE_TILES = 4
COMBINE_TILES = 2
TQ = 256
KV_SLABS = 1 + (LEFT_CHUNKS * CHUNK) // TQ
ROW_BLK = 512
TILE_ROWS = TOP_K * TM
PACK_ROWS = D_MODEL // (2 * LANES)
PACK_W = 2 * LANES
SEG_SIZES = tuple(TM >> k for k in range(TM.bit_length()))
BIG_PIECE = 32
ATTN_BATCH = 2
SCORE_AHEAD = 2
GATHER_AHEAD = 2
GATHER_UNROLL = 36
DMA_PRIORITIES = 2
VMEM_LIMIT = 56 * 1024 * 1024

F32 = jnp.float32
BF16 = jnp.bfloat16
U32 = jnp.uint32


def _dot(a, b):
    return jnp.dot(a, b, preferred_element_type=F32)


def _rms(x, g):
    ms = jnp.mean(x * x, axis=-1, keepdims=True)
    return (x * lax.rsqrt(ms + EPS)) * g


def _sigmoid(x):
    return 1.0 / (1.0 + jnp.exp(-x))


def _pack_block(ref, vals, n_rows, a, is_bf16=False, first_row=0):
    lo = vals[:, 0:LANES]
    hi = vals[:, LANES:PACK_W]
    if not is_bf16:
        lo = lo.astype(BF16).astype(F32)
        hi = hi.astype(BF16).astype(F32)
    ref[pl.ds(first_row * PACK_ROWS + a, n_rows, stride=PACK_ROWS), :] = (
        lax.bitcast_convert_type(hi, U32) | (lax.bitcast_convert_type(lo, U32) >> 16))


def _pack_rows(ref, vals, n_rows, is_bf16=False, first_row=0):
    for a in range(PACK_ROWS):
        _pack_block(ref, vals[:, a * PACK_W:(a + 1) * PACK_W], n_rows, a, is_bf16, first_row)


def _unpack_block(ref, n_rows, a, n_valid=None):
    word = ref[pl.ds(a, n_rows, stride=PACK_ROWS), :]
    if n_valid is not None:
        word = jnp.where(lax.broadcasted_iota(jnp.int32, (n_rows, LANES), 0) < n_valid, word, U32(0))
    lo = lax.bitcast_convert_type(word << 16, F32).astype(BF16)
    hi = lax.bitcast_convert_type(word & U32(0xFFFF0000), F32).astype(BF16)
    return jnp.concatenate([lo, hi], axis=1)


def _split(items, n_groups):
    items = list(items)
    return [items[len(items) * g // n_groups:len(items) * (g + 1) // n_groups] for g in range(n_groups)]


def _inproj_kernel(x_ref, g_ref, wqkv_ref, wconv_ref, b_ref, gq_ref, gk_ref, hm_ref,
                   cw_ref, cb_ref, q_ref, k_ref, v_ref, yc_ref, carry_ref, *, tiles_per_seq):
    i = pl.program_id(0)
    nb = _rms(x_ref[...], g_ref[...]).astype(BF16)

    zq = _dot(nb, wqkv_ref[...]) + b_ref[:, 0:3 * ATTN_W]
    hm = hm_ref[...]

    def head_rms(t, g):
        ms = _dot((t * t).astype(BF16), hm)
        return (t * lax.rsqrt(ms + EPS)) * g

    q_ref[...] = head_rms(zq[:, 0:ATTN_W], gq_ref[...]).astype(BF16)
    k_ref[...] = head_rms(zq[:, ATTN_W:2 * ATTN_W], gk_ref[...]).astype(BF16)
    v_ref[...] = zq[:, 2 * ATTN_W:3 * ATTN_W].astype(BF16)

    zc = _dot(nb, wconv_ref[...]) + b_ref[:, 3 * ATTN_W:3 * ATTN_W + 3 * CONV_W]
    u = zc[:, 0:CONV_W]
    bg = zc[:, CONV_W:2 * CONV_W]
    cg = zc[:, 2 * CONV_W:3 * CONV_W]
    cu = cg * u

    @pl.when((i % tiles_per_seq) == 0)
    def _():
        carry_ref[...] = jnp.zeros_like(carry_ref)

    prev = carry_ref[...]
    carry_ref[...] = cu[TI - SUBLANES:TI, :]
    row = lax.broadcasted_iota(jnp.int32, (SUBLANES, CONV_W), 0)

    def shifted(s):
        r = pltpu.roll(cu, s, 0)
        p = pltpu.roll(prev, s, 0)
        top = jnp.where(row < s, p, r[0:SUBLANES, :])
        return jnp.concatenate([top, r[SUBLANES:, :]], axis=0)

    y = cb_ref[...] + cw_ref[0:1, :] * shifted(2)
    y = y + cw_ref[1:2, :] * shifted(1)
    y = y + cw_ref[2:3, :] * cu
    yc_ref[...] = (bg * y).astype(BF16)


def _inproj(x2, g_mix, wqkv, wconv, b_in, gq, gk, hmat, cw, cb, seq):
    t = x2.shape[0]
    const = lambda i: (0, 0)
    row = lambda i: (i, 0)
    out = jax.ShapeDtypeStruct((t, ATTN_W), BF16)
    return pl.pallas_call(
        functools.partial(_inproj_kernel, tiles_per_seq=seq // TI),
        grid=(t // TI,),
        in_specs=[
            pl.BlockSpec((TI, D_MODEL), row),
            pl.BlockSpec((1, D_MODEL), const),
            pl.BlockSpec((D_MODEL, 3 * ATTN_W), const),
            pl.BlockSpec((D_MODEL, 3 * CONV_W), const),
            pl.BlockSpec((1, 3 * ATTN_W + 3 * CONV_W), const),
            pl.BlockSpec((1, ATTN_W), const),
            pl.BlockSpec((1, ATTN_W), const),
            pl.BlockSpec((ATTN_W, ATTN_W), const),
            pl.BlockSpec((SUBLANES, CONV_W), const),
            pl.BlockSpec((1, CONV_W), const),
        ],
        out_specs=[pl.BlockSpec((TI, ATTN_W), row)] * 4,
        out_shape=[out] * 4,
        scratch_shapes=[pltpu.VMEM((SUBLANES, CONV_W), F32)],
        compiler_params=pltpu.CompilerParams(
            dimension_semantics=("arbitrary",), vmem_limit_bytes=VMEM_LIMIT),
        name="inproj",
    )(x2, g_mix, wqkv, wconv, b_in, gq, gk, hmat, cw, cb)


def _lane_fold(parts, op):
    acc = None
    for a in parts:
        for c in range(0, a.shape[1], LANES):
            piece = a[:, c:c + LANES]
            acc = piece if acc is None else op(acc, piece)
    return acc


def _attn_kernel(q_ref, k0_ref, k1_ref, k2_ref, v0_ref, v1_ref, v2_ref, bias_ref, o_ref):
    k_refs = (k0_ref, k1_ref, k2_ref)
    v_refs = (v0_ref, v1_ref, v2_ref)
    pair_w = 2 * HEAD_DIM
    lane = lax.broadcasted_iota(jnp.int32, (TQ, pair_w), 1)
    low = lane < HEAD_DIM

    def scores(bb, h, pens):
        ps = slice((h // 2) * pair_w, (h // 2 + 1) * pair_w)
        q_pair = q_ref[bb, :, ps]
        own = low if h % 2 == 0 else jnp.logical_not(low)
        qh = jnp.where(own, q_pair, jnp.zeros_like(q_pair))
        s = [lax.dot_general(qh, k_refs[j][bb, :, ps], (((1,), (1,)), ((), ())),
                             preferred_element_type=F32) + bias_ref[h, :, j * TQ:(j + 1) * TQ]
             for j in range(KV_SLABS)]
        return s if pens is None else [sj + pens[j] for j, sj in enumerate(s)]

    def weighted(bb, h, s):
        ps = slice((h // 2) * pair_w, (h // 2 + 1) * pair_w)
        m = _lane_fold(s, jnp.maximum).max(axis=-1, keepdims=True)
        e = [jnp.exp2(sj - m) for sj in s]
        l = _lane_fold(e, jnp.add).sum(axis=-1, keepdims=True)
        acc = None
        for j in range(KV_SLABS):
            oj = _dot(e[j].astype(BF16), v_refs[j][bb, :, ps])
            acc = oj if acc is None else acc + oj
        return acc * (1.0 / l)

    def all_heads(pens):
        items = [(bb, h) for h in range(N_HEADS) for bb in range(ATTN_BATCH)]
        pending = [scores(bb, h, pens) for bb, h in items[:SCORE_AHEAD]]
        o_even = {}
        for n, (bb, h) in enumerate(items):
            if n + SCORE_AHEAD < len(items):
                pending.append(scores(*items[n + SCORE_AHEAD], pens))
            o = weighted(bb, h, pending.pop(0))
            if h % 2 == 0:
                o_even[bb] = o
            else:
                ps = slice((h // 2) * pair_w, (h // 2 + 1) * pair_w)
                o_ref[bb, :, ps] = jnp.where(low, o_even[bb], o).astype(BF16)

    i = pl.program_id(1)

    @pl.when(i >= KV_SLABS - 1)
    def _():
        all_heads(None)

    @pl.when(i < KV_SLABS - 1)
    def _():
        all_heads([jnp.where(i >= KV_SLABS - 1 - j, 0.0, NEG).astype(F32) for j in range(KV_SLABS)])


def _attention(q, k, v, bias):
    b, s, _ = q.shape
    blk = (ATTN_BATCH, TQ, ATTN_W)

    def kv_map(j):
        back = KV_SLABS - 1 - j
        return lambda bi, i: (bi, jnp.maximum(i - back, 0), 0)

    kv_specs = [pl.BlockSpec(blk, kv_map(j)) for j in range(KV_SLABS)]
    return pl.pallas_call(
        _attn_kernel,
        grid=(b // ATTN_BATCH, s // TQ),
        in_specs=[pl.BlockSpec(blk, lambda bi, i: (bi, i, 0))] + kv_specs + kv_specs + [
            pl.BlockSpec((N_HEADS, TQ, KV_SLABS * TQ), lambda bi, i: (0, 0, 0))],
        out_specs=pl.BlockSpec(blk, lambda bi, i: (bi, i, 0)),
        out_shape=jax.ShapeDtypeStruct((b, s, ATTN_W), BF16),
        compiler_params=pltpu.CompilerParams(
            dimension_semantics=("arbitrary", "arbitrary"), vmem_limit_bytes=VMEM_LIMIT),
        name="attn",
    )(q, k, k, k, v, v, v, bias)


def _attn_bias(rel_bias):
    nk = KV_SLABS * TQ
    past = nk - TQ
    d = jnp.arange(TQ - 1 + past, -TQ, -1)
    idx = jnp.clip(d, -(CHUNK - 1), MAX_REL_PAST) + (CHUNK - 1)
    onehot = (idx[:, None] == jnp.arange(rel_bias.shape[1])[None, :]).astype(F32)
    per_dist = jnp.einsum("dn,hn->hd", onehot, rel_bias.astype(F32) * LOG2E,
                          precision=lax.Precision.HIGHEST)
    n_h, span = per_dist.shape
    padded = jnp.pad(per_dist, ((0, 0), (0, 2)))
    skew = jnp.tile(padded, (1, TQ))[:, :TQ * (span + 1)].reshape(n_h, TQ, span + 1)
    table = skew[:, :, TQ - 1:TQ - 1 + nk]
    r = jnp.arange(TQ)[:, None]
    c = jnp.arange(nk)[None, :]
    qc = r // CHUNK
    kc = c // CHUNK
    lead = past // CHUNK - LEFT_CHUNKS
    band = (kc >= qc + lead) & (kc <= qc + lead + LEFT_CHUNKS)
    return jnp.where(band[None], table, NEG)


def _merge_kernel(x_ref, ya_ref, yc_ref, g_ref, wg_ref, bgate_ref, wpa_ref, wpc_ref, wo_ref,
                  gffn_ref, wrt_ref, brt_ref, h_ref, stage_ref, route_ref, cnt_ref, n2_scr, logit_scr):
    @pl.when(pl.program_id(0) == 0)
    def _():
        n2_scr[...] = jnp.zeros_like(n2_scr)
        logit_scr[...] = jnp.zeros_like(logit_scr)

    row8 = lax.broadcasted_iota(jnp.int32, (SUBLANES, TM), 0).astype(F32)
    ninf = -jnp.inf

    def argmax_first(vals):
        mx = vals.max(axis=0, keepdims=True)
        idx = jnp.where(vals == mx, row8, float(SUBLANES)).min(axis=0, keepdims=True)
        return mx, idx

    def route_tile(sub):
        rows = slice(sub * TM, (sub + 1) * TM)
        lt = logit_scr[rows, :].T
        gl = jnp.where(row8 < N_GROUPS, lt[N_EXPERTS:N_EXPERTS + SUBLANES, :], ninf)
        gmax, grp = argmax_first(gl)
        p_grp = 1.0 / jnp.exp(gl - gmax).sum(axis=0, keepdims=True)
        el = lt[0:EXPERTS_PER_GROUP, :]
        for g in range(1, N_GROUPS):
            el = jnp.where(grp == g, lt[g * EXPERTS_PER_GROUP:(g + 1) * EXPERTS_PER_GROUP, :], el)
        l1, i1 = argmax_first(el)
        l2, i2 = argmax_first(jnp.where(row8 == i1, ninf, el))
        e2 = jnp.exp(l2 - l1)
        den = 1.0 + e2
        w1 = p_grp * (1.0 / den)
        w2 = p_grp * (e2 / den)
        x1 = grp * EXPERTS_PER_GROUP + i1
        x2 = grp * EXPERTS_PER_GROUP + i2

        row_e = lax.broadcasted_iota(jnp.int32, (N_EXPERTS, TM), 0).astype(F32)
        oh1 = (row_e == x1).astype(F32)
        oh2 = (row_e == x2).astype(F32)
        oh = (oh1 + oh2).astype(BF16)
        r = lax.broadcasted_iota(jnp.int32, (TM, TM), 0)
        c = lax.broadcasted_iota(jnp.int32, (TM, TM), 1)
        earlier_tok = _dot(oh, (r < c).astype(BF16))
        er = lax.broadcasted_iota(jnp.int32, (N_EXPERTS, N_EXPERTS), 0)
        ec = lax.broadcasted_iota(jnp.int32, (N_EXPERTS, N_EXPERTS), 1)
        lower_exp = _dot((ec < er).astype(BF16), oh).sum(axis=1, keepdims=True)
        where = earlier_tok + lower_exp
        pos1 = (oh1 * where).sum(axis=0, keepdims=True)
        pos2 = (oh2 * where).sum(axis=0, keepdims=True)
        counts = lax.dot_general(jnp.ones((SUBLANES, TM), BF16), oh, (((1,), (1,)), ((), ())),
                                 preferred_element_type=F32)
        cnt_ref[sub] = jnp.concatenate([counts, jnp.zeros((SUBLANES, LANES - N_EXPERTS), F32)], axis=1)

        route_t = jnp.zeros((SUBLANES, TM), F32)
        for j, val in enumerate((x1, x2, w1, w2, pos1, pos2)):
            route_t = jnp.where(row8 == j, val, route_t)
        route_ref[rows, :] = jnp.concatenate([route_t, jnp.zeros((LANES - SUBLANES, TM), F32)], axis=0).T
        return pos1, pos2

    n2_old = [n2_scr[sub * TM:(sub + 1) * TM, :] for sub in range(MERGE_TILES)]
    positions = [route_tile(sub) for sub in range(MERGE_TILES)]

    x = x_ref[...]
    nb = _rms(x, g_ref[...]).astype(BF16)
    sga = _sigmoid(_dot(nb, wg_ref[:, 0:D_MODEL]) + bgate_ref[:, 0:D_MODEL])
    ma = sga * _dot(ya_ref[...], wpa_ref[...])
    sgc = _sigmoid(_dot(nb, wg_ref[:, D_MODEL:2 * D_MODEL]) + bgate_ref[:, D_MODEL:2 * D_MODEL])
    m = ma + sgc * _dot(yc_ref[...], wpc_ref[...])
    h = x + _dot(m.astype(BF16), wo_ref[...])
    h_ref[...] = h
    n2_new = _rms(h, gffn_ref[...]).astype(BF16)
    n2_scr[...] = n2_new
    logit_scr[...] = _dot(n2_new, wrt_ref[...]) + brt_ref[...]

    slot = lax.broadcasted_iota(jnp.int32, (TILE_ROWS, TM), 0).astype(F32)
    for sub, (pos1, pos2) in enumerate(positions):
        place = ((slot == pos1) | (slot == pos2)).astype(BF16)
        _pack_rows(stage_ref, _dot(place, n2_old[sub]), TILE_ROWS, is_bf16=True, first_row=sub * TILE_ROWS)


def _merge(x2, ya, yc, g_mix, wgate, bgate, wpa, wpc, wo, gffn, wrt, brt):
    t = x2.shape[0]
    n_tiles = t // TM
    n_steps = n_tiles // MERGE_TILES
    tmm = MERGE_TILES * TM
    const = lambda i: (0, 0)
    row = lambda i: (jnp.minimum(i, n_steps - 1), 0)
    late = lambda i: (jnp.maximum(i - 1, 0), 0)
    return pl.pallas_call(
        _merge_kernel,
        grid=(n_steps + 1,),
        in_specs=[
            pl.BlockSpec((tmm, D_MODEL), row),
            pl.BlockSpec((tmm, ATTN_W), row),
            pl.BlockSpec((tmm, CONV_W), row),
            pl.BlockSpec((1, D_MODEL), const),
            pl.BlockSpec((D_MODEL, 2 * D_MODEL), const),
            pl.BlockSpec((1, 2 * D_MODEL), const),
            pl.BlockSpec((ATTN_W, D_MODEL), const),
            pl.BlockSpec((CONV_W, D_MODEL), const),
            pl.BlockSpec((D_MODEL, D_MODEL), const),
            pl.BlockSpec((1, D_MODEL), const),
            pl.BlockSpec((D_MODEL, LANES), const),
            pl.BlockSpec((1, LANES), const),
        ],
        out_specs=[
            pl.BlockSpec((tmm, D_MODEL), row),
            pl.BlockSpec((MERGE_TILES * TILE_ROWS * PACK_ROWS, LANES), late),
            pl.BlockSpec((tmm, LANES), late),
            pl.BlockSpec((MERGE_TILES, SUBLANES, LANES), lambda i: (jnp.maximum(i - 1, 0), 0, 0)),
        ],
        out_shape=[
            jax.ShapeDtypeStruct((t, D_MODEL), F32),
            jax.ShapeDtypeStruct((n_tiles * TILE_ROWS * PACK_ROWS, LANES), U32),
            jax.ShapeDtypeStruct((t, LANES), F32),
            jax.ShapeDtypeStruct((n_tiles, SUBLANES, LANES), F32),
        ],
        scratch_shapes=[pltpu.VMEM((tmm, D_MODEL), BF16), pltpu.VMEM((tmm, LANES), F32)],
        compiler_params=pltpu.CompilerParams(
            dimension_semantics=("arbitrary",), vmem_limit_bytes=VMEM_LIMIT),
        name="merge",
    )(x2, ya, yc, g_mix, wgate, bgate, wpa, wpc, wo, gffn, wrt, brt)


def _piece_counts(n):
    return [(n >> (size.bit_length() - 1)) & 1 for size in SEG_SIZES]


ALL_CLASSES = tuple(range(len(SEG_SIZES)))
BIG_CLASSES = tuple(c for c in ALL_CLASSES if SEG_SIZES[c] >= BIG_PIECE)
SMALL_CLASSES = tuple(c for c in ALL_CLASSES if SEG_SIZES[c] < BIG_PIECE)


def _segment_pieces(n, visit, classes=ALL_CLASSES):
    for cls in classes:
        size = SEG_SIZES[cls]

        @pl.when((n & size) != 0)
        def _(cls=cls, size=size):
            visit(cls, n & ~(2 * size - 1))


def _piece_copy(src_ref, dst_ref, sems, cls, src_row, dst_row):
    n = SEG_SIZES[cls] * PACK_ROWS
    return pltpu.make_async_copy(src_ref.at[pl.ds(src_row * PACK_ROWS, n), :],
                                 dst_ref.at[pl.ds(dst_row * PACK_ROWS, n), :], sems.at[cls])


def _drain(src_ref, dst_ref, sems, counts, has_big):
    unroll = 4

    def wait_classes(classes):
        for cls in classes:
            def wait_some(k, cls=cls):
                def body(t, carry):
                    for _ in range(k):
                        _piece_copy(src_ref, dst_ref, sems, cls, 0, 0).wait()
                    return carry
                return body

            n = counts[cls]
            lax.fori_loop(0, n >> (unroll.bit_length() - 1), wait_some(unroll), 0)
            lax.fori_loop(0, n & (unroll - 1), wait_some(1), 0)

    @pl.when(has_big)
    def _():
        wait_classes(BIG_CLASSES)

    wait_classes(SMALL_CLASSES)


def _experts_kernel(be_ref, nb_ref, base_ref, jlo_ref, jhi_ref, nv_ref, big_ref, npiece_ref, cum_ref, end_ref,
                    src_ref, stage_hbm, w1_ref, w3_ref, w2_ref, ys_ref, *scratch, n_tiles, n_blocks):
    xbufs = scratch[:GATHER_AHEAD + 1]
    w1b_ref, w3b_ref, w2b_ref, sems = scratch[GATHER_AHEAD + 1:]
    s = pl.program_id(0)
    nb = nb_ref[0]
    n_cls = len(SEG_SIZES)

    def segment_copies(step, j, live, buf, sem, classes=ALL_CLASSES):
        base = base_ref[step]
        g = jnp.minimum(j, n_tiles - 1) * N_EXPERTS + be_ref[step]
        lo = jnp.maximum(cum_ref[g], base)
        hi = jnp.minimum(end_ref[g], base + ROW_BLK)
        src = src_ref[g] + lo
        dst = lo - base
        _segment_pieces(jnp.where(live, jnp.maximum(hi - lo, 0), 0),
                        lambda cls, o: _piece_copy(stage_hbm, buf, sem, cls, src + o, dst + o).start(
                            priority=cls % DMA_PRIORITIES), classes)

    def looped_copies(step, j0, j1, buf, sem, classes=ALL_CLASSES):
        def body(j, carry):
            segment_copies(step, j, True, buf, sem, classes)
            return carry

        lax.fori_loop(j0, j1, body, 0)

    @pl.when(s == 0)
    def _():
        for buf in xbufs:
            buf[...] = jnp.zeros_like(buf)
        for first in range(GATHER_AHEAD):
            blk = min(first, n_blocks - 1)
            looped_copies(blk, jlo_ref[blk], jnp.where(first < nb, jhi_ref[blk], jlo_ref[blk]),
                          xbufs[first], sems.at[first])

    def step(cur, cur_sem, nxt, nxt_sem):
        prev = be_ref[jnp.maximum(s - 1, 0)]

        @pl.when((s == 0) | (be_ref[s] != prev))
        def _():
            w1b_ref[...] = w1_ref[0].astype(BF16)
            w3b_ref[...] = w3_ref[0].astype(BF16)
            w2b_ref[...] = w2_ref[0].astype(BF16)

        _drain(stage_hbm, cur, cur_sem, [npiece_ref[s * n_cls + c] for c in range(n_cls)], big_ref[s] != 0)

        nxt_step = jnp.minimum(s + GATHER_AHEAD, n_blocks - 1)
        live = s + GATHER_AHEAD < nb
        j0 = jlo_ref[nxt_step]
        j1 = jnp.where(live, jhi_ref[nxt_step], j0)
        looped_copies(nxt_step, j0 + GATHER_UNROLL, j1, nxt, nxt_sem, SMALL_CLASSES)
        looped_copies(nxt_step, j0, jnp.where(big_ref[nxt_step] != 0, j1, j0), nxt, nxt_sem, BIG_CLASSES)
        groups = iter(_split(range(GATHER_UNROLL), 2 * PACK_ROWS))

        def start_group():
            for k in next(groups):
                segment_copies(nxt_step, j0 + k, j0 + k < j1, nxt, nxt_sem, SMALL_CLASSES)

        a = None
        g = None
        for blk in range(PACK_ROWS):
            start_group()
            xa = _unpack_block(cur, ROW_BLK, blk, n_valid=nv_ref[s])
            rows = slice(blk * PACK_W, (blk + 1) * PACK_W)
            da = _dot(xa, w1b_ref[rows, :])
            dg = _dot(xa, w3b_ref[rows, :])
            a = da if a is None else a + da
            g = dg if g is None else g + dg
        hdn = ((a * _sigmoid(a)) * g).astype(BF16)
        for blk in range(PACK_ROWS):
            start_group()
            _pack_block(ys_ref, _dot(hdn, w2b_ref[:, blk * PACK_W:(blk + 1) * PACK_W]), ROW_BLK, blk)

    n_buf = len(xbufs)
    for slot in range(n_buf):
        @pl.when((s < nb) & (s % n_buf == slot))
        def _(slot=slot):
            ahead = (slot + GATHER_AHEAD) % n_buf
            step(xbufs[slot], sems.at[slot], xbufs[ahead], sems.at[ahead])

    @pl.when(s >= nb)
    def _():
        ys_ref[...] = jnp.zeros_like(ys_ref)


def _experts(blk_e, nblk, base, jlo, jhi, nvalid, big, npiece, cum, end, src, stage, w1, w3, w2):
    n_blocks = blk_e.shape[0]
    n_tiles = cum.shape[0] // N_EXPERTS

    def wsel(s, be, nb, *_):
        return (be[jnp.minimum(s, nb[0] - 1)], 0, 0)

    grid_spec = pltpu.PrefetchScalarGridSpec(
        num_scalar_prefetch=11,
        grid=(n_blocks,),
        in_specs=[
            pl.BlockSpec(memory_space=pl.ANY),
            pl.BlockSpec((1, D_MODEL, D_EXPERT), wsel),
            pl.BlockSpec((1, D_MODEL, D_EXPERT), wsel),
            pl.BlockSpec((1, D_EXPERT, D_MODEL), wsel),
        ],
        out_specs=pl.BlockSpec((ROW_BLK * PACK_ROWS, LANES), lambda s, *_: (s, 0)),
        scratch_shapes=[pltpu.VMEM((ROW_BLK * PACK_ROWS, LANES), U32)] * (GATHER_AHEAD + 1) + [
            pltpu.VMEM((D_MODEL, D_EXPERT), BF16),
            pltpu.VMEM((D_MODEL, D_EXPERT), BF16),
            pltpu.VMEM((D_EXPERT, D_MODEL), BF16),
            pltpu.SemaphoreType.DMA((GATHER_AHEAD + 1, len(SEG_SIZES))),
        ],
    )
    return pl.pallas_call(
        functools.partial(_experts_kernel, n_tiles=n_tiles, n_blocks=n_blocks),
        grid_spec=grid_spec,
        out_shape=jax.ShapeDtypeStruct((n_blocks * ROW_BLK * PACK_ROWS, LANES), U32),
        compiler_params=pltpu.CompilerParams(
            dimension_semantics=("arbitrary",), vmem_limit_bytes=VMEM_LIMIT),
        name="experts",
    )(blk_e, nblk, base, jlo, jhi, nvalid, big, npiece, cum, end, src, stage, w1, w3, w2)


def _combine_kernel(cnt_ref, off_ref, dst_ref, big_ref, npiece_ref, h_ref, route_ref, p_ref, gple_ref, wpg_ref,
                    bpg_ref, wpp_ref, ys_hbm, o_ref, *scratch, n_tiles):
    ybufs, sems = scratch[:-1], scratch[-1]
    i = pl.program_id(0)
    n_cls = len(SEG_SIZES)

    def segment_copies(tile, e, live, slot, classes=ALL_CLASSES):
        g = tile * N_EXPERTS + e
        off, dst = off_ref[g], dst_ref[g]
        _segment_pieces(jnp.where(live, cnt_ref[g], 0),
                        lambda cls, o: _piece_copy(ys_hbm, ybufs[slot], sems.at[slot], cls, dst + o, off + o).start(
                            priority=cls % DMA_PRIORITIES),
                        classes)

    def looped_copies(tile, n_experts, slot, classes=ALL_CLASSES):
        def body(e, carry):
            segment_copies(tile, e, True, slot, classes)
            return carry

        lax.fori_loop(0, n_experts, body, 0)

    @pl.when(i == 0)
    def _():
        for first in range(COMBINE_TILES):
            looped_copies(first, N_EXPERTS, first)

    def step(cur_slots, nxt_slots):
        tiles = [i * COMBINE_TILES + sub for sub in range(COMBINE_TILES)]
        for tile, slot in zip(tiles, cur_slots):
            _drain(ys_hbm, ybufs[slot], sems.at[slot], [npiece_ref[tile * n_cls + c] for c in range(n_cls)],
                   big_ref[tile] != 0)
        ahead = [(jnp.minimum(tile + COMBINE_TILES, n_tiles - 1), tile + COMBINE_TILES < n_tiles, slot)
                 for tile, slot in zip(tiles, nxt_slots)]
        for tile, live, slot in ahead:
            looped_copies(tile, jnp.where(live & (big_ref[tile] != 0), N_EXPERTS, 0), slot, BIG_CLASSES)
        groups = iter(_split([(a, e) for a in ahead for e in range(N_EXPERTS)], COMBINE_TILES * PACK_ROWS + 2))

        def start_group():
            for (tile, live, slot), e in next(groups):
                segment_copies(tile, e, live, slot, SMALL_CLASSES)

        start_group()
        pp = _dot(p_ref[...].astype(BF16), wpp_ref[...])
        place = lax.broadcasted_iota(jnp.int32, (TM, TILE_ROWS), 1).astype(F32)
        moe_rows = []
        for sub, slot in enumerate(cur_slots):
            route = route_ref[sub * TM:(sub + 1) * TM, :]
            sel = [(place == route[:, 4 + kk:5 + kk]).astype(BF16) for kk in range(TOP_K)]
            moe = []
            for blk in range(PACK_ROWS):
                start_group()
                cols = _unpack_block(ybufs[slot], TILE_ROWS, blk)
                moe.append(_dot(sel[0], cols) * route[:, 2:3] + _dot(sel[1], cols) * route[:, 3:4])
            moe_rows.append(jnp.concatenate(moe, axis=1))
        start_group()
        h = h_ref[...] + jnp.concatenate(moe_rows, axis=0)
        gate = _sigmoid(_dot(_rms(h, gple_ref[...]).astype(BF16), wpg_ref[...]) + bpg_ref[...])
        o_ref[...] = h + gate * pp

    for parity in range(2):
        @pl.when(i % 2 == parity)
        def _(parity=parity):
            half = [list(range(p * COMBINE_TILES, (p + 1) * COMBINE_TILES)) for p in range(2)]
            step(half[parity], half[1 - parity])


def _combine(cnt, off, dst, big, npiece, h1, route, p2, gple, wpg, bpg, wpp, ys):
    t = h1.shape[0]
    n_tiles = t // TM
    rows = COMBINE_TILES * TM
    const = lambda i, *_: (0, 0)
    row = lambda i, *_: (i, 0)
    grid_spec = pltpu.PrefetchScalarGridSpec(
        num_scalar_prefetch=5,
        grid=(n_tiles // COMBINE_TILES,),
        in_specs=[
            pl.BlockSpec((rows, D_MODEL), row),
            pl.BlockSpec((rows, LANES), row),
            pl.BlockSpec((rows, PLE_DIM), row),
            pl.BlockSpec((1, D_MODEL), const),
            pl.BlockSpec((D_MODEL, D_MODEL), const),
            pl.BlockSpec((1, D_MODEL), const),
            pl.BlockSpec((PLE_DIM, D_MODEL), const),
            pl.BlockSpec(memory_space=pl.ANY),
        ],
        out_specs=pl.BlockSpec((rows, D_MODEL), row),
        scratch_shapes=[pltpu.VMEM((TILE_ROWS * PACK_ROWS, LANES), U32)] * (2 * COMBINE_TILES) + [
            pltpu.SemaphoreType.DMA((2 * COMBINE_TILES, len(SEG_SIZES))),
        ],
    )
    return pl.pallas_call(
        functools.partial(_combine_kernel, n_tiles=n_tiles),
        grid_spec=grid_spec,
        out_shape=jax.ShapeDtypeStruct((t, D_MODEL), F32),
        compiler_params=pltpu.CompilerParams(
            dimension_semantics=("arbitrary",), vmem_limit_bytes=VMEM_LIMIT),
        name="combine",
    )(cnt, off, dst, big, npiece, h1, route, p2, gple, wpg, bpg, wpp, ys)


def _layer(h, p_i, g_mix, w_in, b_in, g_q, g_k, rel_bias, conv_w, conv_b, w_pa, w_pc, w_o,
           g_ffn, w_group, b_group, w_router, b_router, w1, w3, w2,
           g_ple, w_ple_gate, b_ple_gate, w_ple_proj):
    b, s, d = h.shape
    t = b * s
    x2 = h.reshape(t, d)
    row2 = lambda a: a.reshape(1, -1).astype(F32)

    qkv_w = 3 * ATTN_W
    conv_end = qkv_w + 3 * CONV_W
    w_in_b = w_in.astype(BF16)
    gq = row2(jnp.tile(g_q.astype(F32) * (HEAD_DIM ** -0.5 * LOG2E), N_HEADS))
    gk = row2(jnp.tile(g_k.astype(F32), N_HEADS))
    head = jnp.arange(ATTN_W) // HEAD_DIM
    hmat = jnp.where(head[:, None] == head[None, :], 1.0 / HEAD_DIM, 0.0).astype(BF16)
    cw = jnp.concatenate([conv_w.astype(F32), jnp.zeros((SUBLANES - CONV_K, CONV_W), F32)], axis=0)

    q, k, v, yc = _inproj(x2, row2(g_mix), w_in_b[:, :qkv_w], w_in_b[:, qkv_w:conv_end],
                          row2(b_in[:conv_end]), gq, gk, hmat, cw, row2(conv_b), s)

    ya = _attention(q.reshape(b, s, ATTN_W), k.reshape(b, s, ATTN_W), v.reshape(b, s, ATTN_W),
                    _attn_bias(rel_bias)).reshape(t, ATTN_W)

    n_pad = LANES - N_GROUPS - N_EXPERTS
    wrt = jnp.concatenate([w_router, w_group, jnp.zeros((d, n_pad), w_group.dtype)], axis=1).astype(BF16)
    brt = row2(jnp.concatenate([b_router, b_group, jnp.zeros((n_pad,), b_group.dtype)]))
    h1, stage, route, cnt_f = _merge(x2, ya, yc, row2(g_mix), w_in_b[:, conv_end:], row2(b_in[conv_end:]),
                                     w_pa.astype(BF16), w_pc.astype(BF16), w_o.astype(BF16),
                                     row2(g_ffn), wrt, brt)

    n_tiles = t // TM
    cnt = cnt_f[:, 0, :N_EXPERTS].astype(jnp.int32)
    tile_off = jnp.cumsum(cnt, axis=1) - cnt
    tot = cnt.sum(axis=0)
    pcounts = (tot + ROW_BLK - 1) // ROW_BLK * ROW_BLK
    pends = jnp.cumsum(pcounts)
    pstarts = pends - pcounts
    cum = jnp.cumsum(cnt, axis=0) - cnt
    dst = pstarts[None, :] + cum
    n_blocks = (t * TOP_K) // ROW_BLK + N_EXPERTS
    blk_start = jnp.arange(n_blocks, dtype=jnp.int32) * ROW_BLK
    blk_e = jnp.minimum((pends[None, :] <= blk_start[:, None]).sum(axis=1), N_EXPERTS - 1).astype(jnp.int32)
    nblk = (pends[-1:] // ROW_BLK).astype(jnp.int32)
    sel = (jnp.arange(N_EXPERTS, dtype=jnp.int32)[:, None] == blk_e[None, :]).astype(jnp.int32)
    of_block = lambda a: (a[..., None] * sel).sum(axis=-2)
    base = blk_start - of_block(pstarts)
    nvalid = jnp.clip(of_block(tot) - base, 0, ROW_BLK)
    cum_e = of_block(cum)
    cnt_e = of_block(cnt)
    jlo = (cum_e + cnt_e <= base[None, :]).sum(axis=0)
    jhi = (cum_e < base[None, :] + ROW_BLK).sum(axis=0)
    part = jnp.clip(jnp.minimum(cum_e + cnt_e, base[None, :] + ROW_BLK) - jnp.maximum(cum_e, base[None, :]),
                    0, ROW_BLK)
    blk_pieces = jnp.stack(_piece_counts(part), axis=-1).sum(axis=0)
    tile_pieces = jnp.stack(_piece_counts(cnt), axis=-1).sum(axis=1)
    src = jnp.arange(n_tiles, dtype=jnp.int32)[:, None] * TILE_ROWS + tile_off - cum
    flat = lambda a: a.reshape(-1).astype(jnp.int32)

    blk_big = (part >= BIG_PIECE).any(axis=0)
    tile_big = (cnt >= BIG_PIECE).any(axis=1)

    ys = _experts(blk_e, nblk, flat(base), flat(jlo), flat(jhi), flat(nvalid), flat(blk_big), flat(blk_pieces),
                  flat(cum), flat(cum + cnt), flat(src), stage, w1, w3, w2)
    out = _combine(flat(cnt), flat(tile_off), flat(dst), flat(tile_big), flat(tile_pieces), h1, route,
                   p_i.reshape(t, PLE_DIM), row2(g_ple), w_ple_gate.astype(BF16), row2(b_ple_gate),
                   w_ple_proj.astype(BF16), ys)
    return out.reshape(b, s, d)


def kernel(x, p, g_mix, w_in, b_in, g_q, g_k, rel_bias, conv_w, conv_b, w_pa, w_pc, w_o, g_ffn, w_group, b_group, w_router, b_router, w1, w3, w2, g_ple, w_ple_gate, b_ple_gate, w_ple_proj):
    h = x
    for i in range(p.shape[0]):
        h = _layer(h, p[i], g_mix[i], w_in[i], b_in[i], g_q[i], g_k[i], rel_bias[i], conv_w[i], conv_b[i],
                   w_pa[i], w_pc[i], w_o[i], g_ffn[i], w_group[i], b_group[i], w_router[i], b_router[i],
                   w1[i], w3[i], w2[i], g_ple[i], w_ple_gate[i], b_ple_gate[i], w_ple_proj[i])
    return h
```

```python
import functools

import jax
import jax.numpy as jnp
from jax import lax
from jax.experimental import pallas as pl
from jax.experimental.pallas import tpu as pltpu

D_MODEL = 1024
CHUNK = 64
LEFT_CHUNKS = 8
N_HEADS = 8
HEAD_DIM = 64
ATTN_W = N_HEADS * HEAD_DIM
CONV_W = D_MODEL // 2
CONV_K = 3
MAX_REL_PAST = 256
PLE_DIM = 256
N_GROUPS = 4
EXPERTS_PER_GROUP = 8
N_EXPERTS = N_GROUPS * EXPERTS_PER_GROUP
TOP_K = 2
D_EXPERT = 512
EPS = 1e-6
NEG = -1e30
LOG2E = 1.4426950408889634

LANES = 128
SUBLANES = 8
TM = 256
TI = 1024
MERGE_TILES = 4
COMBINE_TILES = 2
TQ = 256
KV_SLABS = 1 + (LEFT_CHUNKS * CHUNK) // TQ
ROW_BLK = 512
TILE_ROWS = TOP_K * TM
PACK_ROWS = D_MODEL // (2 * LANES)
PACK_W = 2 * LANES
SEG_SIZES = tuple(TM >> k for k in range(TM.bit_length()))
BIG_PIECE = 32
ATTN_BATCH = 2
SCORE_AHEAD = 2
GATHER_AHEAD = 2
GATHER_UNROLL = 36
VMEM_LIMIT = 56 * 1024 * 1024

F32 = jnp.float32
BF16 = jnp.bfloat16
U32 = jnp.uint32


def _dot(a, b):
    return jnp.dot(a, b, preferred_element_type=F32)


def _rms(x, g):
    ms = jnp.mean(x * x, axis=-1, keepdims=True)
    return (x * lax.rsqrt(ms + EPS)) * g


def _sigmoid(x):
    return 1.0 / (1.0 + jnp.exp(-x))


def _pack_block(ref, vals, n_rows, a, is_bf16=False, first_row=0):
    lo = vals[:, 0:LANES]
    hi = vals[:, LANES:PACK_W]
    if not is_bf16:
        lo = lo.astype(BF16).astype(F32)
        hi = hi.astype(BF16).astype(F32)
    ref[pl.ds(first_row * PACK_ROWS + a, n_rows, stride=PACK_ROWS), :] = (
        lax.bitcast_convert_type(hi, U32) | (lax.bitcast_convert_type(lo, U32) >> 16))


def _pack_rows(ref, vals, n_rows, is_bf16=False, first_row=0):
    for a in range(PACK_ROWS):
        _pack_block(ref, vals[:, a * PACK_W:(a + 1) * PACK_W], n_rows, a, is_bf16, first_row)


def _unpack_block(ref, n_rows, a, n_valid=None):
    word = ref[pl.ds(a, n_rows, stride=PACK_ROWS), :]
    if n_valid is not None:
        word = jnp.where(lax.broadcasted_iota(jnp.int32, (n_rows, LANES), 0) < n_valid, word, U32(0))
    lo = lax.bitcast_convert_type(word << 16, F32).astype(BF16)
    hi = lax.bitcast_convert_type(word & U32(0xFFFF0000), F32).astype(BF16)
    return jnp.concatenate([lo, hi], axis=1)


def _split(items, n_groups):
    items = list(items)
    return [items[len(items) * g // n_groups:len(items) * (g + 1) // n_groups] for g in range(n_groups)]


def _inproj_kernel(x_ref, g_ref, wqkv_ref, wconv_ref, b_ref, gq_ref, gk_ref, hm_ref,
                   cw_ref, cb_ref, q_ref, k_ref, v_ref, yc_ref, carry_ref, *, tiles_per_seq):
    i = pl.program_id(0)
    nb = _rms(x_ref[...], g_ref[...]).astype(BF16)

    zq = _dot(nb, wqkv_ref[...]) + b_ref[:, 0:3 * ATTN_W]
    hm = hm_ref[...]

    def head_rms(t, g):
        ms = _dot((t * t).astype(BF16), hm)
        return (t * lax.rsqrt(ms + EPS)) * g

    q_ref[...] = head_rms(zq[:, 0:ATTN_W], gq_ref[...]).astype(BF16)
    k_ref[...] = head_rms(zq[:, ATTN_W:2 * ATTN_W], gk_ref[...]).astype(BF16)
    v_ref[...] = zq[:, 2 * ATTN_W:3 * ATTN_W].astype(BF16)

    zc = _dot(nb, wconv_ref[...]) + b_ref[:, 3 * ATTN_W:3 * ATTN_W + 3 * CONV_W]
    u = zc[:, 0:CONV_W]
    bg = zc[:, CONV_W:2 * CONV_W]
    cg = zc[:, 2 * CONV_W:3 * CONV_W]
    cu = cg * u

    @pl.when((i % tiles_per_seq) == 0)
    def _():
        carry_ref[...] = jnp.zeros_like(carry_ref)

    prev = carry_ref[...]
    carry_ref[...] = cu[TI - SUBLANES:TI, :]
    row = lax.broadcasted_iota(jnp.int32, (SUBLANES, CONV_W), 0)

    def shifted(s):
        r = pltpu.roll(cu, s, 0)
        p = pltpu.roll(prev, s, 0)
        top = jnp.where(row < s, p, r[0:SUBLANES, :])
        return jnp.concatenate([top, r[SUBLANES:, :]], axis=0)

    y = cb_ref[...] + cw_ref[0:1, :] * shifted(2)
    y = y + cw_ref[1:2, :] * shifted(1)
    y = y + cw_ref[2:3, :] * cu
    yc_ref[...] = (bg * y).astype(BF16)


def _inproj(x2, g_mix, wqkv, wconv, b_in, gq, gk, hmat, cw, cb, seq):
    t = x2.shape[0]
    const = lambda i: (0, 0)
    row = lambda i: (i, 0)
    out = jax.ShapeDtypeStruct((t, ATTN_W), BF16)
    return pl.pallas_call(
        functools.partial(_inproj_kernel, tiles_per_seq=seq // TI),
        grid=(t // TI,),
        in_specs=[
            pl.BlockSpec((TI, D_MODEL), row),
            pl.BlockSpec((1, D_MODEL), const),
            pl.BlockSpec((D_MODEL, 3 * ATTN_W), const),
            pl.BlockSpec((D_MODEL, 3 * CONV_W), const),
            pl.BlockSpec((1, 3 * ATTN_W + 3 * CONV_W), const),
            pl.BlockSpec((1, ATTN_W), const),
            pl.BlockSpec((1, ATTN_W), const),
            pl.BlockSpec((ATTN_W, ATTN_W), const),
            pl.BlockSpec((SUBLANES, CONV_W), const),
            pl.BlockSpec((1, CONV_W), const),
        ],
        out_specs=[pl.BlockSpec((TI, ATTN_W), row)] * 4,
        out_shape=[out] * 4,
        scratch_shapes=[pltpu.VMEM((SUBLANES, CONV_W), F32)],
        compiler_params=pltpu.CompilerParams(
            dimension_semantics=("arbitrary",), vmem_limit_bytes=VMEM_LIMIT,
            allow_input_fusion=[False, False, True, True] + [False] * 6),
        name="inproj",
    )(x2, g_mix, wqkv, wconv, b_in, gq, gk, hmat, cw, cb)


def _lane_fold(parts, op):
    acc = None
    for a in parts:
        for c in range(0, a.shape[1], LANES):
            piece = a[:, c:c + LANES]
            acc = piece if acc is None else op(acc, piece)
    return acc


def _attn_kernel(q_ref, k0_ref, k1_ref, k2_ref, v0_ref, v1_ref, v2_ref, bias_ref, o_ref):
    k_refs = (k0_ref, k1_ref, k2_ref)
    v_refs = (v0_ref, v1_ref, v2_ref)
    pair_w = 2 * HEAD_DIM
    lane = lax.broadcasted_iota(jnp.int32, (TQ, pair_w), 1)
    low = lane < HEAD_DIM

    def scores(bb, h, pens):
        ps = slice((h // 2) * pair_w, (h // 2 + 1) * pair_w)
        q_pair = q_ref[bb, :, ps]
        own = low if h % 2 == 0 else jnp.logical_not(low)
        qh = jnp.where(own, q_pair, jnp.zeros_like(q_pair))
        s = [lax.dot_general(qh, k_refs[j][bb, :, ps], (((1,), (1,)), ((), ())),
                             preferred_element_type=F32) + bias_ref[h, :, j * TQ:(j + 1) * TQ]
             for j in range(KV_SLABS)]
        return s if pens is None else [sj + pens[j] for j, sj in enumerate(s)]

    def weighted(bb, h, s):
        ps = slice((h // 2) * pair_w, (h // 2 + 1) * pair_w)
        m = _lane_fold(s, jnp.maximum).max(axis=-1, keepdims=True)
        e = [jnp.exp2(sj - m) for sj in s]
        l = _lane_fold(e, jnp.add).sum(axis=-1, keepdims=True)
        acc = None
        for j in range(KV_SLABS):
            oj = _dot(e[j].astype(BF16), v_refs[j][bb, :, ps])
            acc = oj if acc is None else acc + oj
        return acc * (1.0 / l)

    def all_heads(pens):
        items = [(bb, h) for h in range(N_HEADS) for bb in range(ATTN_BATCH)]
        pending = [scores(bb, h, pens) for bb, h in items[:SCORE_AHEAD]]
        o_even = {}
        for n, (bb, h) in enumerate(items):
            if n + SCORE_AHEAD < len(items):
                pending.append(scores(*items[n + SCORE_AHEAD], pens))
            o = weighted(bb, h, pending.pop(0))
            if h % 2 == 0:
                o_even[bb] = o
            else:
                ps = slice((h // 2) * pair_w, (h // 2 + 1) * pair_w)
                o_ref[bb, :, ps] = jnp.where(low, o_even[bb], o).astype(BF16)

    i = pl.program_id(1)

    @pl.when(i >= KV_SLABS - 1)
    def _():
        all_heads(None)

    @pl.when(i < KV_SLABS - 1)
    def _():
        all_heads([jnp.where(i >= KV_SLABS - 1 - j, 0.0, NEG).astype(F32) for j in range(KV_SLABS)])


def _attention(q, k, v, bias):
    b, s, _ = q.shape
    blk = (ATTN_BATCH, TQ, ATTN_W)

    def kv_map(j):
        back = KV_SLABS - 1 - j
        return lambda bi, i: (bi, jnp.maximum(i - back, 0), 0)

    kv_specs = [pl.BlockSpec(blk, kv_map(j)) for j in range(KV_SLABS)]
    return pl.pallas_call(
        _attn_kernel,
        grid=(b // ATTN_BATCH, s // TQ),
        in_specs=[pl.BlockSpec(blk, lambda bi, i: (bi, i, 0))] + kv_specs + kv_specs + [
            pl.BlockSpec((N_HEADS, TQ, KV_SLABS * TQ), lambda bi, i: (0, 0, 0))],
        out_specs=pl.BlockSpec(blk, lambda bi, i: (bi, i, 0)),
        out_shape=jax.ShapeDtypeStruct((b, s, ATTN_W), BF16),
        compiler_params=pltpu.CompilerParams(
            dimension_semantics=("arbitrary", "arbitrary"), vmem_limit_bytes=VMEM_LIMIT),
        name="attn",
    )(q, k, k, k, v, v, v, bias)


def _attn_bias(rel_bias):
    nk = KV_SLABS * TQ
    past = nk - TQ
    d = jnp.arange(TQ - 1 + past, -TQ, -1)
    idx = jnp.clip(d, -(CHUNK - 1), MAX_REL_PAST) + (CHUNK - 1)
    onehot = (idx[:, None] == jnp.arange(rel_bias.shape[1])[None, :]).astype(F32)
    per_dist = jnp.einsum("dn,hn->hd", onehot, rel_bias.astype(F32) * LOG2E,
                          precision=lax.Precision.HIGHEST)
    n_h, span = per_dist.shape
    padded = jnp.pad(per_dist, ((0, 0), (0, 2)))
    skew = jnp.tile(padded, (1, TQ))[:, :TQ * (span + 1)].reshape(n_h, TQ, span + 1)
    table = skew[:, :, TQ - 1:TQ - 1 + nk]
    r = jnp.arange(TQ)[:, None]
    c = jnp.arange(nk)[None, :]
    qc = r // CHUNK
    kc = c // CHUNK
    lead = past // CHUNK - LEFT_CHUNKS
    band = (kc >= qc + lead) & (kc <= qc + lead + LEFT_CHUNKS)
    return jnp.where(band[None], table, NEG)


def _merge_kernel(x_ref, ya_ref, yc_ref, g_ref, wg_ref, bgate_ref, wpa_ref, wpc_ref, wo_ref,
                  gffn_ref, wrt_ref, brt_ref, h_ref, stage_ref, route_ref, cnt_ref, n2_scr, logit_scr):
    @pl.when(pl.program_id(0) == 0)
    def _():
        n2_scr[...] = jnp.zeros_like(n2_scr)
        logit_scr[...] = jnp.zeros_like(logit_scr)

    row8 = lax.broadcasted_iota(jnp.int32, (SUBLANES, TM), 0).astype(F32)
    ninf = -jnp.inf

    def argmax_first(vals):
        mx = vals.max(axis=0, keepdims=True)
        idx = jnp.where(vals == mx, row8, float(SUBLANES)).min(axis=0, keepdims=True)
        return mx, idx

    def route_tile(sub):
        rows = slice(sub * TM, (sub + 1) * TM)
        lt = logit_scr[rows, :].T
        gl = jnp.where(row8 < N_GROUPS, lt[N_EXPERTS:N_EXPERTS + SUBLANES, :], ninf)
        gmax, grp = argmax_first(gl)
        p_grp = 1.0 / jnp.exp(gl - gmax).sum(axis=0, keepdims=True)
        el = lt[0:EXPERTS_PER_GROUP, :]
        for g in range(1, N_GROUPS):
            el = jnp.where(grp == g, lt[g * EXPERTS_PER_GROUP:(g + 1) * EXPERTS_PER_GROUP, :], el)
        l1, i1 = argmax_first(el)
        l2, i2 = argmax_first(jnp.where(row8 == i1, ninf, el))
        e2 = jnp.exp(l2 - l1)
        den = 1.0 + e2
        w1 = p_grp * (1.0 / den)
        w2 = p_grp * (e2 / den)
        x1 = grp * EXPERTS_PER_GROUP + i1
        x2 = grp * EXPERTS_PER_GROUP + i2

        row_e = lax.broadcasted_iota(jnp.int32, (N_EXPERTS, TM), 0).astype(F32)
        oh1 = (row_e == x1).astype(F32)
        oh2 = (row_e == x2).astype(F32)
        oh = (oh1 + oh2).astype(BF16)
        r = lax.broadcasted_iota(jnp.int32, (TM, TM), 0)
        c = lax.broadcasted_iota(jnp.int32, (TM, TM), 1)
        earlier_tok = _dot(oh, (r < c).astype(BF16))
        er = lax.broadcasted_iota(jnp.int32, (N_EXPERTS, N_EXPERTS), 0)
        ec = lax.broadcasted_iota(jnp.int32, (N_EXPERTS, N_EXPERTS), 1)
        lower_exp = _dot((ec < er).astype(BF16), oh).sum(axis=1, keepdims=True)
        where = earlier_tok + lower_exp
        pos1 = (oh1 * where).sum(axis=0, keepdims=True)
        pos2 = (oh2 * where).sum(axis=0, keepdims=True)
        counts = lax.dot_general(jnp.ones((SUBLANES, TM), BF16), oh, (((1,), (1,)), ((), ())),
                                 preferred_element_type=F32)
        cnt_ref[sub] = jnp.concatenate([counts, jnp.zeros((SUBLANES, LANES - N_EXPERTS), F32)], axis=1)

        route_t = jnp.zeros((SUBLANES, TM), F32)
        for j, val in enumerate((x1, x2, w1, w2, pos1, pos2)):
            route_t = jnp.where(row8 == j, val, route_t)
        route_ref[rows, :] = jnp.concatenate([route_t, jnp.zeros((LANES - SUBLANES, TM), F32)], axis=0).T
        return pos1, pos2

    n2_old = [n2_scr[sub * TM:(sub + 1) * TM, :] for sub in range(MERGE_TILES)]
    positions = [route_tile(sub) for sub in range(MERGE_TILES)]

    x = x_ref[...]
    nb = _rms(x, g_ref[...]).astype(BF16)
    sga = _sigmoid(_dot(nb, wg_ref[:, 0:D_MODEL]) + bgate_ref[:, 0:D_MODEL])
    ma = sga * _dot(ya_ref[...], wpa_ref[...])
    sgc = _sigmoid(_dot(nb, wg_ref[:, D_MODEL:2 * D_MODEL]) + bgate_ref[:, D_MODEL:2 * D_MODEL])
    m = ma + sgc * _dot(yc_ref[...], wpc_ref[...])
    h = x + _dot(m.astype(BF16), wo_ref[...])
    h_ref[...] = h
    n2_new = _rms(h, gffn_ref[...]).astype(BF16)
    n2_scr[...] = n2_new
    logit_scr[...] = _dot(n2_new, wrt_ref[...]) + brt_ref[...]

    slot = lax.broadcasted_iota(jnp.int32, (TILE_ROWS, TM), 0).astype(F32)
    for sub, (pos1, pos2) in enumerate(positions):
        place = ((slot == pos1) | (slot == pos2)).astype(BF16)
        _pack_rows(stage_ref, _dot(place, n2_old[sub]), TILE_ROWS, is_bf16=True, first_row=sub * TILE_ROWS)


def _merge(x2, ya, yc, g_mix, wgate, bgate, wpa, wpc, wo, gffn, wrt, brt):
    t = x2.shape[0]
    n_tiles = t // TM
    n_steps = n_tiles // MERGE_TILES
    tmm = MERGE_TILES * TM
    const = lambda i: (0, 0)
    row = lambda i: (jnp.minimum(i, n_steps - 1), 0)
    late = lambda i: (jnp.maximum(i - 1, 0), 0)
    return pl.pallas_call(
        _merge_kernel,
        grid=(n_steps + 1,),
        in_specs=[
            pl.BlockSpec((tmm, D_MODEL), row),
            pl.BlockSpec((tmm, ATTN_W), row),
            pl.BlockSpec((tmm, CONV_W), row),
            pl.BlockSpec((1, D_MODEL), const),
            pl.BlockSpec((D_MODEL, 2 * D_MODEL), const),
            pl.BlockSpec((1, 2 * D_MODEL), const),
            pl.BlockSpec((ATTN_W, D_MODEL), const),
            pl.BlockSpec((CONV_W, D_MODEL), const),
            pl.BlockSpec((D_MODEL, D_MODEL), const),
            pl.BlockSpec((1, D_MODEL), const),
            pl.BlockSpec((D_MODEL, LANES), const),
            pl.BlockSpec((1, LANES), const),
        ],
        out_specs=[
            pl.BlockSpec((tmm, D_MODEL), row),
            pl.BlockSpec((MERGE_TILES * TILE_ROWS * PACK_ROWS, LANES), late),
            pl.BlockSpec((tmm, LANES), late),
            pl.BlockSpec((MERGE_TILES, SUBLANES, LANES), lambda i: (jnp.maximum(i - 1, 0), 0, 0)),
        ],
        out_shape=[
            jax.ShapeDtypeStruct((t, D_MODEL), F32),
            jax.ShapeDtypeStruct((n_tiles * TILE_ROWS * PACK_ROWS, LANES), U32),
            jax.ShapeDtypeStruct((t, LANES), F32),
            jax.ShapeDtypeStruct((n_tiles, SUBLANES, LANES), F32),
        ],
        scratch_shapes=[pltpu.VMEM((tmm, D_MODEL), BF16), pltpu.VMEM((tmm, LANES), F32)],
        compiler_params=pltpu.CompilerParams(
            dimension_semantics=("arbitrary",), vmem_limit_bytes=VMEM_LIMIT,
            allow_input_fusion=[False, False, False, False, True, False, True, True, True, False, True, False]),
        name="merge",
    )(x2, ya, yc, g_mix, wgate, bgate, wpa, wpc, wo, gffn, wrt, brt)


def _piece_counts(n):
    return [(n >> (size.bit_length() - 1)) & 1 for size in SEG_SIZES]


ALL_CLASSES = tuple(range(len(SEG_SIZES)))
BIG_CLASSES = tuple(c for c in ALL_CLASSES if SEG_SIZES[c] >= BIG_PIECE)
SMALL_CLASSES = tuple(c for c in ALL_CLASSES if SEG_SIZES[c] < BIG_PIECE)


def _segment_pieces(n, visit, classes=ALL_CLASSES):
    for cls in classes:
        size = SEG_SIZES[cls]

        @pl.when((n & size) != 0)
        def _(cls=cls, size=size):
            visit(cls, n & ~(2 * size - 1))


def _piece_copy(src_ref, dst_ref, sems, cls, src_row, dst_row):
    n = SEG_SIZES[cls] * PACK_ROWS
    return pltpu.make_async_copy(src_ref.at[pl.ds(src_row * PACK_ROWS, n), :],
                                 dst_ref.at[pl.ds(dst_row * PACK_ROWS, n), :], sems.at[cls])


def _drain(src_ref, dst_ref, sems, counts, has_big):
    unroll = 4

    def wait_classes(classes):
        for cls in classes:
            def wait_some(k, cls=cls):
                def body(t, carry):
                    for _ in range(k):
                        _piece_copy(src_ref, dst_ref, sems, cls, 0, 0).wait()
                    return carry
                return body

            n = counts[cls]
            lax.fori_loop(0, n >> (unroll.bit_length() - 1), wait_some(unroll), 0)
            lax.fori_loop(0, n & (unroll - 1), wait_some(1), 0)

    @pl.when(has_big)
    def _():
        wait_classes(BIG_CLASSES)

    wait_classes(SMALL_CLASSES)


def _experts_kernel(be_ref, nb_ref, base_ref, jlo_ref, jhi_ref, nv_ref, big_ref, npiece_ref, cum_ref, end_ref,
                    src_ref, stage_hbm, w1_ref, w3_ref, w2_ref, ys_ref, *scratch, n_tiles, n_blocks):
    xbufs = scratch[:GATHER_AHEAD + 1]
    w1b_ref, w3b_ref, w2b_ref, sems = scratch[GATHER_AHEAD + 1:]
    s = pl.program_id(0)
    nb = nb_ref[0]
    n_cls = len(SEG_SIZES)

    def segment_copies(step, j, live, buf, sem, classes=ALL_CLASSES):
        base = base_ref[step]
        g = jnp.minimum(j, n_tiles - 1) * N_EXPERTS + be_ref[step]
        lo = jnp.maximum(cum_ref[g], base)
        hi = jnp.minimum(end_ref[g], base + ROW_BLK)
        src = src_ref[g] + lo
        dst = lo - base
        _segment_pieces(jnp.where(live, jnp.maximum(hi - lo, 0), 0),
                        lambda cls, o: _piece_copy(stage_hbm, buf, sem, cls, src + o, dst + o).start(), classes)

    def looped_copies(step, j0, j1, buf, sem, classes=ALL_CLASSES):
        def body(j, carry):
            segment_copies(step, j, True, buf, sem, classes)
            return carry

        lax.fori_loop(j0, j1, body, 0)

    @pl.when(s == 0)
    def _():
        for buf in xbufs:
            buf[...] = jnp.zeros_like(buf)
        for first in range(GATHER_AHEAD):
            blk = min(first, n_blocks - 1)
            looped_copies(blk, jlo_ref[blk], jnp.where(first < nb, jhi_ref[blk], jlo_ref[blk]),
                          xbufs[first], sems.at[first])

    def step(cur, cur_sem, nxt, nxt_sem):
        prev = be_ref[jnp.maximum(s - 1, 0)]

        @pl.when((s == 0) | (be_ref[s] != prev))
        def _():
            w1b_ref[...] = w1_ref[0].astype(BF16)
            w3b_ref[...] = w3_ref[0].astype(BF16)
            w2b_ref[...] = w2_ref[0].astype(BF16)

        _drain(stage_hbm, cur, cur_sem, [npiece_ref[s * n_cls + c] for c in range(n_cls)], big_ref[s] != 0)

        nxt_step = jnp.minimum(s + GATHER_AHEAD, n_blocks - 1)
        live = s + GATHER_AHEAD < nb
        j0 = jlo_ref[nxt_step]
        j1 = jnp.where(live, jhi_ref[nxt_step], j0)
        looped_copies(nxt_step, j0 + GATHER_UNROLL, j1, nxt, nxt_sem, SMALL_CLASSES)
        looped_copies(nxt_step, j0, jnp.where(big_ref[nxt_step] != 0, j1, j0), nxt, nxt_sem, BIG_CLASSES)
        groups = iter(_split(range(GATHER_UNROLL), 2 * PACK_ROWS))

        def start_group():
            for k in next(groups):
                segment_copies(nxt_step, j0 + k, j0 + k < j1, nxt, nxt_sem, SMALL_CLASSES)

        a = None
        g = None
        for blk in range(PACK_ROWS):
            start_group()
            xa = _unpack_block(cur, ROW_BLK, blk, n_valid=nv_ref[s])
            rows = slice(blk * PACK_W, (blk + 1) * PACK_W)
            da = _dot(xa, w1b_ref[rows, :])
            dg = _dot(xa, w3b_ref[rows, :])
            a = da if a is None else a + da
            g = dg if g is None else g + dg
        hdn = ((a * _sigmoid(a)) * g).astype(BF16)
        for blk in range(PACK_ROWS):
            start_group()
            _pack_block(ys_ref, _dot(hdn, w2b_ref[:, blk * PACK_W:(blk + 1) * PACK_W]), ROW_BLK, blk)

    n_buf = len(xbufs)
    for slot in range(n_buf):
        @pl.when((s < nb) & (s % n_buf == slot))
        def _(slot=slot):
            ahead = (slot + GATHER_AHEAD) % n_buf
            step(xbufs[slot], sems.at[slot], xbufs[ahead], sems.at[ahead])

    @pl.when(s >= nb)
    def _():
        ys_ref[...] = jnp.zeros_like(ys_ref)


def _experts(blk_e, nblk, base, jlo, jhi, nvalid, big, npiece, cum, end, src, stage, w1, w3, w2):
    n_blocks = blk_e.shape[0]
    n_tiles = cum.shape[0] // N_EXPERTS

    def wsel(s, be, nb, *_):
        return (be[jnp.minimum(s, nb[0] - 1)], 0, 0)

    grid_spec = pltpu.PrefetchScalarGridSpec(
        num_scalar_prefetch=11,
        grid=(n_blocks,),
        in_specs=[
            pl.BlockSpec(memory_space=pl.ANY),
            pl.BlockSpec((1, D_MODEL, D_EXPERT), wsel),
            pl.BlockSpec((1, D_MODEL, D_EXPERT), wsel),
            pl.BlockSpec((1, D_EXPERT, D_MODEL), wsel),
        ],
        out_specs=pl.BlockSpec((ROW_BLK * PACK_ROWS, LANES), lambda s, *_: (s, 0)),
        scratch_shapes=[pltpu.VMEM((ROW_BLK * PACK_ROWS, LANES), U32)] * (GATHER_AHEAD + 1) + [
            pltpu.VMEM((D_MODEL, D_EXPERT), BF16),
            pltpu.VMEM((D_MODEL, D_EXPERT), BF16),
            pltpu.VMEM((D_EXPERT, D_MODEL), BF16),
            pltpu.SemaphoreType.DMA((GATHER_AHEAD + 1, len(SEG_SIZES))),
        ],
    )
    return pl.pallas_call(
        functools.partial(_experts_kernel, n_tiles=n_tiles, n_blocks=n_blocks),
        grid_spec=grid_spec,
        out_shape=jax.ShapeDtypeStruct((n_blocks * ROW_BLK * PACK_ROWS, LANES), U32),
        compiler_params=pltpu.CompilerParams(
            dimension_semantics=("arbitrary",), vmem_limit_bytes=VMEM_LIMIT),
        name="experts",
    )(blk_e, nblk, base, jlo, jhi, nvalid, big, npiece, cum, end, src, stage, w1, w3, w2)


def _combine_kernel(cnt_ref, off_ref, dst_ref, big_ref, npiece_ref, h_ref, route_ref, p_ref, gple_ref, wpg_ref,
                    bpg_ref, wpp_ref, ys_hbm, o_ref, *scratch, n_tiles):
    ybufs, sems = scratch[:-1], scratch[-1]
    i = pl.program_id(0)
    n_cls = len(SEG_SIZES)

    def segment_copies(tile, e, live, slot, classes=ALL_CLASSES):
        g = tile * N_EXPERTS + e
        off, dst = off_ref[g], dst_ref[g]
        _segment_pieces(jnp.where(live, cnt_ref[g], 0),
                        lambda cls, o: _piece_copy(ys_hbm, ybufs[slot], sems.at[slot], cls, dst + o, off + o).start(),
                        classes)

    def looped_copies(tile, n_experts, slot, classes=ALL_CLASSES):
        def body(e, carry):
            segment_copies(tile, e, True, slot, classes)
            return carry

        lax.fori_loop(0, n_experts, body, 0)

    @pl.when(i == 0)
    def _():
        for first in range(COMBINE_TILES):
            looped_copies(first, N_EXPERTS, first)

    def step(cur_slots, nxt_slots):
        tiles = [i * COMBINE_TILES + sub for sub in range(COMBINE_TILES)]
        for tile, slot in zip(tiles, cur_slots):
            _drain(ys_hbm, ybufs[slot], sems.at[slot], [npiece_ref[tile * n_cls + c] for c in range(n_cls)],
                   big_ref[tile] != 0)
        ahead = [(jnp.minimum(tile + COMBINE_TILES, n_tiles - 1), tile + COMBINE_TILES < n_tiles, slot)
                 for tile, slot in zip(tiles, nxt_slots)]
        for tile, live, slot in ahead:
            looped_copies(tile, jnp.where(live & (big_ref[tile] != 0), N_EXPERTS, 0), slot, BIG_CLASSES)
        groups = iter(_split([(a, e) for a in ahead for e in range(N_EXPERTS)], COMBINE_TILES * PACK_ROWS + 2))

        def start_group():
            for (tile, live, slot), e in next(groups):
                segment_copies(tile, e, live, slot, SMALL_CLASSES)

        start_group()
        pp = _dot(p_ref[...].astype(BF16), wpp_ref[...])
        place = lax.broadcasted_iota(jnp.int32, (TM, TILE_ROWS), 1).astype(F32)
        moe_rows = []
        for sub, slot in enumerate(cur_slots):
            route = route_ref[sub * TM:(sub + 1) * TM, :]
            sel = [(place == route[:, 4 + kk:5 + kk]).astype(BF16) for kk in range(TOP_K)]
            moe = []
            for blk in range(PACK_ROWS):
                start_group()
                cols = _unpack_block(ybufs[slot], TILE_ROWS, blk)
                moe.append(_dot(sel[0], cols) * route[:, 2:3] + _dot(sel[1], cols) * route[:, 3:4])
            moe_rows.append(jnp.concatenate(moe, axis=1))
        start_group()
        h = h_ref[...] + jnp.concatenate(moe_rows, axis=0)
        gate = _sigmoid(_dot(_rms(h, gple_ref[...]).astype(BF16), wpg_ref[...]) + bpg_ref[...])
        o_ref[...] = h + gate * pp

    for parity in range(2):
        @pl.when(i % 2 == parity)
        def _(parity=parity):
            half = [list(range(p * COMBINE_TILES, (p + 1) * COMBINE_TILES)) for p in range(2)]
            step(half[parity], half[1 - parity])


def _combine(cnt, off, dst, big, npiece, h1, route, p2, gple, wpg, bpg, wpp, ys):
    t = h1.shape[0]
    n_tiles = t // TM
    rows = COMBINE_TILES * TM
    const = lambda i, *_: (0, 0)
    row = lambda i, *_: (i, 0)
    grid_spec = pltpu.PrefetchScalarGridSpec(
        num_scalar_prefetch=5,
        grid=(n_tiles // COMBINE_TILES,),
        in_specs=[
            pl.BlockSpec((rows, D_MODEL), row),
            pl.BlockSpec((rows, LANES), row),
            pl.BlockSpec((rows, PLE_DIM), row),
            pl.BlockSpec((1, D_MODEL), const),
            pl.BlockSpec((D_MODEL, D_MODEL), const),
            pl.BlockSpec((1, D_MODEL), const),
            pl.BlockSpec((PLE_DIM, D_MODEL), const),
            pl.BlockSpec(memory_space=pl.ANY),
        ],
        out_specs=pl.BlockSpec((rows, D_MODEL), row),
        scratch_shapes=[pltpu.VMEM((TILE_ROWS * PACK_ROWS, LANES), U32)] * (2 * COMBINE_TILES) + [
            pltpu.SemaphoreType.DMA((2 * COMBINE_TILES, len(SEG_SIZES))),
        ],
    )
    return pl.pallas_call(
        functools.partial(_combine_kernel, n_tiles=n_tiles),
        grid_spec=grid_spec,
        out_shape=jax.ShapeDtypeStruct((t, D_MODEL), F32),
        compiler_params=pltpu.CompilerParams(
            dimension_semantics=("arbitrary",), vmem_limit_bytes=VMEM_LIMIT),
        name="combine",
    )(cnt, off, dst, big, npiece, h1, route, p2, gple, wpg, bpg, wpp, ys)


def _layer(h, p_i, g_mix, w_in, b_in, g_q, g_k, rel_bias, conv_w, conv_b, w_pa, w_pc, w_o,
           g_ffn, w_group, b_group, w_router, b_router, w1, w3, w2,
           g_ple, w_ple_gate, b_ple_gate, w_ple_proj):
    b, s, d = h.shape
    t = b * s
    x2 = h.reshape(t, d)
    row2 = lambda a: a.reshape(1, -1).astype(F32)

    qkv_w = 3 * ATTN_W
    conv_end = qkv_w + 3 * CONV_W
    w_in_b = w_in.astype(BF16)
    gq = row2(jnp.tile(g_q.astype(F32) * (HEAD_DIM ** -0.5 * LOG2E), N_HEADS))
    gk = row2(jnp.tile(g_k.astype(F32), N_HEADS))
    head = jnp.arange(ATTN_W) // HEAD_DIM
    hmat = jnp.where(head[:, None] == head[None, :], 1.0 / HEAD_DIM, 0.0).astype(BF16)
    cw = jnp.concatenate([conv_w.astype(F32), jnp.zeros((SUBLANES - CONV_K, CONV_W), F32)], axis=0)

    q, k, v, yc = _inproj(x2, row2(g_mix), w_in_b[:, :qkv_w], w_in_b[:, qkv_w:conv_end],
                          row2(b_in[:conv_end]), gq, gk, hmat, cw, row2(conv_b), s)

    ya = _attention(q.reshape(b, s, ATTN_W), k.reshape(b, s, ATTN_W), v.reshape(b, s, ATTN_W),
                    _attn_bias(rel_bias)).reshape(t, ATTN_W)

    n_pad = LANES - N_GROUPS - N_EXPERTS
    wrt = jnp.concatenate([w_router, w_group, jnp.zeros((d, n_pad), w_group.dtype)], axis=1).astype(BF16)
    brt = row2(jnp.concatenate([b_router, b_group, jnp.zeros((n_pad,), b_group.dtype)]))
    h1, stage, route, cnt_f = _merge(x2, ya, yc, row2(g_mix), w_in_b[:, conv_end:], row2(b_in[conv_end:]),
                                     w_pa.astype(BF16), w_pc.astype(BF16), w_o.astype(BF16),
                                     row2(g_ffn), wrt, brt)

    n_tiles = t // TM
    cnt = cnt_f[:, 0, :N_EXPERTS].astype(jnp.int32)
    tile_off = jnp.cumsum(cnt, axis=1) - cnt
    tot = cnt.sum(axis=0)
    pcounts = (tot + ROW_BLK - 1) // ROW_BLK * ROW_BLK
    pends = jnp.cumsum(pcounts)
    pstarts = pends - pcounts
    cum = jnp.cumsum(cnt, axis=0) - cnt
    dst = pstarts[None, :] + cum
    n_blocks = (t * TOP_K) // ROW_BLK + N_EXPERTS
    blk_start = jnp.arange(n_blocks, dtype=jnp.int32) * ROW_BLK
    blk_e = jnp.minimum((pends[None, :] <= blk_start[:, None]).sum(axis=1), N_EXPERTS - 1).astype(jnp.int32)
    nblk = (pends[-1:] // ROW_BLK).astype(jnp.int32)
    sel = (jnp.arange(N_EXPERTS, dtype=jnp.int32)[:, None] == blk_e[None, :]).astype(jnp.int32)
    of_block = lambda a: (a[..., None] * sel).sum(axis=-2)
    base = blk_start - of_block(pstarts)
    nvalid = jnp.clip(of_block(tot) - base, 0, ROW_BLK)
    cum_e = of_block(cum)
    cnt_e = of_block(cnt)
    jlo = (cum_e + cnt_e <= base[None, :]).sum(axis=0)
    jhi = (cum_e < base[None, :] + ROW_BLK).sum(axis=0)
    part = jnp.clip(jnp.minimum(cum_e + cnt_e, base[None, :] + ROW_BLK) - jnp.maximum(cum_e, base[None, :]),
                    0, ROW_BLK)
    blk_pieces = jnp.stack(_piece_counts(part), axis=-1).sum(axis=0)
    tile_pieces = jnp.stack(_piece_counts(cnt), axis=-1).sum(axis=1)
    src = jnp.arange(n_tiles, dtype=jnp.int32)[:, None] * TILE_ROWS + tile_off - cum
    flat = lambda a: a.reshape(-1).astype(jnp.int32)

    blk_big = (part >= BIG_PIECE).any(axis=0)
    tile_big = (cnt >= BIG_PIECE).any(axis=1)

    ys = _experts(blk_e, nblk, flat(base), flat(jlo), flat(jhi), flat(nvalid), flat(blk_big), flat(blk_pieces),
                  flat(cum), flat(cum + cnt), flat(src), stage, w1, w3, w2)
    out = _combine(flat(cnt), flat(tile_off), flat(dst), flat(tile_big), flat(tile_pieces), h1, route,
                   p_i.reshape(t, PLE_DIM), row2(g_ple), w_ple_gate.astype(BF16), row2(b_ple_gate),
                   w_ple_proj.astype(BF16), ys)
    return out.reshape(b, s, d)


def kernel(x, p, g_mix, w_in, b_in, g_q, g_k, rel_bias, conv_w, conv_b, w_pa, w_pc, w_o, g_ffn, w_group, b_group, w_router, b_router, w1, w3, w2, g_ple, w_ple_gate, b_ple_gate, w_ple_proj):
    h = x
    for i in range(p.shape[0]):
        h = _layer(h, p[i], g_mix[i], w_in[i], b_in[i], g_q[i], g_k[i], rel_bias[i], conv_w[i], conv_b[i],
                   w_pa[i], w_pc[i], w_o[i], g_ffn[i], w_group[i], b_group[i], w_router[i], b_router[i],
                   w1[i], w3[i], w2[i], g_ple[i], w_ple_gate[i], b_ple_gate[i], w_ple_proj[i])
    return h
```
